```python
import jax, jax.numpy as jnp
from jax import lax
import numpy as np

D_MODEL = 1024
BATCH = 2
SEQ = 8192
DEPTH = 1
DEC_BATCH = 128
DEC_SEQ = 1
PAST_LEN = 2048
PAGE_SIZE = 128

D_MIX = D_MODEL
D_ATTN = D_MIX // 2
D_POOL = D_MIX - D_ATTN
HEAD_DIM = 64
N_HEADS = D_ATTN // HEAD_DIM
N_KV_HEADS = 2
GQA = N_HEADS // N_KV_HEADS
CMP_BLOCK = 32
CMP_STRIDE = 16
CMP_HIDDEN = 2 * HEAD_DIM
SEL_BLOCK = 64
N_SEL = 16
WINDOW = 512
Q_BLOCK = 128
N_BRANCH = 3
POOL_WINDOWS = (2, 4, 8, 16)
N_POOL_GROUPS = len(POOL_WINDOWS)
POOL_GROUP_DIM = D_POOL // N_POOL_GROUPS
POOL_STATE = max(POOL_WINDOWS) - 1
KV_W = 2 * N_KV_HEADS * HEAD_DIM
P_TOT = D_ATTN + 3 * KV_W + N_HEADS * N_BRANCH + D_ATTN + D_POOL + D_POOL
RMS_EPS = 1e-6
MASK_BIG = 1e9

kernel_name = 'nsa_pool_parallel_hybrid_step'


def rmsnorm(x, g):
    xf = x.astype(jnp.float32)
    y = xf * lax.rsqrt(jnp.mean(xf * xf, axis=-1, keepdims=True) + RMS_EPS)
    return (y * g.astype(jnp.float32)).astype(x.dtype)


def project(x, norm_g, w_in):
    n, l = x.shape[:2]
    p = jnp.einsum('nld,dp->nlp', rmsnorm(x, norm_g), w_in)
    sizes = [D_ATTN, KV_W, KV_W, KV_W, N_HEADS * N_BRANCH, D_ATTN, D_POOL]
    q, ckv, skv, wkv, gl, za, u, zp = jnp.split(p, np.cumsum(sizes).tolist(), axis=-1)
    kv_shape = (n, l, 2, N_KV_HEADS, HEAD_DIM)
    return (q.reshape(n, l, N_KV_HEADS, GQA, HEAD_DIM), ckv.reshape(kv_shape),
            skv.reshape(kv_shape), wkv.reshape(kv_shape),
            gl.reshape(n, l, N_KV_HEADS, GQA, N_BRANCH), za, u, zp)


def compress_blocks(kv, pe, w1, w2):
    n, l = kv.shape[:2]
    r = CMP_BLOCK // CMP_STRIDE
    nc = (l - CMP_BLOCK) // CMP_STRIDE + 1
    n_chunk = nc + r - 1
    chunks = kv[:, :n_chunk * CMP_STRIDE].reshape(n, n_chunk, CMP_STRIDE, 2, N_KV_HEADS, HEAD_DIM)
    pe_r = pe.reshape(2, r, CMP_STRIDE, HEAD_DIM)
    w1_r = w1.reshape(2, r, CMP_STRIDE, HEAD_DIM, CMP_HIDDEN)
    h = None
    for j in range(r):
        part = chunks[:, j:j + nc] + jnp.transpose(pe_r[:, j], (1, 0, 2))[:, :, None, :]
        term = jnp.einsum('ncsjkd,jsdh->ncjkh', part, w1_r[:, j])
        h = term if h is None else h + term
    return jnp.einsum('ncjkh,jhd->ncjkd', jax.nn.silu(h), w2)


def nsa_attention(q, gate_logits, cmp_kv, slc_kv, win_kv, q_pos0, win_pos0, win_idx0, pe, w1, w2):
    f32 = jnp.float32
    n, lq = q.shape[:2]
    lk = cmp_kv.shape[1]
    qf = q.astype(f32) * (HEAD_DIM ** -0.5)
    gate = jax.nn.sigmoid(gate_logits.astype(f32))
    kv_c = compress_blocks(cmp_kv.astype(f32), pe, w1, w2)
    k_c, v_c = kv_c[:, :, 0], kv_c[:, :, 1]
    nc = kv_c.shape[1]
    c_start = jnp.arange(nc) * CMP_STRIDE
    c_last = c_start + CMP_BLOCK - 1
    ns = -(-lk // SEL_BLOCK)
    n_top = min(N_SEL, ns)
    s_start = jnp.arange(ns) * SEL_BLOCK
    cover = ((c_start[:, None] < s_start[None, :] + SEL_BLOCK)
             & (c_start[:, None] + CMP_BLOCK > s_start[None, :])).astype(f32)
    slc = jnp.pad(slc_kv.astype(f32), ((0, 0), (0, ns * SEL_BLOCK - lk), (0, 0), (0, 0), (0, 0)))
    k_s = jnp.transpose(slc[:, :, 0], (0, 2, 1, 3))
    v_s = jnp.transpose(slc[:, :, 1], (0, 2, 1, 3))
    win = jnp.pad(win_kv.astype(f32), ((0, 0), (WINDOW, 0), (0, 0), (0, 0), (0, 0)))
    qb = Q_BLOCK if lq % Q_BLOCK == 0 else lq
    n_blk = lq // qb
    gather_rows = jax.vmap(jax.vmap(lambda rows, idx: rows[idx]))

    def block(bi):
        qs = bi * qb
        qq = lax.dynamic_slice_in_dim(qf, qs, qb, axis=1)
        gg = lax.dynamic_slice_in_dim(gate, qs, qb, axis=1)
        t = q_pos0 + qs + jnp.arange(qb)
        s_c = jnp.einsum('nqkgd,nckd->nkgqc', qq, k_c)
        ok_c = c_last[None, :] <= t[:, None]
        p_c = jnp.where(ok_c, jax.nn.softmax(jnp.where(ok_c, s_c, -MASK_BIG), axis=-1), 0.0)
        o_c = jnp.einsum('nkgqc,nckd->nqkgd', p_c, v_c)
        imp = jnp.einsum('nkgqc,cs->nkqs', p_c, cover)
        blk = jnp.arange(ns)[None, :]
        cur = (t // SEL_BLOCK)[:, None]
        forced = (blk == 0) | (blk == cur) | (blk == cur - 1)
        imp = jnp.where(forced, MASK_BIG, jnp.where(s_start[None, :] <= t[:, None], imp, -MASK_BIG))
        _, top = lax.top_k(imp, n_top)
        tok = (top[..., None] * SEL_BLOCK + jnp.arange(SEL_BLOCK)).reshape(n, N_KV_HEADS, qb, n_top * SEL_BLOCK)
        kg = gather_rows(k_s, tok)
        vg = gather_rows(v_s, tok)
        s_s = jnp.einsum('nqkgd,nkqsd->nkgqs', qq, kg)
        ok_s = (tok <= t[None, None, :, None])[:, :, None]
        p_s = jax.nn.softmax(jnp.where(ok_s, s_s, -MASK_BIG), axis=-1)
        o_s = jnp.einsum('nkgqs,nkqsd->nqkgd', p_s, vg)
        wrows = lax.dynamic_slice_in_dim(win, win_idx0 + qs, WINDOW + qb, axis=1)
        kp = win_pos0 + win_idx0 + qs - WINDOW + jnp.arange(WINDOW + qb)
        ok_w = (kp[None, :] >= 0) & (kp[None, :] <= t[:, None]) & (kp[None, :] > t[:, None] - WINDOW)
        s_w = jnp.einsum('nqkgd,nskd->nkgqs', qq, wrows[:, :, 0])
        p_w = jax.nn.softmax(jnp.where(ok_w, s_w, -MASK_BIG), axis=-1)
        o_w = jnp.einsum('nkgqs,nskd->nqkgd', p_w, wrows[:, :, 1])
        return gg[..., 0:1] * o_c + gg[..., 1:2] * o_s + gg[..., 2:3] * o_w

    out = lax.map(block, jnp.arange(n_blk))
    return jnp.moveaxis(out, 0, 1).reshape(n, lq, N_KV_HEADS * GQA * HEAD_DIM)


def pool_mixer(u_ext, n_out, w_pool, pool_scale):
    f32 = jnp.float32
    uf = u_ext.astype(f32)
    n, le, _ = uf.shape
    csum = jnp.concatenate([jnp.zeros((n, 1, D_POOL), f32), jnp.cumsum(uf, axis=1)], axis=1)
    win = jnp.repeat(jnp.array(POOL_WINDOWS, jnp.int32), POOL_GROUP_DIM)
    end = le - n_out + 1 + jnp.arange(n_out)
    start = jnp.maximum(end[:, None] - win[None, :], 0)
    lo = jnp.take_along_axis(csum, jnp.broadcast_to(start[None], (n, n_out, D_POOL)), axis=1)
    mean = (csum[:, le - n_out + 1:] - lo) / (end[:, None] - start).astype(f32)
    d = (mean - uf[:, le - n_out:]).reshape(n, n_out, N_POOL_GROUPS, POOL_GROUP_DIM)
    y = jnp.einsum('nlgc,gcd->nlgd', d, w_pool).reshape(n, n_out, D_POOL)
    return y * pool_scale


def combine(x, o_attn, za, o_pool, zp, w_out):
    a = o_attn * jax.nn.silu(za.astype(jnp.float32))
    b = o_pool * jax.nn.silu(zp.astype(jnp.float32))
    mix = jnp.concatenate([a, b], axis=-1).astype(x.dtype)
    return x + jnp.einsum('nlm,md->nld', mix, w_out)


def gather_pages(cache, page_table):
    g = cache[page_table]
    return g.reshape(g.shape[0], g.shape[1] * g.shape[2], *g.shape[3:])


def setup_inputs(seed: int = 0) -> dict:
    key = jax.random.key(seed)
    ks = jax.random.split(key, 20)
    f = jnp.float32
    n_pages = PAST_LEN // PAGE_SIZE
    n_used = DEC_BATCH * n_pages
    n_phys = (5 * n_used) // 4
    win_c = min(WINDOW, PAST_LEN)
    nrm = lambda k, s: jax.random.normal(k, s, f)
    return {
        'x_prompt': nrm(ks[0], (BATCH, SEQ, D_MODEL)),
        'x_sample': nrm(ks[1], (DEC_BATCH, DEC_SEQ, D_MODEL)),
        'cache_cmp_kv': nrm(ks[2], (DEPTH, n_phys, PAGE_SIZE, 2, N_KV_HEADS, HEAD_DIM)),
        'cache_slc_kv': nrm(ks[3], (DEPTH, n_phys, PAGE_SIZE, 2, N_KV_HEADS, HEAD_DIM)),
        'cache_win_kv': nrm(ks[4], (DEPTH, DEC_BATCH, win_c, 2, N_KV_HEADS, HEAD_DIM)),
        'state_pool': nrm(ks[5], (DEPTH, DEC_BATCH, POOL_STATE, D_POOL)),
        'page_table': jax.random.permutation(ks[6], n_phys)[:n_used].reshape(DEC_BATCH, n_pages).astype(jnp.int32),
        'norm_g': 1.0 + 0.05 * nrm(ks[7], (DEPTH, D_MODEL)),
        'w_in': nrm(ks[8], (DEPTH, D_MODEL, P_TOT)) * D_MODEL ** -0.5,
        'cmp_pe': 0.1 * nrm(ks[9], (DEPTH, 2, CMP_BLOCK, HEAD_DIM)),
        'cmp_w1': nrm(ks[10], (DEPTH, 2, CMP_BLOCK, HEAD_DIM, CMP_HIDDEN)) * (CMP_BLOCK * HEAD_DIM) ** -0.5,
        'cmp_w2': nrm(ks[11], (DEPTH, 2, CMP_HIDDEN, HEAD_DIM)) * CMP_HIDDEN ** -0.5,
        'pool_w': nrm(ks[12], (DEPTH, N_POOL_GROUPS, POOL_GROUP_DIM, POOL_GROUP_DIM)) * POOL_GROUP_DIM ** -0.5,
        'pool_scale': 1.0 + 0.05 * nrm(ks[13], (DEPTH, D_POOL)),
        'w_out': nrm(ks[14], (DEPTH, D_MIX, D_MODEL)) * D_MIX ** -0.5,
        'final_g': 1.0 + 0.05 * nrm(ks[15], (D_MODEL,)),
    }


def reference(x_prompt, x_sample, cache_cmp_kv, cache_slc_kv, cache_win_kv, state_pool, page_table,
              norm_g, w_in, cmp_pe, cmp_w1, cmp_w2, pool_w, pool_scale, w_out, final_g):
    win_p = min(WINDOW, SEQ)
    win_c = cache_win_kv.shape[2]
    xp, xs = x_prompt, x_sample
    p_cmp, p_slc, p_win, p_pool = [], [], [], []
    s_cmp, s_slc, s_win, s_pool = [], [], [], []
    for l in range(DEPTH):
        q, ckv, skv, wkv, gl, za, u, zp = project(xp, norm_g[l], w_in[l])
        o_a = nsa_attention(q, gl, ckv, skv, wkv, 0, 0, 0, cmp_pe[l], cmp_w1[l], cmp_w2[l])
        o_p = pool_mixer(u, SEQ, pool_w[l], pool_scale[l])
        xp = combine(xp, o_a, za, o_p, zp, w_out[l])
        p_cmp.append(ckv)
        p_slc.append(skv)
        p_win.append(wkv[:, SEQ - win_p:])
        p_pool.append(u[:, SEQ - POOL_STATE:])
        q, ckv, skv, wkv, gl, za, u, zp = project(xs, norm_g[l], w_in[l])
        cmp_all = jnp.concatenate([gather_pages(cache_cmp_kv[l], page_table), ckv], axis=1)
        slc_all = jnp.concatenate([gather_pages(cache_slc_kv[l], page_table), skv], axis=1)
        win_all = jnp.concatenate([cache_win_kv[l], wkv], axis=1)
        o_a = nsa_attention(q, gl, cmp_all, slc_all, win_all, PAST_LEN, PAST_LEN - win_c, win_c,
                            cmp_pe[l], cmp_w1[l], cmp_w2[l])
        u_all = jnp.concatenate([state_pool[l], u], axis=1)
        o_p = pool_mixer(u_all, DEC_SEQ, pool_w[l], pool_scale[l])
        xs = combine(xs, o_a, za, o_p, zp, w_out[l])
        s_cmp.append(ckv)
        s_slc.append(skv)
        s_win.append(win_all[:, win_all.shape[1] - win_c:])
        s_pool.append(u_all[:, u_all.shape[1] - POOL_STATE:])
    y_prompt = rmsnorm(xp, final_g)
    y_sample = rmsnorm(xs, final_g)
    new_cmp_kv_prompt = jnp.stack(p_cmp)
    new_slc_kv_prompt = jnp.stack(p_slc)
    new_win_kv_prompt = jnp.stack(p_win)
    new_pool_prompt = jnp.stack(p_pool)
    new_cmp_kv_sample = jnp.stack(s_cmp)
    new_slc_kv_sample = jnp.stack(s_slc)
    new_win_kv_sample = jnp.stack(s_win)
    new_pool_sample = jnp.stack(s_pool)
    return (y_prompt, y_sample, new_cmp_kv_prompt, new_slc_kv_prompt, new_win_kv_prompt, new_pool_prompt,
            new_cmp_kv_sample, new_slc_kv_sample, new_win_kv_sample, new_pool_sample)
```

```python
import functools

import jax
import jax.numpy as jnp
from jax import lax
from jax.experimental import pallas as pl
from jax.experimental.pallas import tpu as pltpu

F32 = jnp.float32
BF16 = jnp.bfloat16

HEAD_DIM = 64
N_KV_HEADS = 2
GQA = 4
N_HEADS = N_KV_HEADS * GQA
D_ATTN = N_HEADS * HEAD_DIM
KV_W = 2 * N_KV_HEADS * HEAD_DIM
N_BRANCH = 3
D_POOL = 512
CMP_BLOCK = 32
CMP_STRIDE = 16
CMP_HIDDEN = 128
SEL_BLOCK = 64
N_SEL = 16
WINDOW = 512
Q_BLOCK = 128
POOL_WINDOWS = (2, 4, 8, 16)
POOL_GROUP_DIM = 128
POOL_STATE = 15
RMS_EPS = 1e-6
LANES = 128
NEG_BIAS = -(2.0 ** 30)
NEG_FILL = -1e30
GATE_COLS = 2 * LANES
P_PAD = D_ATTN + 3 * KV_W + 3 * 512 + GATE_COLS
VMEM_LIMIT = 48 * 1024 * 1024


def _cparams(sem):
    return pltpu.CompilerParams(dimension_semantics=sem, vmem_limit_bytes=VMEM_LIMIT)


def _silu(z):
    return z * jax.nn.sigmoid(z)


def _proj_kernel(x_ref, g_ref, w_ref, q_ref, crm_ref, ct_ref, st_ref, wt_ref, kst_ref, vs_ref, kwt_ref, vw_ref,
                 gate_ref, sza_ref, u_ref, szp_ref, *, tm, seq_len):
    i = pl.program_id(0)
    x = x_ref[...]
    ms = jnp.mean(x * x, axis=-1, keepdims=True)
    xn = (x * lax.rsqrt(ms + RMS_EPS) * g_ref[...]).astype(BF16)

    def mm(c0, c1):
        return jnp.dot(xn, w_ref[:, c0:c1], preferred_element_type=F32)

    q_ref[...] = (mm(0, D_ATTN) * (HEAD_DIM ** -0.5)).astype(BF16)
    kv = mm(D_ATTN, D_ATTN + 3 * KV_W)
    crm_ref[0] = kv[:, 0:LANES]
    crm_ref[1] = kv[:, LANES:2 * LANES]
    kvt = kv.T
    ct_ref[...] = kvt[0:KV_W, :]
    st_ref[...] = kvt[KV_W:2 * KV_W, :]
    wt_ref[...] = kvt[2 * KV_W:3 * KV_W, :]

    lane = lax.broadcasted_iota(jnp.int32, (tm, LANES), 1)
    ones_col = jnp.where(lane == HEAD_DIM, 1.0, 0.0).astype(F32)
    pos = (i * tm) % seq_len + lax.broadcasted_iota(jnp.int32, (HEAD_DIM, tm), 1)
    hot = lax.broadcasted_iota(jnp.int32, (HEAD_DIM, tm), 0) == (pos // SEL_BLOCK) % HEAD_DIM
    onehot = jnp.where(hot, NEG_BIAS, 0.0).astype(BF16)
    for kvh in range(N_KV_HEADS):
        ks_t = kvt[KV_W + kvh * HEAD_DIM:KV_W + (kvh + 1) * HEAD_DIM, :].astype(BF16)
        kw_t = kvt[2 * KV_W + kvh * HEAD_DIM:2 * KV_W + (kvh + 1) * HEAD_DIM, :].astype(BF16)
        for j in range(tm // LANES):
            kst_ref[kvh, j, 0:HEAD_DIM, :] = ks_t[:, j * LANES:(j + 1) * LANES]
            kst_ref[kvh, j, HEAD_DIM:2 * HEAD_DIM, :] = onehot[:, j * LANES:(j + 1) * LANES]
            kwt_ref[kvh, j, :, :] = kw_t[:, j * LANES:(j + 1) * LANES]
        for (src0, dst) in ((KV_W + LANES, vs_ref), (2 * KV_W + LANES, vw_ref)):
            slab = kv[:, src0:src0 + LANES]
            if kvh == 1:
                slab = pltpu.roll(slab, HEAD_DIM, 1)
            dst[kvh, :, :] = jnp.where(lane < HEAD_DIM, slab, ones_col).astype(BF16)

    c = D_ATTN + 3 * KV_W
    sza_ref[...] = _silu(mm(c, c + 512))
    u_ref[...] = mm(c + 512, c + 1024)
    szp_ref[...] = _silu(mm(c + 1024, c + 1536))
    gate_ref[...] = jax.nn.sigmoid(mm(c + 1536, c + 1536 + GATE_COLS))


def _project(x2d, norm_g, w_r, *, tm, seq_len):
    rows = x2d.shape[0]
    nt = rows // tm
    jt = tm // LANES
    tps = seq_len // tm
    row_blk = lambda w: pl.BlockSpec((tm, w), lambda i: (i, 0))
    kv_t = jax.ShapeDtypeStruct((rows // seq_len, KV_W, seq_len), F32)
    kv_t_blk = pl.BlockSpec((None, KV_W, tm), lambda i: (i // tps, 0, i % tps))
    out_shape = (
        jax.ShapeDtypeStruct((rows, D_ATTN), BF16),
        jax.ShapeDtypeStruct((2, rows, LANES), F32),
        kv_t, kv_t, kv_t,
        jax.ShapeDtypeStruct((N_KV_HEADS, rows // LANES, 2 * HEAD_DIM, LANES), BF16),
        jax.ShapeDtypeStruct((N_KV_HEADS, rows, LANES), BF16),
        jax.ShapeDtypeStruct((N_KV_HEADS, rows // LANES, HEAD_DIM, LANES), BF16),
        jax.ShapeDtypeStruct((N_KV_HEADS, rows, LANES), BF16),
        jax.ShapeDtypeStruct((rows, GATE_COLS), F32),
        jax.ShapeDtypeStruct((rows, 512), F32),
        jax.ShapeDtypeStruct((rows, 512), F32),
        jax.ShapeDtypeStruct((rows, 512), F32),
    )
    out_specs = (
        row_blk(D_ATTN),
        pl.BlockSpec((2, tm, LANES), lambda i: (0, i, 0)),
        kv_t_blk, kv_t_blk, kv_t_blk,
        pl.BlockSpec((N_KV_HEADS, jt, 2 * HEAD_DIM, LANES), lambda i: (0, i, 0, 0)),
        pl.BlockSpec((N_KV_HEADS, tm, LANES), lambda i: (0, i, 0)),
        pl.BlockSpec((N_KV_HEADS, jt, HEAD_DIM, LANES), lambda i: (0, i, 0, 0)),
        pl.BlockSpec((N_KV_HEADS, tm, LANES), lambda i: (0, i, 0)),
        row_blk(GATE_COLS),
        row_blk(512), row_blk(512), row_blk(512),
    )
    return pl.pallas_call(
        functools.partial(_proj_kernel, tm=tm, seq_len=seq_len),
        grid=(nt,),
        in_specs=[row_blk(x2d.shape[1]),
                  pl.BlockSpec((1, x2d.shape[1]), lambda i: (0, 0)),
                  pl.BlockSpec(w_r.shape, lambda i: (0, 0))],
        out_specs=out_specs,
        out_shape=out_shape,
        compiler_params=_cparams(("arbitrary",)),
        name="proj",
    )(x2d, norm_g, w_r)


def _compress_hidden(load_rows, pe_ref, w1_ref, kv, m):
    xs = [load_rows(s) for s in range(CMP_STRIDE)]
    hs = []
    for sub in range(CMP_BLOCK // CMP_STRIDE):
        lhs = jnp.concatenate([(xs[s] + pe_ref[kv, sub, s]).astype(BF16) for s in range(CMP_STRIDE)], axis=1)
        hs.append(jnp.dot(lhs, w1_ref[kv, sub], preferred_element_type=F32))
    return hs[0] + pltpu.roll(hs[1], m - 1, 0)


def _cover_matrix(nc_pad, ns_pad):
    c0 = lax.broadcasted_iota(jnp.int32, (nc_pad, ns_pad), 0) * CMP_STRIDE
    s0 = lax.broadcasted_iota(jnp.int32, (nc_pad, ns_pad), 1) * SEL_BLOCK
    return jnp.where((c0 < s0 + SEL_BLOCK) & (c0 + CMP_BLOCK > s0), 1.0, 0.0).astype(F32)


def _compress_prompt_kernel(c_ref, pe_ref, w1_ref, w2_ref, kct_ref, rhs_ref, *, nchunk):
    lane = lax.broadcasted_iota(jnp.int32, (nchunk, LANES), 1)
    ones_col = jnp.where(lane == HEAD_DIM, 1.0, 0.0).astype(F32)
    cover = _cover_matrix(nchunk, LANES)
    for kv in range(2):
        load = lambda s: c_ref[kv, pl.ds(s, nchunk, stride=CMP_STRIDE), :]
        h = _compress_hidden(load, pe_ref, w1_ref, kv, nchunk)
        full = jnp.dot(_silu(h).astype(BF16), w2_ref[kv], preferred_element_type=F32)
        if kv == 0:
            full_t = full.T
            for kvh in range(N_KV_HEADS):
                kct_ref[kvh, :, :] = full_t[kvh * HEAD_DIM:(kvh + 1) * HEAD_DIM, :].astype(BF16)
        else:
            for kvh in range(N_KV_HEADS):
                slab = full if kvh == 0 else pltpu.roll(full, HEAD_DIM, 1)
                rhs_ref[kvh, :, 0:LANES] = jnp.where(lane < HEAD_DIM, slab, ones_col).astype(BF16)
                rhs_ref[kvh, :, LANES:2 * LANES] = cover.astype(BF16)


def _compress_prompt(crm, n, pe_t, w1bd, w2bd):
    seq_len = crm.shape[1] // n
    nchunk = seq_len // CMP_STRIDE
    return pl.pallas_call(
        functools.partial(_compress_prompt_kernel, nchunk=nchunk),
        grid=(n,),
        in_specs=[pl.BlockSpec((2, seq_len, LANES), lambda b: (0, b, 0)),
                  pl.BlockSpec(pe_t.shape, lambda b: (0, 0, 0, 0, 0)),
                  pl.BlockSpec(w1bd.shape, lambda b: (0, 0, 0, 0)),
                  pl.BlockSpec(w2bd.shape, lambda b: (0, 0, 0))],
        out_specs=(pl.BlockSpec((None, N_KV_HEADS, HEAD_DIM, nchunk), lambda b: (b, 0, 0, 0)),
                   pl.BlockSpec((None, N_KV_HEADS, nchunk, 2 * LANES), lambda b: (b, 0, 0, 0))),
        out_shape=(jax.ShapeDtypeStruct((n, N_KV_HEADS, HEAD_DIM, nchunk), BF16),
                   jax.ShapeDtypeStruct((n, N_KV_HEADS, nchunk, 2 * LANES), BF16)),
        compiler_params=_cparams(("arbitrary",)),
        name="compress_prompt",
    )(crm, pe_t, w1bd, w2bd)


def _not_selected(val, n_top):
    lane = lax.broadcasted_iota(jnp.int32, val.shape, 1).astype(F32)
    notsel = jnp.ones(val.shape, F32)
    for _ in range(n_top):
        m = jnp.max(val, axis=1, keepdims=True)
        idx = jnp.min(jnp.where(val == m, lane, float(LANES)), axis=1, keepdims=True)
        pick = lane == idx
        notsel = jnp.where(pick, 0.0, notsel)
        val = jnp.where(pick, -3e38, val)
    return notsel


def _masked_importance(imp, t, n_blocks):
    blk = lax.broadcasted_iota(jnp.int32, imp.shape, 1)
    cur = t // SEL_BLOCK
    forced = (blk == 0) | (blk == cur) | (blk == cur - 1)
    val = jnp.where(forced, 1e9, jnp.where(blk * SEL_BLOCK <= t, imp, -1e9))
    return jnp.where(blk < n_blocks, val, -2e9)


def _attn_prompt_kernel(q_ref, gate_ref, sza_ref, kct_ref, rhs_ref, kst_ref, vs_ref, kwt_ref, vw_ref, out_ref,
                        qlo_ref, qhi_ref, qp_ref, m_ref, acc_ref, mw_ref, accw_ref, *, n_sel_blocks):
    qb = pl.program_id(2)
    qs = qb * Q_BLOCK
    row = lax.broadcasted_iota(jnp.int32, (Q_BLOCK, LANES), 0)
    col = lax.broadcasted_iota(jnp.int32, (Q_BLOCK, LANES), 1)
    qf = q_ref[...].astype(F32)
    heads = [qf[:, g * HEAD_DIM:(g + 1) * HEAD_DIM] for g in range(GQA)]

    nc = kct_ref.shape[1]
    t_c = qs + lax.broadcasted_iota(jnp.int32, (Q_BLOCK, nc), 0)
    c_last = lax.broadcasted_iota(jnp.int32, (Q_BLOCK, nc), 1) * CMP_STRIDE + (CMP_BLOCK - 1)
    ok_c = c_last <= t_c
    kct = kct_ref[...]
    rhs = rhs_ref[...]
    imp = jnp.zeros((Q_BLOCK, LANES), F32)
    o_c = []
    for g in range(GQA):
        s = jnp.dot(heads[g].astype(BF16), kct, preferred_element_type=F32)
        s = jnp.where(ok_c, s, NEG_FILL)
        e = jnp.where(ok_c, jnp.exp(s - jnp.max(s, axis=1, keepdims=True)), 0.0)
        r = jnp.dot(e.astype(BF16), rhs, preferred_element_type=F32)
        inv = 1.0 / jnp.maximum(r[:, HEAD_DIM:HEAD_DIM + 1], 1e-30)
        o_c.append(r[:, 0:HEAD_DIM] * inv)
        imp = imp + r[:, LANES:2 * LANES] * inv

    t = qs + row[:, 0:1]
    notsel = _not_selected(_masked_importance(imp, t, n_sel_blocks), min(N_SEL, n_sel_blocks))
    for g in range(GQA):
        qlo_ref[g] = jnp.concatenate([heads[g], notsel[:, 0:HEAD_DIM]], axis=1).astype(BF16)
        qhi_ref[g] = jnp.concatenate([heads[g], notsel[:, HEAD_DIM:2 * HEAD_DIM]], axis=1).astype(BF16)
        qp_ref[g] = heads[g].astype(BF16)

    def flash_tile(q_of_g, k_t, v, mask, mref, aref):
        for g in range(GQA):
            s = jnp.dot(q_of_g(g), k_t, preferred_element_type=F32)
            if mask is not None:
                s = jnp.where(mask, s, NEG_FILL)
            m_old = mref[g]
            m_new = jnp.maximum(m_old, jnp.max(s, axis=1, keepdims=True))
            p = jnp.exp(s - m_new)
            aref[g] = jnp.exp(m_old - m_new) * aref[g] + jnp.dot(p.astype(BF16), v, preferred_element_type=F32)
            mref[g] = m_new

    m_ref[...] = jnp.full(m_ref.shape, NEG_FILL, F32)
    acc_ref[...] = jnp.zeros(acc_ref.shape, F32)
    mw_ref[...] = jnp.full(mw_ref.shape, NEG_FILL, F32)
    accw_ref[...] = jnp.zeros(accw_ref.shape, F32)

    def sel_tile(qref, kt, mask):
        start = pl.multiple_of(kt * LANES, LANES)
        flash_tile(lambda g: qref[g], kst_ref[kt], vs_ref[pl.ds(start, LANES), :], mask, m_ref, acc_ref)

    half = HEAD_DIM // (LANES // SEL_BLOCK)

    def body_lo(kt, carry):
        sel_tile(qlo_ref, kt, None)
        return carry

    def body_hi(kt, carry):
        sel_tile(qhi_ref, kt, None)
        return carry

    lax.fori_loop(0, jnp.minimum(qb, half), body_lo, 0)
    lax.fori_loop(half, jnp.maximum(qb, half), body_hi, 0)
    causal = col <= row

    @pl.when(qb < half)
    def _():
        sel_tile(qlo_ref, qb, causal)

    @pl.when(qb >= half)
    def _():
        sel_tile(qhi_ref, qb, causal)

    def win_tile(kt, mask):
        start = pl.multiple_of(kt * LANES, LANES)
        flash_tile(lambda g: qp_ref[g], kwt_ref[kt], vw_ref[pl.ds(start, LANES), :], mask, mw_ref, accw_ref)

    n_back = WINDOW // Q_BLOCK
    win_tile(qb, causal)
    for j in range(1, n_back + 1):
        @pl.when(qb >= j)
        def _(j=j):
            win_tile(qb - j, (col > row) if j == n_back else None)

    gates = gate_ref[...]
    sza = sza_ref[...]
    outs = []
    for g in range(GQA):
        a_s = acc_ref[g]
        a_w = accw_ref[g]
        o_s = a_s[:, 0:HEAD_DIM] * (1.0 / a_s[:, HEAD_DIM:HEAD_DIM + 1])
        o_w = a_w[:, 0:HEAD_DIM] * (1.0 / a_w[:, HEAD_DIM:HEAD_DIM + 1])
        gc = gates[:, g * N_BRANCH + 0:g * N_BRANCH + 1]
        gs = gates[:, g * N_BRANCH + 1:g * N_BRANCH + 2]
        gw = gates[:, g * N_BRANCH + 2:g * N_BRANCH + 3]
        outs.append((gc * o_c[g] + gs * o_s + gw * o_w) * sza[:, g * HEAD_DIM:(g + 1) * HEAD_DIM])
    out_ref[...] = jnp.concatenate(outs, axis=1).astype(BF16)


def _attn_prompt(q3, gates3, sza3, kct, rhs_c, kst, vs, kwt, vw):
    n, seq_len, _ = q3.shape
    nqb = seq_len // Q_BLOCK
    nchunk = kct.shape[-1]
    hw = GQA * HEAD_DIM
    return pl.pallas_call(
        functools.partial(_attn_prompt_kernel, n_sel_blocks=-(-seq_len // SEL_BLOCK)),
        grid=(n, N_KV_HEADS, nqb),
        in_specs=[
            pl.BlockSpec((None, Q_BLOCK, hw), lambda b, k, i: (b, i, k)),
            pl.BlockSpec((None, Q_BLOCK, LANES), lambda b, k, i: (b, i, k)),
            pl.BlockSpec((None, Q_BLOCK, hw), lambda b, k, i: (b, i, k)),
            pl.BlockSpec((None, None, HEAD_DIM, nchunk), lambda b, k, i: (b, k, 0, 0)),
            pl.BlockSpec((None, None, nchunk, 2 * LANES), lambda b, k, i: (b, k, 0, 0)),
            pl.BlockSpec((None, nqb, 2 * HEAD_DIM, LANES), lambda b, k, i: (k, b, 0, 0)),
            pl.BlockSpec((None, seq_len, LANES), lambda b, k, i: (k, b, 0)),
            pl.BlockSpec((None, nqb, HEAD_DIM, LANES), lambda b, k, i: (k, b, 0, 0)),
            pl.BlockSpec((None, seq_len, LANES), lambda b, k, i: (k, b, 0)),
        ],
        out_specs=pl.BlockSpec((None, Q_BLOCK, hw), lambda b, k, i: (b, i, k)),
        out_shape=jax.ShapeDtypeStruct((n, seq_len, D_ATTN), BF16),
        scratch_shapes=[
            pltpu.VMEM((GQA, Q_BLOCK, LANES), BF16),
            pltpu.VMEM((GQA, Q_BLOCK, LANES), BF16),
            pltpu.VMEM((GQA, Q_BLOCK, HEAD_DIM), BF16),
            pltpu.VMEM((GQA, Q_BLOCK, 1), F32),
            pltpu.VMEM((GQA, Q_BLOCK, LANES), F32),
            pltpu.VMEM((GQA, Q_BLOCK, 1), F32),
            pltpu.VMEM((GQA, Q_BLOCK, LANES), F32),
        ],
        compiler_params=_cparams(("arbitrary", "arbitrary", "arbitrary")),
        name="attn_prompt",
    )(q3, gates3, sza3, kct, rhs_c, kst, vs, kwt, vw)


def _pool_out(d, pw_ref, ps_ref, szp):
    ys = [jnp.dot(d[:, g * POOL_GROUP_DIM:(g + 1) * POOL_GROUP_DIM].astype(BF16), pw_ref[g],
                  preferred_element_type=F32) for g in range(len(POOL_WINDOWS))]
    return jnp.concatenate(ys, axis=1) * ps_ref[...] * szp


def _finish(x, a_bf16, b, wo_ref, fg_ref):
    mix = jnp.concatenate([a_bf16, b.astype(BF16)], axis=1)
    y = x + jnp.dot(mix, wo_ref[...], preferred_element_type=F32)
    ms = jnp.mean(y * y, axis=-1, keepdims=True)
    return y * lax.rsqrt(ms + RMS_EPS) * fg_ref[...]


def _out_prompt_kernel(a_ref, u_ref, halo_ref, szp_ref, x_ref, pw_ref, ps_ref, wo_ref, fg_ref, y_ref, ext_ref,
                       *, tm, halo):
    i = pl.program_id(1)
    u = u_ref[...]
    ext_ref[0:halo, :] = jnp.where(i > 0, halo_ref[...], 0.0)
    ext_ref[halo:halo + tm, :] = u
    pos = i * tm + lax.broadcasted_iota(jnp.int32, (tm, POOL_GROUP_DIM), 0)
    ds = []
    for g, w in enumerate(POOL_WINDOWS):
        c0 = g * POOL_GROUP_DIM
        acc = u[:, c0:c0 + POOL_GROUP_DIM]
        for k in range(1, w):
            acc = acc + ext_ref[halo - k:halo - k + tm, c0:c0 + POOL_GROUP_DIM]
        cnt = jnp.minimum(pos + 1, w).astype(F32)
        ds.append(acc / cnt - u[:, c0:c0 + POOL_GROUP_DIM])
    b = _pool_out(jnp.concatenate(ds, axis=1), pw_ref, ps_ref, szp_ref[...])
    y_ref[...] = _finish(x_ref[...], a_ref[...], b, wo_ref, fg_ref)


def _out_prompt(a3, u3, szp3, x3, pool_w, pool_scale, w_out, final_g, *, tm):
    n, seq_len, d_model = x3.shape
    halo = 16
    nt = seq_len // tm
    blk = lambda w: pl.BlockSpec((None, tm, w), lambda b, i: (b, i, 0))
    const = lambda a: pl.BlockSpec(a.shape, lambda b, i: (0,) * a.ndim)
    return pl.pallas_call(
        functools.partial(_out_prompt_kernel, tm=tm, halo=halo),
        grid=(n, nt),
        in_specs=[blk(D_ATTN), blk(D_POOL),
                  pl.BlockSpec((None, halo, D_POOL), lambda b, i: (b, jnp.maximum(i * (tm // halo) - 1, 0), 0)),
                  blk(D_POOL), blk(d_model),
                  const(pool_w), const(pool_scale), const(w_out), const(final_g)],
        out_specs=blk(d_model),
        out_shape=jax.ShapeDtypeStruct((n, seq_len, d_model), F32),
        scratch_shapes=[pltpu.VMEM((tm + halo, D_POOL), F32)],
        compiler_params=_cparams(("arbitrary", "arbitrary")),
        name="out_prompt",
    )(a3, u3, u3, szp3, x3, pool_w, pool_scale, w_out, final_g)


def _out_sample_kernel(o_ref, sza_ref, u_ref, st_ref, szp_ref, x_ref, pw_ref, ps_ref, wo_ref, fg_ref, y_ref):
    u = u_ref[...]
    ds = []
    for g, w in enumerate(POOL_WINDOWS):
        c0 = g * POOL_GROUP_DIM
        acc = u[:, c0:c0 + POOL_GROUP_DIM]
        for k in range(1, w):
            acc = acc + st_ref[POOL_STATE - k, :, c0:c0 + POOL_GROUP_DIM]
        ds.append(acc / float(w) - u[:, c0:c0 + POOL_GROUP_DIM])
    b = _pool_out(jnp.concatenate(ds, axis=1), pw_ref, ps_ref, szp_ref[...])
    a = (o_ref[...] * sza_ref[...]).astype(BF16)
    y_ref[...] = _finish(x_ref[...], a, b, wo_ref, fg_ref)


def _out_sample(o2, sza, u, state_t, szp, x2, pool_w, pool_scale, w_out, final_g):
    args = (o2, sza, u, state_t, szp, x2, pool_w, pool_scale, w_out, final_g)
    full = lambda a: pl.BlockSpec(a.shape, lambda i: (0,) * a.ndim)
    return pl.pallas_call(
        _out_sample_kernel,
        grid=(1,),
        in_specs=[full(a) for a in args],
        out_specs=full(x2),
        out_shape=jax.ShapeDtypeStruct(x2.shape, F32),
        compiler_params=_cparams(("arbitrary",)),
        name="out_sample",
    )(*args)


def _cmp_sample_kernel(pt_ref, *refs, n_pages, page):
    del pt_ref
    pages = refs[:n_pages]
    qbd_ref, pe_ref, w1_ref, w2_ref, oc_ref, imp_ref, rows_ref = refs[n_pages:]
    nchunk = n_pages * page // CMP_STRIDE
    for p in range(n_pages):
        for kv in range(2):
            rows_ref[kv, p * page:(p + 1) * page, :] = pages[p][kv * LANES:(kv + 1) * LANES, :].T
    fulls = []
    for kv in range(2):
        load = lambda s: rows_ref[kv, pl.ds(s, nchunk, stride=CMP_STRIDE), :]
        h = _compress_hidden(load, pe_ref, w1_ref, kv, nchunk)
        fulls.append(jnp.dot(_silu(h).astype(BF16), w2_ref[kv], preferred_element_type=F32))
    k_c, v_c = fulls
    qbd = qbd_ref[...]
    s = lax.dot_general(qbd, k_c.astype(BF16), (((1,), (1,)), ((), ())), preferred_element_type=F32)
    ok = lax.broadcasted_iota(jnp.int32, s.shape, 1) < nchunk - 1
    s = jnp.where(ok, s, NEG_FILL)
    e = jnp.where(ok, jnp.exp(s - jnp.max(s, axis=1, keepdims=True)), 0.0)
    pc = (e * (1.0 / jnp.sum(e, axis=1, keepdims=True))).astype(BF16)
    oc_ref[...] = jnp.dot(pc, v_c.astype(BF16), preferred_element_type=F32)
    imp_ref[...] = jnp.dot(pc, _cover_matrix(nchunk, LANES).astype(BF16), preferred_element_type=F32)


def _cmp_sample(page_table, cache_t, qbd, pe_t, w1bd, w2bd):
    nb, n_pages = page_table.shape
    page = cache_t.shape[-1]
    page_spec = lambda p: pl.BlockSpec((None, KV_W, page), lambda b, pt, p=p: (pt[b, p], 0, 0))
    const = lambda a: pl.BlockSpec(a.shape, lambda b, pt: (0,) * a.ndim)
    per_b = pl.BlockSpec((None, N_HEADS, LANES), lambda b, pt: (b, 0, 0))
    grid_spec = pltpu.PrefetchScalarGridSpec(
        num_scalar_prefetch=1,
        grid=(nb,),
        in_specs=[page_spec(p) for p in range(n_pages)] + [per_b, const(pe_t), const(w1bd), const(w2bd)],
        out_specs=(per_b, per_b),
        scratch_shapes=[pltpu.VMEM((2, n_pages * page, LANES), F32)],
    )
    return pl.pallas_call(
        functools.partial(_cmp_sample_kernel, n_pages=n_pages, page=page),
        grid_spec=grid_spec,
        out_shape=(jax.ShapeDtypeStruct((nb, N_HEADS, LANES), F32),
                   jax.ShapeDtypeStruct((nb, N_HEADS, LANES), F32)),
        compiler_params=_cparams(("arbitrary",)),
        name="cmp_sample",
    )(page_table, *([cache_t] * n_pages), qbd, pe_t, w1bd, w2bd)


def _topk_sample_kernel(imp_ref, out_ref, *, t_pos, n_blocks):
    imp = imp_ref[0] + imp_ref[1] + imp_ref[2] + imp_ref[3]
    t = jnp.full((imp.shape[0], 1), t_pos, jnp.int32)
    out_ref[...] = _not_selected(_masked_importance(imp, t, n_blocks), min(N_SEL, n_blocks))


def _topk_sample(imp_g, *, t_pos, n_blocks):
    rows = imp_g.shape[1]
    return pl.pallas_call(
        functools.partial(_topk_sample_kernel, t_pos=t_pos, n_blocks=n_blocks),
        grid=(1,),
        in_specs=[pl.BlockSpec(imp_g.shape, lambda i: (0, 0, 0))],
        out_specs=pl.BlockSpec((rows, LANES), lambda i: (0, 0)),
        out_shape=jax.ShapeDtypeStruct((rows, LANES), F32),
        compiler_params=_cparams(("arbitrary",)),
        name="topk_sample",
    )(imp_g)


def _attn_sample_kernel(pt_ref, *refs, n_pages, page):
    del pt_ref
    pages = refs[:n_pages]
    (win_ref, qbd_ref, ns_ref, snew_ref, wnew_ref, wcol_ref, oc_ref, gate_ref, o_ref, nwin_ref) = refs[n_pages:]
    qbd = qbd_ref[...]
    qf = qbd.astype(F32)
    ns = ns_ref[...]
    nt_ = (((1,), (1,)), ((), ()))

    def new_token(row_ref):
        k_new = row_ref[:, 0:LANES].astype(BF16).astype(F32)
        v_new = row_ref[:, LANES:2 * LANES].astype(BF16).astype(F32)
        return jnp.sum(qf * k_new, axis=1, keepdims=True), v_new

    lane = lax.broadcasted_iota(jnp.int32, (N_HEADS, page), 1)
    per_page = page // SEL_BLOCK
    scores = []
    for p in range(n_pages):
        s = jnp.dot(qbd, pages[p][0:LANES, :].astype(BF16), preferred_element_type=F32)
        flag = ns[:, p * per_page:p * per_page + 1]
        for j in range(1, per_page):
            flag = jnp.where(lane >= j * SEL_BLOCK, ns[:, p * per_page + j:p * per_page + j + 1], flag)
        scores.append(jnp.where(flag > 0.5, NEG_FILL, s))
    nb_cache = n_pages * per_page
    s_new, v_new = new_token(snew_ref)
    s_new = jnp.where(ns[:, nb_cache:nb_cache + 1] > 0.5, NEG_FILL, s_new)
    m = s_new
    for s in scores:
        m = jnp.maximum(m, jnp.max(s, axis=1, keepdims=True))
    e_new = jnp.exp(s_new - m)
    l = e_new
    acc = e_new * v_new
    for p in range(n_pages):
        e = jnp.exp(scores[p] - m).astype(BF16)
        l = l + jnp.sum(e.astype(F32), axis=1, keepdims=True)
        acc = acc + lax.dot_general(e, pages[p][LANES:2 * LANES, :].astype(BF16), nt_, preferred_element_type=F32)
    o_s = acc * (1.0 / l)

    win = win_ref[...]
    wlen = win.shape[1]
    s_w = jnp.dot(qbd, win[0:LANES, :].astype(BF16), preferred_element_type=F32)
    first = max(wlen - WINDOW + 1, 0)
    s_w = jnp.where(lax.broadcasted_iota(jnp.int32, s_w.shape, 1) >= first, s_w, NEG_FILL)
    sw_new, vw_new = new_token(wnew_ref)
    m_w = jnp.maximum(sw_new, jnp.max(s_w, axis=1, keepdims=True))
    ew_new = jnp.exp(sw_new - m_w)
    e_w = jnp.exp(s_w - m_w).astype(BF16)
    l_w = ew_new + jnp.sum(e_w.astype(F32), axis=1, keepdims=True)
    acc_w = ew_new * vw_new + lax.dot_general(e_w, win[LANES:2 * LANES, :].astype(BF16), nt_,
                                              preferred_element_type=F32)
    o_w = acc_w * (1.0 / l_w)

    gates = gate_ref[...]
    o_ref[...] = gates[:, 0:1] * oc_ref[...] + gates[:, 1:2] * o_s + gates[:, 2:3] * o_w

    shifted = pltpu.roll(win, wlen - 1, 1)
    last = lax.broadcasted_iota(jnp.int32, win.shape, 1) == wlen - 1
    nwin_ref[...] = jnp.where(last, wcol_ref[...], shifted)


def _attn_sample(page_table, cache_t, win_t, qbd, ns8, s_new, w_new, w_col, o_c, gates8):
    nb, n_pages = page_table.shape
    page = cache_t.shape[-1]
    wlen = win_t.shape[-1]
    page_spec = lambda p: pl.BlockSpec((None, KV_W, page), lambda b, pt, p=p: (pt[b, p], 0, 0))
    per_b = lambda a: pl.BlockSpec((None,) + a.shape[1:], lambda b, pt: (b,) + (0,) * (a.ndim - 1))
    grid_spec = pltpu.PrefetchScalarGridSpec(
        num_scalar_prefetch=1,
        grid=(nb,),
        in_specs=[page_spec(p) for p in range(n_pages)]
        + [per_b(a) for a in (win_t, qbd, ns8, s_new, w_new, w_col, o_c, gates8)],
        out_specs=(per_b(o_c), per_b(win_t)),
    )
    return pl.pallas_call(
        functools.partial(_attn_sample_kernel, n_pages=n_pages, page=page),
        grid_spec=grid_spec,
        out_shape=(jax.ShapeDtypeStruct(o_c.shape, F32), jax.ShapeDtypeStruct(win_t.shape, F32)),
        compiler_params=_cparams(("arbitrary",)),
        name="attn_sample",
    )(page_table, *([cache_t] * n_pages), win_t, qbd, ns8, s_new, w_new, w_col, o_c, gates8)


def _prep_w_in(w_in):
    o = D_ATTN + 3 * KV_W
    ng = GQA * N_BRANCH
    gl = w_in[:, o:o + N_KV_HEADS * ng]
    rest = w_in[:, o + N_KV_HEADS * ng:]
    pad = jnp.zeros((w_in.shape[0], LANES - ng), w_in.dtype)
    return jnp.concatenate([w_in[:, :o], rest, gl[:, :ng], pad, gl[:, ng:], pad], axis=1).astype(BF16)


def _prep_compress(cmp_pe, cmp_w1, cmp_w2):
    r = CMP_BLOCK // CMP_STRIDE
    pe_t = jnp.tile(cmp_pe.reshape(2, r, CMP_STRIDE, 1, HEAD_DIM), (1, 1, 1, 1, N_KV_HEADS))
    w1 = cmp_w1.reshape(2, r, CMP_STRIDE, HEAD_DIM, CMP_HIDDEN)
    z1 = jnp.zeros_like(w1)
    top = jnp.concatenate([w1, z1], axis=-1)
    bot = jnp.concatenate([z1, w1], axis=-1)
    w1bd = jnp.concatenate([top, bot], axis=3)
    w1bd = w1bd.reshape(2, r, CMP_STRIDE * LANES, N_KV_HEADS * CMP_HIDDEN).astype(BF16)
    z2 = jnp.zeros_like(cmp_w2)
    w2bd = jnp.concatenate([jnp.concatenate([cmp_w2, z2], axis=-1),
                            jnp.concatenate([z2, cmp_w2], axis=-1)], axis=1).astype(BF16)
    return pe_t, w1bd, w2bd


def _kv_out(kv_t):
    n, _, rows = kv_t.shape
    return jnp.transpose(kv_t.reshape(n, 2, N_KV_HEADS, HEAD_DIM, rows), (0, 4, 1, 2, 3))[None]


def kernel(x_prompt, x_sample, cache_cmp_kv, cache_slc_kv, cache_win_kv, state_pool, page_table, norm_g, w_in,
           cmp_pe, cmp_w1, cmp_w2, pool_w, pool_scale, w_out, final_g):
    n, seq_len, d_model = x_prompt.shape
    nb = x_sample.shape[0]
    n_phys, page = cache_cmp_kv.shape[1], cache_cmp_kv.shape[2]
    n_pages = page_table.shape[1]
    past_len = n_pages * page
    wlen = cache_win_kv.shape[2]

    w_r = _prep_w_in(w_in[0])
    pe_t, w1bd, w2bd = _prep_compress(cmp_pe[0], cmp_w1[0], cmp_w2[0])
    pool_w_b = pool_w[0].astype(BF16)
    w_out_b = w_out[0].astype(BF16)
    fg = final_g.reshape(1, d_model)

    (q, crm, ckv_t, skv_t, wkv_t, kst, vs, kwt, vw, gates, sza, u, szp) = _project(
        x_prompt.reshape(n * seq_len, d_model), norm_g, w_r, tm=256, seq_len=seq_len)
    r3 = lambda a: a.reshape(n, seq_len, a.shape[-1])
    kct, rhs_c = _compress_prompt(crm, n, pe_t, w1bd, w2bd)
    a3 = _attn_prompt(r3(q), r3(gates), r3(sza), kct, rhs_c, kst, vs, kwt, vw)
    y_prompt = _out_prompt(a3, r3(u), r3(szp), x_prompt, pool_w_b, pool_scale, w_out_b, fg, tm=512)

    new_cmp_p = _kv_out(ckv_t)
    new_slc_p = _kv_out(skv_t)
    new_win_p = _kv_out(wkv_t[:, :, seq_len - min(WINDOW, seq_len):])
    new_pool_p = r3(u)[:, seq_len - POOL_STATE:][None]

    (q_s, _, ckv_ts, skv_ts, wkv_ts, _, _, _, _, gates_s, sza_s, u_s, szp_s) = _project(
        x_sample.reshape(nb, d_model), norm_g, w_r, tm=nb, seq_len=nb)
    q5 = q_s.reshape(nb, N_KV_HEADS, GQA, 1, HEAD_DIM)
    eye = jnp.eye(N_KV_HEADS, dtype=q_s.dtype).reshape(1, N_KV_HEADS, 1, N_KV_HEADS, 1)
    qbd = (q5 * eye).reshape(nb, N_HEADS, LANES)

    to_pages = lambda c: jnp.transpose(c[0], (0, 2, 3, 4, 1)).reshape(n_phys, KV_W, page)
    o_c, imp8 = _cmp_sample(page_table, to_pages(cache_cmp_kv), qbd, pe_t, w1bd, w2bd)
    imp_g = jnp.transpose(imp8.reshape(nb * N_KV_HEADS, GQA, LANES), (1, 0, 2))
    n_blocks = -(-(past_len + 1) // SEL_BLOCK)
    notsel = _topk_sample(imp_g, t_pos=past_len, n_blocks=n_blocks)
    ns8 = jnp.repeat(notsel.reshape(nb, N_KV_HEADS, 1, LANES), GQA, axis=2).reshape(nb, N_HEADS, LANES)

    gates8 = gates_s.reshape(nb, N_KV_HEADS, LANES)[:, :, :GQA * N_BRANCH].reshape(nb, N_HEADS, N_BRANCH)
    gates8 = jnp.pad(gates8, ((0, 0), (0, 0), (0, LANES - N_BRANCH)))
    win_t = jnp.transpose(cache_win_kv[0], (0, 2, 3, 4, 1)).reshape(nb, KV_W, wlen)
    s_new = skv_ts[0].T.reshape(nb, 1, KV_W)
    w_new = wkv_ts[0].T.reshape(nb, 1, KV_W)
    w_col = w_new.reshape(nb, KV_W, 1)
    o8, nwin_t = _attn_sample(page_table, to_pages(cache_slc_kv), win_t, qbd, ns8, s_new, w_new, w_col, o_c, gates8)
    o5 = o8.reshape(nb, N_KV_HEADS, GQA, N_KV_HEADS, HEAD_DIM)
    o2 = jnp.stack([o5[:, k, :, k, :] for k in range(N_KV_HEADS)], axis=1).reshape(nb, D_ATTN)

    state_t = jnp.transpose(state_pool[0], (1, 0, 2))
    y_sample = _out_sample(o2, sza_s, u_s, state_t, szp_s, x_sample.reshape(nb, d_model),
                           pool_w_b, pool_scale, w_out_b, fg).reshape(nb, 1, d_model)

    kv_out_s = lambda t: jnp.transpose(t.reshape(2, N_KV_HEADS, HEAD_DIM, nb, 1), (3, 4, 0, 1, 2))[None]
    new_cmp_s = kv_out_s(ckv_ts)
    new_slc_s = kv_out_s(skv_ts)
    new_win_s = jnp.transpose(nwin_t.reshape(nb, 2, N_KV_HEADS, HEAD_DIM, wlen), (0, 4, 1, 2, 3))[None]
    new_pool_s = jnp.transpose(jnp.concatenate([state_t[1:], u_s[None]], axis=0), (1, 0, 2))[None]

    return (y_prompt, y_sample, new_cmp_p, new_slc_p, new_win_p, new_pool_p,
            new_cmp_s, new_slc_s, new_win_s, new_pool_s)
```

```python
import functools

import jax
import jax.numpy as jnp
from jax import lax
from jax.experimental import pallas as pl
from jax.experimental.pallas import tpu as pltpu

F32 = jnp.float32
BF16 = jnp.bfloat16

HEAD_DIM = 64
N_KV_HEADS = 2
GQA = 4
N_HEADS = N_KV_HEADS * GQA
D_ATTN = N_HEADS * HEAD_DIM
KV_W = 2 * N_KV_HEADS * HEAD_DIM
N_BRANCH = 3
D_POOL = 512
CMP_BLOCK = 32
CMP_STRIDE = 16
CMP_HIDDEN = 128
SEL_BLOCK = 64
N_SEL = 16
WINDOW = 512
Q_BLOCK = 128
POOL_WINDOWS = (2, 4, 8, 16)
POOL_GROUP_DIM = 128
POOL_STATE = 15
RMS_EPS = 1e-6
LANES = 128
NEG_BIAS = -(2.0 ** 30)
NEG_FILL = -1e30
GATE_COLS = 2 * LANES
GATE_ROWS = 16
BF16_SUBLANES = 16
V_ROWS = HEAD_DIM + BF16_SUBLANES
KEY_TILE = 512
P_PAD = D_ATTN + 3 * KV_W + 3 * 512 + GATE_COLS
VMEM_LIMIT = 48 * 1024 * 1024


def _cparams(sem):
    return pltpu.CompilerParams(dimension_semantics=sem, vmem_limit_bytes=VMEM_LIMIT)


def _silu(z):
    return z * jax.nn.sigmoid(z)


def _proj_kernel(x_ref, g_ref, w_ref, qt_ref, crm_ref, ct_ref, st_ref, wt_ref, ks_ref, vst_ref, kw_ref, vwt_ref,
                 gatet_ref, sza_ref, u_ref, szp_ref, *, tm, seq_len):
    i = pl.program_id(0)
    x = x_ref[...]
    ms = jnp.mean(x * x, axis=-1, keepdims=True)
    xn = (x * lax.rsqrt(ms + RMS_EPS) * g_ref[...]).astype(BF16)

    def mm(c0, c1):
        return jnp.dot(xn, w_ref[:, c0:c1], preferred_element_type=F32)

    qt_ref[...] = (mm(0, D_ATTN) * (HEAD_DIM ** -0.5)).T.astype(BF16)
    kv = mm(D_ATTN, D_ATTN + 3 * KV_W)
    crm_ref[0] = kv[:, 0:LANES]
    crm_ref[1] = kv[:, LANES:2 * LANES]
    kvt = kv.T
    ct_ref[...] = kvt[0:KV_W, :]
    st_ref[...] = kvt[KV_W:2 * KV_W, :]
    wt_ref[...] = kvt[2 * KV_W:3 * KV_W, :]

    lane = lax.broadcasted_iota(jnp.int32, (tm, LANES), 1)
    pos = (i * tm) % seq_len + lax.broadcasted_iota(jnp.int32, (tm, LANES), 0)
    onehot = jnp.where(lane - HEAD_DIM == (pos // SEL_BLOCK) % HEAD_DIM, NEG_BIAS, 0.0).astype(F32)
    ones_rows = jnp.where(lax.broadcasted_iota(jnp.int32, (V_ROWS - HEAD_DIM, tm), 0) == 0, 1.0, 0.0).astype(BF16)
    for kvh in range(N_KV_HEADS):
        for (base, k_dst, v_dst, k_pad) in ((KV_W, ks_ref, vst_ref, onehot), (2 * KV_W, kw_ref, vwt_ref, 0.0)):
            slab = kv[:, base:base + LANES]
            if kvh == 1:
                slab = pltpu.roll(slab, HEAD_DIM, 1)
            k_dst[kvh, :, :] = jnp.where(lane < HEAD_DIM, slab, k_pad).astype(BF16)
            v0 = base + LANES + kvh * HEAD_DIM
            v_dst[kvh, 0:HEAD_DIM, :] = kvt[v0:v0 + HEAD_DIM, :].astype(BF16)
            v_dst[kvh, HEAD_DIM:V_ROWS, :] = ones_rows

    c = D_ATTN + 3 * KV_W
    sza_ref[...] = _silu(mm(c, c + 512))
    u_ref[...] = mm(c + 512, c + 1024)
    szp_ref[...] = _silu(mm(c + 1024, c + 1536))
    gate_t = jax.nn.sigmoid(mm(c + 1536, c + 1536 + GATE_COLS)).T
    for kvh in range(N_KV_HEADS):
        gatet_ref[kvh, :, :] = gate_t[kvh * LANES:kvh * LANES + GATE_ROWS, :]


def _project(x2d, norm_g, w_r, *, tm, seq_len):
    rows = x2d.shape[0]
    nt = rows // tm
    tps = seq_len // tm
    row_blk = lambda w: pl.BlockSpec((tm, w), lambda i: (i, 0))
    kv_t = jax.ShapeDtypeStruct((rows // seq_len, KV_W, seq_len), F32)
    kv_t_blk = pl.BlockSpec((None, KV_W, tm), lambda i: (i // tps, 0, i % tps))
    k_rm = jax.ShapeDtypeStruct((N_KV_HEADS, rows, LANES), BF16)
    k_rm_blk = pl.BlockSpec((N_KV_HEADS, tm, LANES), lambda i: (0, i, 0))
    v_t = jax.ShapeDtypeStruct((N_KV_HEADS, V_ROWS, rows), BF16)
    v_t_blk = pl.BlockSpec((N_KV_HEADS, V_ROWS, tm), lambda i: (0, 0, i))
    out_shape = (
        jax.ShapeDtypeStruct((D_ATTN, rows), BF16),
        jax.ShapeDtypeStruct((2, rows, LANES), F32),
        kv_t, kv_t, kv_t,
        k_rm,
        v_t,
        k_rm,
        v_t,
        jax.ShapeDtypeStruct((N_KV_HEADS, GATE_ROWS, rows), F32),
        jax.ShapeDtypeStruct((rows, 512), F32),
        jax.ShapeDtypeStruct((rows, 512), F32),
        jax.ShapeDtypeStruct((rows, 512), F32),
    )
    out_specs = (
        pl.BlockSpec((D_ATTN, tm), lambda i: (0, i)),
        pl.BlockSpec((2, tm, LANES), lambda i: (0, i, 0)),
        kv_t_blk, kv_t_blk, kv_t_blk,
        k_rm_blk, v_t_blk, k_rm_blk, v_t_blk,
        pl.BlockSpec((N_KV_HEADS, GATE_ROWS, tm), lambda i: (0, 0, i)),
        row_blk(512), row_blk(512), row_blk(512),
    )
    return pl.pallas_call(
        functools.partial(_proj_kernel, tm=tm, seq_len=seq_len),
        grid=(nt,),
        in_specs=[row_blk(x2d.shape[1]),
                  pl.BlockSpec((1, x2d.shape[1]), lambda i: (0, 0)),
                  pl.BlockSpec(w_r.shape, lambda i: (0, 0))],
        out_specs=out_specs,
        out_shape=out_shape,
        compiler_params=_cparams(("arbitrary",)),
        name="proj",
    )(x2d, norm_g, w_r)


def _compress_hidden(load_rows, pe_ref, w1_ref, kv, m):
    xs = [load_rows(s) for s in range(CMP_STRIDE)]
    hs = []
    for sub in range(CMP_BLOCK // CMP_STRIDE):
        lhs = jnp.concatenate([(xs[s] + pe_ref[kv, sub, s]).astype(BF16) for s in range(CMP_STRIDE)], axis=1)
        hs.append(jnp.dot(lhs, w1_ref[kv, sub], preferred_element_type=F32))
    return hs[0] + pltpu.roll(hs[1], m - 1, 0)


def _cover_matrix(nc_pad, ns_pad):
    c0 = lax.broadcasted_iota(jnp.int32, (nc_pad, ns_pad), 0) * CMP_STRIDE
    s0 = lax.broadcasted_iota(jnp.int32, (nc_pad, ns_pad), 1) * SEL_BLOCK
    return jnp.where((c0 < s0 + SEL_BLOCK) & (c0 + CMP_BLOCK > s0), 1.0, 0.0).astype(F32)


def _compress_prompt_kernel(c_ref, pe_ref, w1_ref, w2_ref, kc_ref, rhs_ref, *, nchunk):
    ones_rows = jnp.where(lax.broadcasted_iota(jnp.int32, (HEAD_DIM, nchunk), 0) == 0, 1.0, 0.0).astype(BF16)
    s0 = lax.broadcasted_iota(jnp.int32, (LANES, nchunk), 0) * SEL_BLOCK
    c0 = lax.broadcasted_iota(jnp.int32, (LANES, nchunk), 1) * CMP_STRIDE
    cover_t = jnp.where((c0 < s0 + SEL_BLOCK) & (c0 + CMP_BLOCK > s0), 1.0, 0.0).astype(BF16)
    for kv in range(2):
        load = lambda s: c_ref[kv, pl.ds(s, nchunk, stride=CMP_STRIDE), :]
        h = _compress_hidden(load, pe_ref, w1_ref, kv, nchunk)
        full = jnp.dot(_silu(h).astype(BF16), w2_ref[kv], preferred_element_type=F32)
        if kv == 0:
            kc_ref[0, :, :] = full[:, 0:HEAD_DIM].astype(BF16)
            kc_ref[1, :, :] = full[:, HEAD_DIM:2 * HEAD_DIM].astype(BF16)
        else:
            full_t = full.T
            for kvh in range(N_KV_HEADS):
                rhs_ref[kvh, 0:HEAD_DIM, :] = full_t[kvh * HEAD_DIM:(kvh + 1) * HEAD_DIM, :].astype(BF16)
                rhs_ref[kvh, HEAD_DIM:LANES, :] = ones_rows
                rhs_ref[kvh, LANES:2 * LANES, :] = cover_t


def _compress_prompt(crm, n, pe_t, w1bd, w2bd):
    seq_len = crm.shape[1] // n
    nchunk = seq_len // CMP_STRIDE
    return pl.pallas_call(
        functools.partial(_compress_prompt_kernel, nchunk=nchunk),
        grid=(n,),
        in_specs=[pl.BlockSpec((2, seq_len, LANES), lambda b: (0, b, 0)),
                  pl.BlockSpec(pe_t.shape, lambda b: (0, 0, 0, 0, 0)),
                  pl.BlockSpec(w1bd.shape, lambda b: (0, 0, 0, 0)),
                  pl.BlockSpec(w2bd.shape, lambda b: (0, 0, 0))],
        out_specs=(pl.BlockSpec((None, N_KV_HEADS, nchunk, HEAD_DIM), lambda b: (b, 0, 0, 0)),
                   pl.BlockSpec((None, N_KV_HEADS, 2 * LANES, nchunk), lambda b: (b, 0, 0, 0))),
        out_shape=(jax.ShapeDtypeStruct((n, N_KV_HEADS, nchunk, HEAD_DIM), BF16),
                   jax.ShapeDtypeStruct((n, N_KV_HEADS, 2 * LANES, nchunk), BF16)),
        compiler_params=_cparams(("arbitrary",)),
        name="compress_prompt",
    )(crm, pe_t, w1bd, w2bd)


def _not_selected(val, n_top, axis):
    blk = lax.broadcasted_iota(jnp.int32, val.shape, axis).astype(F32)
    notsel = jnp.ones(val.shape, F32)
    for _ in range(n_top):
        m = jnp.max(val, axis=axis, keepdims=True)
        idx = jnp.min(jnp.where(val == m, blk, float(LANES)), axis=axis, keepdims=True)
        pick = blk == idx
        notsel = jnp.where(pick, 0.0, notsel)
        val = jnp.where(pick, -3e38, val)
    return notsel


def _masked_importance(imp, t, n_blocks, axis):
    blk = lax.broadcasted_iota(jnp.int32, imp.shape, axis)
    cur = t // SEL_BLOCK
    forced = (blk == 0) | (blk == cur) | (blk == cur - 1)
    val = jnp.where(forced, 1e9, jnp.where(blk * SEL_BLOCK <= t, imp, -1e9))
    return jnp.where(blk < n_blocks, val, -2e9)


def _attn_prompt_kernel(qt_ref, gatet_ref, sza_ref, kc_ref, rhs_ref, ks_ref, vst_ref, kw_ref, vwt_ref, out_ref,
                        qlo_ref, qhi_ref, m_ref, acc_ref, mw_ref, accw_ref, oc_ref, *, n_sel_blocks, seq_len):
    qb = pl.program_id(2)
    qs = qb * Q_BLOCK
    hq = GQA * Q_BLOCK
    t_row = qs + lax.broadcasted_iota(jnp.int32, (1, Q_BLOCK), 1)
    t_all = jnp.concatenate([t_row] * GQA, axis=1)
    qt = qt_ref[...]
    qt_all = jnp.concatenate([qt[g * HEAD_DIM:(g + 1) * HEAD_DIM, :] for g in range(GQA)], axis=1)

    nc = kc_ref.shape[0]
    c_last = lax.broadcasted_iota(jnp.int32, (nc, hq), 0) * CMP_STRIDE + (CMP_BLOCK - 1)
    ok_c = c_last <= t_all
    s = jnp.dot(kc_ref[...], qt_all, preferred_element_type=F32)
    s = jnp.where(ok_c, s, NEG_FILL)
    e = jnp.where(ok_c, jnp.exp(s - jnp.max(s, axis=0, keepdims=True)), 0.0)
    r = jnp.dot(rhs_ref[...], e.astype(BF16), preferred_element_type=F32)
    inv = 1.0 / jnp.maximum(r[HEAD_DIM:HEAD_DIM + 1, :], 1e-30)
    oc_ref[...] = r[0:HEAD_DIM, :] * inv
    imp_all = r[LANES:2 * LANES, :] * inv
    imp = imp_all[:, 0:Q_BLOCK]
    for g in range(1, GQA):
        imp = imp + imp_all[:, g * Q_BLOCK:(g + 1) * Q_BLOCK]

    notsel = _not_selected(_masked_importance(imp, t_row, n_sel_blocks, 0), min(N_SEL, n_sel_blocks), 0)
    notsel = notsel.astype(BF16)
    qlo_ref[...] = jnp.concatenate([qt_all, jnp.concatenate([notsel[0:HEAD_DIM, :]] * GQA, axis=1)], axis=0)
    qhi_ref[...] = jnp.concatenate([qt_all, jnp.concatenate([notsel[HEAD_DIM:2 * HEAD_DIM, :]] * GQA, axis=1)],
                                   axis=0)

    def flash_step(qref, k_ref, vt_ref, start, width, mask_fn, mref, aref):
        start = pl.multiple_of(start, LANES)
        k = k_ref[pl.ds(start, width), :]
        vt = vt_ref[:, pl.ds(start, width)]
        s = jnp.dot(k, qref[...], preferred_element_type=F32)
        if mask_fn is not None:
            s = jnp.where(mask_fn(start + lax.broadcasted_iota(jnp.int32, (width, hq), 0)), s, NEG_FILL)
        m_old = mref[...]
        m_new = jnp.maximum(m_old, jnp.max(s, axis=0, keepdims=True))
        p = jnp.exp(s - m_new).astype(BF16)
        aref[...] = jnp.exp(m_old - m_new) * aref[...] + jnp.dot(vt, p, preferred_element_type=F32)
        mref[...] = m_new

    m_ref[...] = jnp.full(m_ref.shape, NEG_FILL, F32)
    acc_ref[...] = jnp.zeros(acc_ref.shape, F32)
    mw_ref[...] = jnp.full(mw_ref.shape, NEG_FILL, F32)
    accw_ref[...] = jnp.zeros(accw_ref.shape, F32)

    tile = min(KEY_TILE, seq_len)
    n_full = qs // tile
    lo_tiles = HEAD_DIM * SEL_BLOCK // tile
    causal = lambda kpos: kpos <= t_all

    def body_lo(kt, carry):
        flash_step(qlo_ref, ks_ref, vst_ref, kt * tile, tile, None, m_ref, acc_ref)
        return carry

    def body_hi(kt, carry):
        flash_step(qhi_ref, ks_ref, vst_ref, kt * tile, tile, None, m_ref, acc_ref)
        return carry

    lax.fori_loop(0, jnp.minimum(n_full, lo_tiles), body_lo, 0)
    lax.fori_loop(lo_tiles, jnp.maximum(n_full, lo_tiles), body_hi, 0)

    @pl.when(n_full < lo_tiles)
    def _():
        flash_step(qlo_ref, ks_ref, vst_ref, n_full * tile, tile, causal, m_ref, acc_ref)

    @pl.when(n_full >= lo_tiles)
    def _():
        flash_step(qhi_ref, ks_ref, vst_ref, n_full * tile, tile, causal, m_ref, acc_ref)

    wwidth = min(WINDOW, seq_len)
    band = lambda kpos: (kpos > t_all - WINDOW) & (kpos <= t_all)
    flash_step(qlo_ref, kw_ref, vwt_ref, jnp.maximum(qs - wwidth, 0), wwidth, band, mw_ref, accw_ref)

    @pl.when(qs >= wwidth)
    def _():
        flash_step(qlo_ref, kw_ref, vwt_ref, qs, Q_BLOCK, causal, mw_ref, accw_ref)

    gates = gatet_ref[...]
    sza = sza_ref[...]
    a_s = acc_ref[...]
    a_w = accw_ref[...]
    o_s = a_s[0:HEAD_DIM, :] * (1.0 / a_s[HEAD_DIM:HEAD_DIM + 1, :])
    o_w = a_w[0:HEAD_DIM, :] * (1.0 / a_w[HEAD_DIM:HEAD_DIM + 1, :])
    o_c = oc_ref[...]
    outs = []
    for g in range(GQA):
        cols = slice(g * Q_BLOCK, (g + 1) * Q_BLOCK)
        gc = gates[g * N_BRANCH + 0:g * N_BRANCH + 1, :]
        gs = gates[g * N_BRANCH + 1:g * N_BRANCH + 2, :]
        gw = gates[g * N_BRANCH + 2:g * N_BRANCH + 3, :]
        outs.append((gc * o_c[:, cols] + gs * o_s[:, cols] + gw * o_w[:, cols]).T)
    out_ref[...] = (jnp.concatenate(outs, axis=1) * sza).astype(BF16)


def _attn_prompt(qt, gates_t, sza3, kc, rhs_c, ks, vst, kw, vwt):
    n, seq_len, _ = sza3.shape
    nqb = seq_len // Q_BLOCK
    nchunk = kc.shape[2]
    hw = GQA * HEAD_DIM
    k_blk = pl.BlockSpec((None, seq_len, LANES), lambda b, k, i: (k, b, 0))
    vt_blk = pl.BlockSpec((None, V_ROWS, seq_len), lambda b, k, i: (k, 0, b))
    return pl.pallas_call(
        functools.partial(_attn_prompt_kernel, n_sel_blocks=-(-seq_len // SEL_BLOCK), seq_len=seq_len),
        grid=(n, N_KV_HEADS, nqb),
        in_specs=[
            pl.BlockSpec((hw, Q_BLOCK), lambda b, k, i: (k, b * nqb + i)),
            pl.BlockSpec((None, GATE_ROWS, Q_BLOCK), lambda b, k, i: (k, 0, b * nqb + i)),
            pl.BlockSpec((None, Q_BLOCK, hw), lambda b, k, i: (b, i, k)),
            pl.BlockSpec((None, None, nchunk, HEAD_DIM), lambda b, k, i: (b, k, 0, 0)),
            pl.BlockSpec((None, None, 2 * LANES, nchunk), lambda b, k, i: (b, k, 0, 0)),
            k_blk, vt_blk, k_blk, vt_blk,
        ],
        out_specs=pl.BlockSpec((None, Q_BLOCK, hw), lambda b, k, i: (b, i, k)),
        out_shape=jax.ShapeDtypeStruct((n, seq_len, D_ATTN), BF16),
        scratch_shapes=[
            pltpu.VMEM((2 * HEAD_DIM, GQA * Q_BLOCK), BF16),
            pltpu.VMEM((2 * HEAD_DIM, GQA * Q_BLOCK), BF16),
            pltpu.VMEM((1, GQA * Q_BLOCK), F32),
            pltpu.VMEM((V_ROWS, GQA * Q_BLOCK), F32),
            pltpu.VMEM((1, GQA * Q_BLOCK), F32),
            pltpu.VMEM((V_ROWS, GQA * Q_BLOCK), F32),
            pltpu.VMEM((HEAD_DIM, GQA * Q_BLOCK), F32),
        ],
        compiler_params=_cparams(("arbitrary", "arbitrary", "arbitrary")),
        name="attn_prompt",
    )(qt, gates_t, sza3, kc, rhs_c, ks, vst, kw, vwt)


def _pool_out(d, pw_ref, ps_ref, szp):
    ys = [jnp.dot(d[:, g * POOL_GROUP_DIM:(g + 1) * POOL_GROUP_DIM].astype(BF16), pw_ref[g],
                  preferred_element_type=F32) for g in range(len(POOL_WINDOWS))]
    return jnp.concatenate(ys, axis=1) * ps_ref[...] * szp


def _finish(x, a_bf16, b, wo_ref, fg_ref):
    mix = jnp.concatenate([a_bf16, b.astype(BF16)], axis=1)
    y = x + jnp.dot(mix, wo_ref[...], preferred_element_type=F32)
    ms = jnp.mean(y * y, axis=-1, keepdims=True)
    return y * lax.rsqrt(ms + RMS_EPS) * fg_ref[...]


def _out_prompt_kernel(a_ref, u_ref, halo_ref, szp_ref, x_ref, pw_ref, ps_ref, wo_ref, fg_ref, y_ref, ext_ref,
                       *, tm, halo):
    i = pl.program_id(1)
    u = u_ref[...]
    ext_ref[0:halo, :] = jnp.where(i > 0, halo_ref[...], 0.0)
    ext_ref[halo:halo + tm, :] = u
    pos = i * tm + lax.broadcasted_iota(jnp.int32, (tm, POOL_GROUP_DIM), 0)
    ds = []
    for g, w in enumerate(POOL_WINDOWS):
        c0 = g * POOL_GROUP_DIM
        acc = u[:, c0:c0 + POOL_GROUP_DIM]
        for k in range(1, w):
            acc = acc + ext_ref[halo - k:halo - k + tm, c0:c0 + POOL_GROUP_DIM]
        cnt = jnp.minimum(pos + 1, w).astype(F32)
        ds.append(acc / cnt - u[:, c0:c0 + POOL_GROUP_DIM])
    b = _pool_out(jnp.concatenate(ds, axis=1), pw_ref, ps_ref, szp_ref[...])
    y_ref[...] = _finish(x_ref[...], a_ref[...], b, wo_ref, fg_ref)


def _out_prompt(a3, u3, szp3, x3, pool_w, pool_scale, w_out, final_g, *, tm):
    n, seq_len, d_model = x3.shape
    halo = 16
    nt = seq_len // tm
    blk = lambda w: pl.BlockSpec((None, tm, w), lambda b, i: (b, i, 0))
    const = lambda a: pl.BlockSpec(a.shape, lambda b, i: (0,) * a.ndim)
    return pl.pallas_call(
        functools.partial(_out_prompt_kernel, tm=tm, halo=halo),
        grid=(n, nt),
        in_specs=[blk(D_ATTN), blk(D_POOL),
                  pl.BlockSpec((None, halo, D_POOL), lambda b, i: (b, jnp.maximum(i * (tm // halo) - 1, 0), 0)),
                  blk(D_POOL), blk(d_model),
                  const(pool_w), const(pool_scale), const(w_out), const(final_g)],
        out_specs=blk(d_model),
        out_shape=jax.ShapeDtypeStruct((n, seq_len, d_model), F32),
        scratch_shapes=[pltpu.VMEM((tm + halo, D_POOL), F32)],
        compiler_params=_cparams(("arbitrary", "arbitrary")),
        name="out_prompt",
    )(a3, u3, u3, szp3, x3, pool_w, pool_scale, w_out, final_g)


def _out_sample_kernel(o_ref, sza_ref, u_ref, st_ref, szp_ref, x_ref, pw_ref, ps_ref, wo_ref, fg_ref, y_ref):
    u = u_ref[...]
    ds = []
    for g, w in enumerate(POOL_WINDOWS):
        c0 = g * POOL_GROUP_DIM
        acc = u[:, c0:c0 + POOL_GROUP_DIM]
        for k in range(1, w):
            acc = acc + st_ref[POOL_STATE - k, :, c0:c0 + POOL_GROUP_DIM]
        ds.append(acc / float(w) - u[:, c0:c0 + POOL_GROUP_DIM])
    b = _pool_out(jnp.concatenate(ds, axis=1), pw_ref, ps_ref, szp_ref[...])
    a = (o_ref[...] * sza_ref[...]).astype(BF16)
    y_ref[...] = _finish(x_ref[...], a, b, wo_ref, fg_ref)


def _out_sample(o2, sza, u, state_t, szp, x2, pool_w, pool_scale, w_out, final_g):
    args = (o2, sza, u, state_t, szp, x2, pool_w, pool_scale, w_out, final_g)
    full = lambda a: pl.BlockSpec(a.shape, lambda i: (0,) * a.ndim)
    return pl.pallas_call(
        _out_sample_kernel,
        grid=(1,),
        in_specs=[full(a) for a in args],
        out_specs=full(x2),
        out_shape=jax.ShapeDtypeStruct(x2.shape, F32),
        compiler_params=_cparams(("arbitrary",)),
        name="out_sample",
    )(*args)


def _cmp_sample_kernel(pt_ref, *refs, n_pages, page):
    del pt_ref
    pages = refs[:n_pages]
    qbd_ref, pe_ref, w1_ref, w2_ref, oc_ref, imp_ref, rows_ref = refs[n_pages:]
    nchunk = n_pages * page // CMP_STRIDE
    for p in range(n_pages):
        for kv in range(2):
            rows_ref[kv, p * page:(p + 1) * page, :] = pages[p][kv * LANES:(kv + 1) * LANES, :].T
    fulls = []
    for kv in range(2):
        load = lambda s: rows_ref[kv, pl.ds(s, nchunk, stride=CMP_STRIDE), :]
        h = _compress_hidden(load, pe_ref, w1_ref, kv, nchunk)
        fulls.append(jnp.dot(_silu(h).astype(BF16), w2_ref[kv], preferred_element_type=F32))
    k_c, v_c = fulls
    qbd = qbd_ref[...]
    s = lax.dot_general(qbd, k_c.astype(BF16), (((1,), (1,)), ((), ())), preferred_element_type=F32)
    ok = lax.broadcasted_iota(jnp.int32, s.shape, 1) < nchunk - 1
    s = jnp.where(ok, s, NEG_FILL)
    e = jnp.where(ok, jnp.exp(s - jnp.max(s, axis=1, keepdims=True)), 0.0)
    pc = (e * (1.0 / jnp.sum(e, axis=1, keepdims=True))).astype(BF16)
    oc_ref[...] = jnp.dot(pc, v_c.astype(BF16), preferred_element_type=F32)
    imp_ref[...] = jnp.dot(pc, _cover_matrix(nchunk, LANES).astype(BF16), preferred_element_type=F32)


def _cmp_sample(page_table, cache_t, qbd, pe_t, w1bd, w2bd):
    nb, n_pages = page_table.shape
    page = cache_t.shape[-1]
    page_spec = lambda p: pl.BlockSpec((None, KV_W, page), lambda b, pt, p=p: (pt[b, p], 0, 0))
    const = lambda a: pl.BlockSpec(a.shape, lambda b, pt: (0,) * a.ndim)
    per_b = pl.BlockSpec((None, N_HEADS, LANES), lambda b, pt: (b, 0, 0))
    grid_spec = pltpu.PrefetchScalarGridSpec(
        num_scalar_prefetch=1,
        grid=(nb,),
        in_specs=[page_spec(p) for p in range(n_pages)] + [per_b, const(pe_t), const(w1bd), const(w2bd)],
        out_specs=(per_b, per_b),
        scratch_shapes=[pltpu.VMEM((2, n_pages * page, LANES), F32)],
    )
    return pl.pallas_call(
        functools.partial(_cmp_sample_kernel, n_pages=n_pages, page=page),
        grid_spec=grid_spec,
        out_shape=(jax.ShapeDtypeStruct((nb, N_HEADS, LANES), F32),
                   jax.ShapeDtypeStruct((nb, N_HEADS, LANES), F32)),
        compiler_params=_cparams(("arbitrary",)),
        name="cmp_sample",
    )(page_table, *([cache_t] * n_pages), qbd, pe_t, w1bd, w2bd)


def _topk_sample_kernel(imp_ref, out_ref, *, t_pos, n_blocks):
    imp = imp_ref[0] + imp_ref[1] + imp_ref[2] + imp_ref[3]
    t = jnp.full((imp.shape[0], 1), t_pos, jnp.int32)
    out_ref[...] = _not_selected(_masked_importance(imp, t, n_blocks, 1), min(N_SEL, n_blocks), 1)


def _topk_sample(imp_g, *, t_pos, n_blocks):
    rows = imp_g.shape[1]
    return pl.pallas_call(
        functools.partial(_topk_sample_kernel, t_pos=t_pos, n_blocks=n_blocks),
        grid=(1,),
        in_specs=[pl.BlockSpec(imp_g.shape, lambda i: (0, 0, 0))],
        out_specs=pl.BlockSpec((rows, LANES), lambda i: (0, 0)),
        out_shape=jax.ShapeDtypeStruct((rows, LANES), F32),
        compiler_params=_cparams(("arbitrary",)),
        name="topk_sample",
    )(imp_g)


def _attn_sample_kernel(pt_ref, *refs, n_pages, page):
    del pt_ref
    pages = refs[:n_pages]
    (win_ref, qbd_ref, ns_ref, snew_ref, wnew_ref, wcol_ref, oc_ref, gate_ref, o_ref, nwin_ref) = refs[n_pages:]
    qbd = qbd_ref[...]
    qf = qbd.astype(F32)
    ns = ns_ref[...]
    nt_ = (((1,), (1,)), ((), ()))

    def new_token(row_ref):
        k_new = row_ref[:, 0:LANES].astype(BF16).astype(F32)
        v_new = row_ref[:, LANES:2 * LANES].astype(BF16).astype(F32)
        return jnp.sum(qf * k_new, axis=1, keepdims=True), v_new

    lane = lax.broadcasted_iota(jnp.int32, (N_HEADS, page), 1)
    per_page = page // SEL_BLOCK
    scores = []
    for p in range(n_pages):
        s = jnp.dot(qbd, pages[p][0:LANES, :].astype(BF16), preferred_element_type=F32)
        flag = ns[:, p * per_page:p * per_page + 1]
        for j in range(1, per_page):
            flag = jnp.where(lane >= j * SEL_BLOCK, ns[:, p * per_page + j:p * per_page + j + 1], flag)
        scores.append(jnp.where(flag > 0.5, NEG_FILL, s))
    nb_cache = n_pages * per_page
    s_new, v_new = new_token(snew_ref)
    s_new = jnp.where(ns[:, nb_cache:nb_cache + 1] > 0.5, NEG_FILL, s_new)
    m = s_new
    for s in scores:
        m = jnp.maximum(m, jnp.max(s, axis=1, keepdims=True))
    e_new = jnp.exp(s_new - m)
    l = e_new
    acc = e_new * v_new
    for p in range(n_pages):
        e = jnp.exp(scores[p] - m).astype(BF16)
        l = l + jnp.sum(e.astype(F32), axis=1, keepdims=True)
        acc = acc + lax.dot_general(e, pages[p][LANES:2 * LANES, :].astype(BF16), nt_, preferred_element_type=F32)
    o_s = acc * (1.0 / l)

    win = win_ref[...]
    wlen = win.shape[1]
    s_w = jnp.dot(qbd, win[0:LANES, :].astype(BF16), preferred_element_type=F32)
    first = max(wlen - WINDOW + 1, 0)
    s_w = jnp.where(lax.broadcasted_iota(jnp.int32, s_w.shape, 1) >= first, s_w, NEG_FILL)
    sw_new, vw_new = new_token(wnew_ref)
    m_w = jnp.maximum(sw_new, jnp.max(s_w, axis=1, keepdims=True))
    ew_new = jnp.exp(sw_new - m_w)
    e_w = jnp.exp(s_w - m_w).astype(BF16)
    l_w = ew_new + jnp.sum(e_w.astype(F32), axis=1, keepdims=True)
    acc_w = ew_new * vw_new + lax.dot_general(e_w, win[LANES:2 * LANES, :].astype(BF16), nt_,
                                              preferred_element_type=F32)
    o_w = acc_w * (1.0 / l_w)

    gates = gate_ref[...]
    o_ref[...] = gates[:, 0:1] * oc_ref[...] + gates[:, 1:2] * o_s + gates[:, 2:3] * o_w

    shifted = pltpu.roll(win, wlen - 1, 1)
    last = lax.broadcasted_iota(jnp.int32, win.shape, 1) == wlen - 1
    nwin_ref[...] = jnp.where(last, wcol_ref[...], shifted)


def _attn_sample(page_table, cache_t, win_t, qbd, ns8, s_new, w_new, w_col, o_c, gates8):
    nb, n_pages = page_table.shape
    page = cache_t.shape[-1]
    wlen = win_t.shape[-1]
    page_spec = lambda p: pl.BlockSpec((None, KV_W, page), lambda b, pt, p=p: (pt[b, p], 0, 0))
    per_b = lambda a: pl.BlockSpec((None,) + a.shape[1:], lambda b, pt: (b,) + (0,) * (a.ndim - 1))
    grid_spec = pltpu.PrefetchScalarGridSpec(
        num_scalar_prefetch=1,
        grid=(nb,),
        in_specs=[page_spec(p) for p in range(n_pages)]
        + [per_b(a) for a in (win_t, qbd, ns8, s_new, w_new, w_col, o_c, gates8)],
        out_specs=(per_b(o_c), per_b(win_t)),
    )
    return pl.pallas_call(
        functools.partial(_attn_sample_kernel, n_pages=n_pages, page=page),
        grid_spec=grid_spec,
        out_shape=(jax.ShapeDtypeStruct(o_c.shape, F32), jax.ShapeDtypeStruct(win_t.shape, F32)),
        compiler_params=_cparams(("arbitrary",)),
        name="attn_sample",
    )(page_table, *([cache_t] * n_pages), win_t, qbd, ns8, s_new, w_new, w_col, o_c, gates8)


def _prep_w_in(w_in):
    o = D_ATTN + 3 * KV_W
    ng = GQA * N_BRANCH
    gl = w_in[:, o:o + N_KV_HEADS * ng]
    rest = w_in[:, o + N_KV_HEADS * ng:]
    pad = jnp.zeros((w_in.shape[0], LANES - ng), w_in.dtype)
    return jnp.concatenate([w_in[:, :o], rest, gl[:, :ng], pad, gl[:, ng:], pad], axis=1).astype(BF16)


def _prep_compress(cmp_pe, cmp_w1, cmp_w2):
    r = CMP_BLOCK // CMP_STRIDE
    pe_t = jnp.tile(cmp_pe.reshape(2, r, CMP_STRIDE, 1, HEAD_DIM), (1, 1, 1, 1, N_KV_HEADS))
    w1 = cmp_w1.reshape(2, r, CMP_STRIDE, HEAD_DIM, CMP_HIDDEN)
    z1 = jnp.zeros_like(w1)
    top = jnp.concatenate([w1, z1], axis=-1)
    bot = jnp.concatenate([z1, w1], axis=-1)
    w1bd = jnp.concatenate([top, bot], axis=3)
    w1bd = w1bd.reshape(2, r, CMP_STRIDE * LANES, N_KV_HEADS * CMP_HIDDEN).astype(BF16)
    z2 = jnp.zeros_like(cmp_w2)
    w2bd = jnp.concatenate([jnp.concatenate([cmp_w2, z2], axis=-1),
                            jnp.concatenate([z2, cmp_w2], axis=-1)], axis=1).astype(BF16)
    return pe_t, w1bd, w2bd


def _kv_out(kv_t):
    n, _, rows = kv_t.shape
    return jnp.transpose(kv_t.reshape(n, 2, N_KV_HEADS, HEAD_DIM, rows), (0, 4, 1, 2, 3))[None]


def kernel(x_prompt, x_sample, cache_cmp_kv, cache_slc_kv, cache_win_kv, state_pool, page_table, norm_g, w_in,
           cmp_pe, cmp_w1, cmp_w2, pool_w, pool_scale, w_out, final_g):
    n, seq_len, d_model = x_prompt.shape
    nb = x_sample.shape[0]
    n_phys, page = cache_cmp_kv.shape[1], cache_cmp_kv.shape[2]
    n_pages = page_table.shape[1]
    past_len = n_pages * page
    wlen = cache_win_kv.shape[2]

    w_r = _prep_w_in(w_in[0])
    pe_t, w1bd, w2bd = _prep_compress(cmp_pe[0], cmp_w1[0], cmp_w2[0])
    pool_w_b = pool_w[0].astype(BF16)
    w_out_b = w_out[0].astype(BF16)
    fg = final_g.reshape(1, d_model)

    (qt, crm, ckv_t, skv_t, wkv_t, ks, vst, kw, vwt, gates_t, sza, u, szp) = _project(
        x_prompt.reshape(n * seq_len, d_model), norm_g, w_r, tm=256, seq_len=seq_len)
    r3 = lambda a: a.reshape(n, seq_len, a.shape[-1])
    kc, rhs_c = _compress_prompt(crm, n, pe_t, w1bd, w2bd)
    a3 = _attn_prompt(qt, gates_t, r3(sza), kc, rhs_c, ks, vst, kw, vwt)
    y_prompt = _out_prompt(a3, r3(u), r3(szp), x_prompt, pool_w_b, pool_scale, w_out_b, fg, tm=512)

    new_cmp_p = _kv_out(ckv_t)
    new_slc_p = _kv_out(skv_t)
    new_win_p = _kv_out(wkv_t[:, :, seq_len - min(WINDOW, seq_len):])
    new_pool_p = r3(u)[:, seq_len - POOL_STATE:][None]

    (qt_s, _, ckv_ts, skv_ts, wkv_ts, _, _, _, _, gates_ts, sza_s, u_s, szp_s) = _project(
        x_sample.reshape(nb, d_model), norm_g, w_r, tm=nb, seq_len=nb)
    q_s = qt_s.T
    q5 = q_s.reshape(nb, N_KV_HEADS, GQA, 1, HEAD_DIM)
    eye = jnp.eye(N_KV_HEADS, dtype=q_s.dtype).reshape(1, N_KV_HEADS, 1, N_KV_HEADS, 1)
    qbd = (q5 * eye).reshape(nb, N_HEADS, LANES)

    to_pages = lambda c: jnp.transpose(c[0], (0, 2, 3, 4, 1)).reshape(n_phys, KV_W, page)
    o_c, imp8 = _cmp_sample(page_table, to_pages(cache_cmp_kv), qbd, pe_t, w1bd, w2bd)
    imp_g = jnp.transpose(imp8.reshape(nb * N_KV_HEADS, GQA, LANES), (1, 0, 2))
    n_blocks = -(-(past_len + 1) // SEL_BLOCK)
    notsel = _topk_sample(imp_g, t_pos=past_len, n_blocks=n_blocks)
    ns8 = jnp.repeat(notsel.reshape(nb, N_KV_HEADS, 1, LANES), GQA, axis=2).reshape(nb, N_HEADS, LANES)

    gates8 = jnp.transpose(gates_ts[:, :GQA * N_BRANCH, :], (2, 0, 1)).reshape(nb, N_HEADS, N_BRANCH)
    gates8 = jnp.pad(gates8, ((0, 0), (0, 0), (0, LANES - N_BRANCH)))
    win_t = jnp.transpose(cache_win_kv[0], (0, 2, 3, 4, 1)).reshape(nb, KV_W, wlen)
    s_new = skv_ts[0].T.reshape(nb, 1, KV_W)
    w_new = wkv_ts[0].T.reshape(nb, 1, KV_W)
    w_col = w_new.reshape(nb, KV_W, 1)
    o8, nwin_t = _attn_sample(page_table, to_pages(cache_slc_kv), win_t, qbd, ns8, s_new, w_new, w_col, o_c, gates8)
    o5 = o8.reshape(nb, N_KV_HEADS, GQA, N_KV_HEADS, HEAD_DIM)
    o2 = jnp.stack([o5[:, k, :, k, :] for k in range(N_KV_HEADS)], axis=1).reshape(nb, D_ATTN)

    state_t = jnp.transpose(state_pool[0], (1, 0, 2))
    y_sample = _out_sample(o2, sza_s, u_s, state_t, szp_s, x_sample.reshape(nb, d_model),
                           pool_w_b, pool_scale, w_out_b, fg).reshape(nb, 1, d_model)

    kv_out_s = lambda t: jnp.transpose(t.reshape(2, N_KV_HEADS, HEAD_DIM, nb, 1), (3, 4, 0, 1, 2))[None]
    new_cmp_s = kv_out_s(ckv_ts)
    new_slc_s = kv_out_s(skv_ts)
    new_win_s = jnp.transpose(nwin_t.reshape(nb, 2, N_KV_HEADS, HEAD_DIM, wlen), (0, 4, 1, 2, 3))[None]
    new_pool_s = jnp.transpose(jnp.concatenate([state_t[1:], u_s[None]], axis=0), (1, 0, 2))[None]

    return (y_prompt, y_sample, new_cmp_p, new_slc_p, new_win_p, new_pool_p,
            new_cmp_s, new_slc_s, new_win_s, new_pool_s)
```

```python
import functools

import jax
import jax.numpy as jnp
from jax import lax
from jax.experimental import pallas as pl
from jax.experimental.pallas import tpu as pltpu

F32 = jnp.float32
BF16 = jnp.bfloat16

HEAD_DIM = 64
N_KV_HEADS = 2
GQA = 4
N_HEADS = N_KV_HEADS * GQA
D_ATTN = N_HEADS * HEAD_DIM
KV_W = 2 * N_KV_HEADS * HEAD_DIM
N_BRANCH = 3
D_POOL = 512
CMP_BLOCK = 32
CMP_STRIDE = 16
CMP_HIDDEN = 128
SEL_BLOCK = 64
N_SEL = 16
WINDOW = 512
Q_BLOCK = 128
POOL_WINDOWS = (2, 4, 8, 16)
POOL_GROUP_DIM = 128
POOL_STATE = 15
RMS_EPS = 1e-6
LANES = 128
NEG_BIAS = -(2.0 ** 30)
NEG_FILL = -1e30
GATE_COLS = 2 * LANES
GATE_ROWS = 16
BF16_SUBLANES = 16
V_ROWS = HEAD_DIM + BF16_SUBLANES
KEY_TILE = 512
SEL_CHAINS = 2
LOG2E = 1.4426950408889634
P_PAD = D_ATTN + 3 * KV_W + 3 * 512 + GATE_COLS
VMEM_LIMIT = 48 * 1024 * 1024


def _cparams(sem):
    return pltpu.CompilerParams(dimension_semantics=sem, vmem_limit_bytes=VMEM_LIMIT)


def _silu(z):
    return z * jax.nn.sigmoid(z)


def _proj_kernel(x_ref, g_ref, w_ref, qt_ref, crm_ref, ct_ref, st_ref, wt_ref, ks_ref, vst_ref, kw_ref, vwt_ref,
                 gatet_ref, sza_ref, u_ref, szp_ref, *, tm, seq_len):
    i = pl.program_id(0)
    x = x_ref[...]
    ms = jnp.mean(x * x, axis=-1, keepdims=True)
    xn = (x * lax.rsqrt(ms + RMS_EPS) * g_ref[...]).astype(BF16)

    def mm(c0, c1):
        return jnp.dot(xn, w_ref[:, c0:c1], preferred_element_type=F32)

    qt_ref[...] = (mm(0, D_ATTN) * (HEAD_DIM ** -0.5 * LOG2E)).T.astype(BF16)
    kv = mm(D_ATTN, D_ATTN + 3 * KV_W)
    crm_ref[0] = kv[:, 0:LANES]
    crm_ref[1] = kv[:, LANES:2 * LANES]
    kvt = kv.T
    ct_ref[...] = kvt[0:KV_W, :]
    st_ref[...] = kvt[KV_W:2 * KV_W, :]
    wt_ref[...] = kvt[2 * KV_W:3 * KV_W, :]

    lane = lax.broadcasted_iota(jnp.int32, (tm, LANES), 1)
    pos = (i * tm) % seq_len + lax.broadcasted_iota(jnp.int32, (tm, LANES), 0)
    onehot = jnp.where(lane - HEAD_DIM == (pos // SEL_BLOCK) % HEAD_DIM, NEG_BIAS, 0.0).astype(F32)
    ones_rows = jnp.where(lax.broadcasted_iota(jnp.int32, (V_ROWS - HEAD_DIM, tm), 0) == 0, 1.0, 0.0).astype(BF16)
    for kvh in range(N_KV_HEADS):
        for (base, k_dst, v_dst, k_pad) in ((KV_W, ks_ref, vst_ref, onehot), (2 * KV_W, kw_ref, vwt_ref, 0.0)):
            slab = kv[:, base:base + LANES]
            if kvh == 1:
                slab = pltpu.roll(slab, HEAD_DIM, 1)
            k_dst[kvh, :, :] = jnp.where(lane < HEAD_DIM, slab, k_pad).astype(BF16)
            v0 = base + LANES + kvh * HEAD_DIM
            v_dst[kvh, 0:HEAD_DIM, :] = kvt[v0:v0 + HEAD_DIM, :].astype(BF16)
            v_dst[kvh, HEAD_DIM:V_ROWS, :] = ones_rows

    c = D_ATTN + 3 * KV_W
    sza_ref[...] = _silu(mm(c, c + 512))
    u_ref[...] = mm(c + 512, c + 1024)
    szp_ref[...] = _silu(mm(c + 1024, c + 1536))
    gate_t = jax.nn.sigmoid(mm(c + 1536, c + 1536 + GATE_COLS)).T
    for kvh in range(N_KV_HEADS):
        gatet_ref[kvh, :, :] = gate_t[kvh * LANES:kvh * LANES + GATE_ROWS, :]


def _project(x2d, norm_g, w_r, *, tm, seq_len):
    rows = x2d.shape[0]
    nt = rows // tm
    tps = seq_len // tm
    row_blk = lambda w: pl.BlockSpec((tm, w), lambda i: (i, 0))
    kv_t = jax.ShapeDtypeStruct((rows // seq_len, KV_W, seq_len), F32)
    kv_t_blk = pl.BlockSpec((None, KV_W, tm), lambda i: (i // tps, 0, i % tps))
    k_rm = jax.ShapeDtypeStruct((N_KV_HEADS, rows, LANES), BF16)
    k_rm_blk = pl.BlockSpec((N_KV_HEADS, tm, LANES), lambda i: (0, i, 0))
    v_t = jax.ShapeDtypeStruct((N_KV_HEADS, V_ROWS, rows), BF16)
    v_t_blk = pl.BlockSpec((N_KV_HEADS, V_ROWS, tm), lambda i: (0, 0, i))
    out_shape = (
        jax.ShapeDtypeStruct((D_ATTN, rows), BF16),
        jax.ShapeDtypeStruct((2, rows, LANES), F32),
        kv_t, kv_t, kv_t,
        k_rm,
        v_t,
        k_rm,
        v_t,
        jax.ShapeDtypeStruct((N_KV_HEADS, GATE_ROWS, rows), F32),
        jax.ShapeDtypeStruct((rows, 512), F32),
        jax.ShapeDtypeStruct((rows, 512), F32),
        jax.ShapeDtypeStruct((rows, 512), F32),
    )
    out_specs = (
        pl.BlockSpec((D_ATTN, tm), lambda i: (0, i)),
        pl.BlockSpec((2, tm, LANES), lambda i: (0, i, 0)),
        kv_t_blk, kv_t_blk, kv_t_blk,
        k_rm_blk, v_t_blk, k_rm_blk, v_t_blk,
        pl.BlockSpec((N_KV_HEADS, GATE_ROWS, tm), lambda i: (0, 0, i)),
        row_blk(512), row_blk(512), row_blk(512),
    )
    return pl.pallas_call(
        functools.partial(_proj_kernel, tm=tm, seq_len=seq_len),
        grid=(nt,),
        in_specs=[row_blk(x2d.shape[1]),
                  pl.BlockSpec((1, x2d.shape[1]), lambda i: (0, 0)),
                  pl.BlockSpec(w_r.shape, lambda i: (0, 0))],
        out_specs=out_specs,
        out_shape=out_shape,
        compiler_params=_cparams(("arbitrary",)),
        name="proj",
    )(x2d, norm_g, w_r)


def _compress_hidden(load_rows, pe_ref, w1_ref, kv, m):
    xs = [load_rows(s) for s in range(CMP_STRIDE)]
    hs = []
    for sub in range(CMP_BLOCK // CMP_STRIDE):
        lhs = jnp.concatenate([(xs[s] + pe_ref[kv, sub, s]).astype(BF16) for s in range(CMP_STRIDE)], axis=1)
        hs.append(jnp.dot(lhs, w1_ref[kv, sub], preferred_element_type=F32))
    return hs[0] + pltpu.roll(hs[1], m - 1, 0)


def _cover_matrix(nc_pad, ns_pad):
    c0 = lax.broadcasted_iota(jnp.int32, (nc_pad, ns_pad), 0) * CMP_STRIDE
    s0 = lax.broadcasted_iota(jnp.int32, (nc_pad, ns_pad), 1) * SEL_BLOCK
    return jnp.where((c0 < s0 + SEL_BLOCK) & (c0 + CMP_BLOCK > s0), 1.0, 0.0).astype(F32)


def _compress_prompt_kernel(c_ref, pe_ref, w1_ref, w2_ref, kc_ref, rhs_ref, *, nchunk):
    ones_rows = jnp.where(lax.broadcasted_iota(jnp.int32, (HEAD_DIM, nchunk), 0) == 0, 1.0, 0.0).astype(BF16)
    s0 = lax.broadcasted_iota(jnp.int32, (LANES, nchunk), 0) * SEL_BLOCK
    c0 = lax.broadcasted_iota(jnp.int32, (LANES, nchunk), 1) * CMP_STRIDE
    cover_t = jnp.where((c0 < s0 + SEL_BLOCK) & (c0 + CMP_BLOCK > s0), 1.0, 0.0).astype(BF16)
    for kv in range(2):
        load = lambda s: c_ref[kv, pl.ds(s, nchunk, stride=CMP_STRIDE), :]
        h = _compress_hidden(load, pe_ref, w1_ref, kv, nchunk)
        full = jnp.dot(_silu(h).astype(BF16), w2_ref[kv], preferred_element_type=F32)
        if kv == 0:
            kc_ref[0, :, :] = full[:, 0:HEAD_DIM].astype(BF16)
            kc_ref[1, :, :] = full[:, HEAD_DIM:2 * HEAD_DIM].astype(BF16)
        else:
            full_t = full.T
            for kvh in range(N_KV_HEADS):
                rhs_ref[kvh, 0:HEAD_DIM, :] = full_t[kvh * HEAD_DIM:(kvh + 1) * HEAD_DIM, :].astype(BF16)
                rhs_ref[kvh, HEAD_DIM:LANES, :] = ones_rows
                rhs_ref[kvh, LANES:2 * LANES, :] = cover_t


def _compress_prompt(crm, n, pe_t, w1bd, w2bd):
    seq_len = crm.shape[1] // n
    nchunk = seq_len // CMP_STRIDE
    return pl.pallas_call(
        functools.partial(_compress_prompt_kernel, nchunk=nchunk),
        grid=(n,),
        in_specs=[pl.BlockSpec((2, seq_len, LANES), lambda b: (0, b, 0)),
                  pl.BlockSpec(pe_t.shape, lambda b: (0, 0, 0, 0, 0)),
                  pl.BlockSpec(w1bd.shape, lambda b: (0, 0, 0, 0)),
                  pl.BlockSpec(w2bd.shape, lambda b: (0, 0, 0))],
        out_specs=(pl.BlockSpec((None, N_KV_HEADS, nchunk, HEAD_DIM), lambda b: (b, 0, 0, 0)),
                   pl.BlockSpec((None, N_KV_HEADS, 2 * LANES, nchunk), lambda b: (b, 0, 0, 0))),
        out_shape=(jax.ShapeDtypeStruct((n, N_KV_HEADS, nchunk, HEAD_DIM), BF16),
                   jax.ShapeDtypeStruct((n, N_KV_HEADS, 2 * LANES, nchunk), BF16)),
        compiler_params=_cparams(("arbitrary",)),
        name="compress_prompt",
    )(crm, pe_t, w1bd, w2bd)


def _not_selected(val, n_top, axis):
    blk = lax.broadcasted_iota(jnp.int32, val.shape, axis).astype(F32)
    notsel = jnp.ones(val.shape, F32)
    for _ in range(n_top):
        m = jnp.max(val, axis=axis, keepdims=True)
        idx = jnp.min(jnp.where(val == m, blk, float(LANES)), axis=axis, keepdims=True)
        pick = blk == idx
        notsel = jnp.where(pick, 0.0, notsel)
        val = jnp.where(pick, -3e38, val)
    return notsel


def _masked_importance(imp, t, n_blocks, axis):
    blk = lax.broadcasted_iota(jnp.int32, imp.shape, axis)
    cur = t // SEL_BLOCK
    forced = (blk == 0) | (blk == cur) | (blk == cur - 1)
    val = jnp.where(forced, 1e9, jnp.where(blk * SEL_BLOCK <= t, imp, -1e9))
    return jnp.where(blk < n_blocks, val, -2e9)


def _attn_prompt_kernel(qt_ref, gatet_ref, sza_ref, kc_ref, rhs_ref, ks_ref, vst_ref, kw_ref, vwt_ref, out_ref,
                        qsel_ref, sbuf_ref, m_ref, acc_ref, mw_ref, accw_ref, oc_ref,
                        *, n_sel_blocks, seq_len):
    qb = pl.program_id(2)
    qs = qb * Q_BLOCK
    hq = GQA * Q_BLOCK
    t_row = qs + lax.broadcasted_iota(jnp.int32, (1, Q_BLOCK), 1)
    t_all = jnp.concatenate([t_row] * GQA, axis=1)
    qt = qt_ref[...]
    qt_all = jnp.concatenate([qt[g * HEAD_DIM:(g + 1) * HEAD_DIM, :] for g in range(GQA)], axis=1)
    zeros_lo = jnp.zeros((HEAD_DIM, hq), BF16)
    qw = jnp.concatenate([qt_all, zeros_lo], axis=0)

    tile = min(KEY_TILE, seq_len)
    n_full = qs // tile
    lo_tiles = HEAD_DIM * SEL_BLOCK // tile
    wwidth = min(WINDOW, seq_len)
    causal = lambda kpos: kpos <= t_all
    band = lambda kpos: (kpos > t_all - WINDOW) & (kpos <= t_all)
    diag_w = lambda kpos: (kpos <= t_all) & (qs >= wwidth)

    def flash_steps(steps):
        staged = []
        for (q, k_ref, vt_ref, start, width, mask_fn, mref, aref, c) in steps:
            start = pl.multiple_of(start, LANES)
            s = jnp.dot(k_ref[pl.ds(start, width), :], q, preferred_element_type=F32)
            if mask_fn is not None:
                s = jnp.where(mask_fn(start + lax.broadcasted_iota(jnp.int32, (width, hq), 0)), s, NEG_FILL)
            staged.append((s, vt_ref[:, pl.ds(start, width)]))
        softmaxed = []
        for (s, vt), step in zip(staged, steps):
            mref, c = step[6], step[8]
            m_old = mref[c]
            m_new = jnp.maximum(m_old, jnp.max(s, axis=0, keepdims=True))
            softmaxed.append((jnp.exp2(s - m_new).astype(BF16), jnp.exp2(m_old - m_new), m_new, vt))
        for (p, alpha, m_new, vt), step in zip(softmaxed, steps):
            mref, aref, c = step[6], step[7], step[8]
            aref[c] = alpha * aref[c] + jnp.dot(vt, p, preferred_element_type=F32)
            mref[c] = m_new

    m_ref[...] = jnp.full(m_ref.shape, NEG_FILL, F32)
    acc_ref[...] = jnp.zeros(acc_ref.shape, F32)
    mw_ref[...] = jnp.full(mw_ref.shape, NEG_FILL, F32)
    accw_ref[...] = jnp.zeros(accw_ref.shape, F32)

    nc = kc_ref.shape[0]
    c_last = lax.broadcasted_iota(jnp.int32, (nc, hq), 0) * CMP_STRIDE + (CMP_BLOCK - 1)
    ok_c = c_last <= t_all
    s = jnp.dot(kc_ref[...], qt_all, preferred_element_type=F32)
    flash_steps([(qw, kw_ref, vwt_ref, jnp.maximum(qs - wwidth, 0), wwidth, band, mw_ref, accw_ref, 0),
                 (qw, kw_ref, vwt_ref, qs, Q_BLOCK, diag_w, mw_ref, accw_ref, 1)])
    s = jnp.where(ok_c, s, NEG_FILL)
    e = jnp.where(ok_c, jnp.exp2(s - jnp.max(s, axis=0, keepdims=True)), 0.0)
    r = jnp.dot(rhs_ref[...], e.astype(BF16), preferred_element_type=F32)
    inv = 1.0 / jnp.maximum(r[HEAD_DIM:HEAD_DIM + 1, :], 1e-30)
    oc_ref[...] = r[0:HEAD_DIM, :] * inv
    imp_all = r[LANES:2 * LANES, :] * inv
    imp = imp_all[:, 0:Q_BLOCK]
    for g in range(1, GQA):
        imp = imp + imp_all[:, g * Q_BLOCK:(g + 1) * Q_BLOCK]

    notsel = _not_selected(_masked_importance(imp, t_row, n_sel_blocks, 0), min(N_SEL, n_sel_blocks), 0)
    notsel = notsel.astype(BF16)
    q_lo = jnp.concatenate([qt_all, jnp.concatenate([notsel[0:HEAD_DIM, :]] * GQA, axis=1)], axis=0)
    q_hi = jnp.concatenate([qt_all, jnp.concatenate([notsel[HEAD_DIM:2 * HEAD_DIM, :]] * GQA, axis=1)], axis=0)
    qsel_ref[0] = q_lo
    qsel_ref[1] = q_hi

    def scores(b, kt, masked):
        start = pl.multiple_of(kt * tile, LANES)
        q = qsel_ref[jnp.where(kt >= lo_tiles, 1, 0)]
        s = jnp.dot(ks_ref[pl.ds(start, tile), :], q, preferred_element_type=F32)
        if masked:
            s = jnp.where(causal(start + lax.broadcasted_iota(jnp.int32, (tile, hq), 0)), s, NEG_FILL)
        sbuf_ref[b] = s

    def consume(b, kt):
        start = pl.multiple_of(kt * tile, LANES)
        s = sbuf_ref[b]
        m_old = m_ref[b]
        m_new = jnp.maximum(m_old, jnp.max(s, axis=0, keepdims=True))
        p = jnp.exp2(s - m_new).astype(BF16)
        acc_ref[b] = jnp.exp2(m_old - m_new) * acc_ref[b] + jnp.dot(vst_ref[:, pl.ds(start, tile)], p,
                                                                  preferred_element_type=F32)
        m_ref[b] = m_new

    @pl.when(n_full == 0)
    def _():
        scores(0, 0, True)

    @pl.when(n_full > 0)
    def _():
        scores(0, 0, False)

    def pair_body(j, carry):
        scores(1, 2 * j + 1, False)
        consume(0, 2 * j)
        scores(0, 2 * j + 2, False)
        consume(1, 2 * j + 1)
        return carry

    lax.fori_loop(0, jnp.maximum(n_full - 1, 0) // 2, pair_body, 0)

    @pl.when((n_full > 0) & (n_full % 2 == 0))
    def _():
        scores(1, n_full - 1, False)
        consume(0, n_full - 2)
        scores(0, n_full, True)
        consume(1, n_full - 1)
        consume(0, n_full)

    @pl.when(n_full % 2 == 1)
    def _():
        scores(1, n_full, True)
        consume(0, n_full - 1)
        consume(1, n_full)

    @pl.when(n_full == 0)
    def _():
        consume(0, 0)

    def merged(mref, aref, n_chain):
        m = mref[0]
        for c in range(1, n_chain):
            m = jnp.maximum(m, mref[c])
        a = jnp.exp2(mref[0] - m) * aref[0]
        for c in range(1, n_chain):
            a = a + jnp.exp2(mref[c] - m) * aref[c]
        return a[0:HEAD_DIM, :] * (1.0 / a[HEAD_DIM:HEAD_DIM + 1, :])

    gates = gatet_ref[...]
    sza = sza_ref[...]
    o_s = merged(m_ref, acc_ref, SEL_CHAINS)
    o_w = merged(mw_ref, accw_ref, 2)
    o_c = oc_ref[...]
    outs = []
    for g in range(GQA):
        cols = slice(g * Q_BLOCK, (g + 1) * Q_BLOCK)
        gc = gates[g * N_BRANCH + 0:g * N_BRANCH + 1, :]
        gs = gates[g * N_BRANCH + 1:g * N_BRANCH + 2, :]
        gw = gates[g * N_BRANCH + 2:g * N_BRANCH + 3, :]
        outs.append((gc * o_c[:, cols] + gs * o_s[:, cols] + gw * o_w[:, cols]).T)
    out_ref[...] = (jnp.concatenate(outs, axis=1) * sza).astype(BF16)


def _attn_prompt(qt, gates_t, sza3, kc, rhs_c, ks, vst, kw, vwt):
    n, seq_len, _ = sza3.shape
    nqb = seq_len // Q_BLOCK
    nchunk = kc.shape[2]
    hw = GQA * HEAD_DIM
    k_blk = pl.BlockSpec((None, seq_len, LANES), lambda b, k, i: (k, b, 0))
    vt_blk = pl.BlockSpec((None, V_ROWS, seq_len), lambda b, k, i: (k, 0, b))
    return pl.pallas_call(
        functools.partial(_attn_prompt_kernel, n_sel_blocks=-(-seq_len // SEL_BLOCK), seq_len=seq_len),
        grid=(n, N_KV_HEADS, nqb),
        in_specs=[
            pl.BlockSpec((hw, Q_BLOCK), lambda b, k, i: (k, b * nqb + i)),
            pl.BlockSpec((None, GATE_ROWS, Q_BLOCK), lambda b, k, i: (k, 0, b * nqb + i)),
            pl.BlockSpec((None, Q_BLOCK, hw), lambda b, k, i: (b, i, k)),
            pl.BlockSpec((None, None, nchunk, HEAD_DIM), lambda b, k, i: (b, k, 0, 0)),
            pl.BlockSpec((None, None, 2 * LANES, nchunk), lambda b, k, i: (b, k, 0, 0)),
            k_blk, vt_blk, k_blk, vt_blk,
        ],
        out_specs=pl.BlockSpec((None, Q_BLOCK, hw), lambda b, k, i: (b, i, k)),
        out_shape=jax.ShapeDtypeStruct((n, seq_len, D_ATTN), BF16),
        scratch_shapes=[
            pltpu.VMEM((2, 2 * HEAD_DIM, GQA * Q_BLOCK), BF16),
            pltpu.VMEM((SEL_CHAINS, min(KEY_TILE, seq_len), GQA * Q_BLOCK), F32),
            pltpu.VMEM((SEL_CHAINS, 1, GQA * Q_BLOCK), F32),
            pltpu.VMEM((SEL_CHAINS, V_ROWS, GQA * Q_BLOCK), F32),
            pltpu.VMEM((2, 1, GQA * Q_BLOCK), F32),
            pltpu.VMEM((2, V_ROWS, GQA * Q_BLOCK), F32),
            pltpu.VMEM((HEAD_DIM, GQA * Q_BLOCK), F32),
        ],
        compiler_params=_cparams(("arbitrary", "arbitrary", "arbitrary")),
        name="attn_prompt",
    )(qt, gates_t, sza3, kc, rhs_c, ks, vst, kw, vwt)


def _pool_out(d, pw_ref, ps_ref, szp):
    ys = [jnp.dot(d[:, g * POOL_GROUP_DIM:(g + 1) * POOL_GROUP_DIM].astype(BF16), pw_ref[g],
                  preferred_element_type=F32) for g in range(len(POOL_WINDOWS))]
    return jnp.concatenate(ys, axis=1) * ps_ref[...] * szp


def _finish(x, a_bf16, b, wo_ref, fg_ref):
    mix = jnp.concatenate([a_bf16, b.astype(BF16)], axis=1)
    y = x + jnp.dot(mix, wo_ref[...], preferred_element_type=F32)
    ms = jnp.mean(y * y, axis=-1, keepdims=True)
    return y * lax.rsqrt(ms + RMS_EPS) * fg_ref[...]


def _out_prompt_kernel(a_ref, u_ref, halo_ref, szp_ref, x_ref, pw_ref, ps_ref, wo_ref, fg_ref, y_ref, ext_ref,
                       *, tm, halo):
    i = pl.program_id(1)
    u = u_ref[...]
    ext_ref[0:halo, :] = jnp.where(i > 0, halo_ref[...], 0.0)
    ext_ref[halo:halo + tm, :] = u
    pos = i * tm + lax.broadcasted_iota(jnp.int32, (tm, POOL_GROUP_DIM), 0)
    ds = []
    for g, w in enumerate(POOL_WINDOWS):
        c0 = g * POOL_GROUP_DIM
        acc = u[:, c0:c0 + POOL_GROUP_DIM]
        for k in range(1, w):
            acc = acc + ext_ref[halo - k:halo - k + tm, c0:c0 + POOL_GROUP_DIM]
        cnt = jnp.minimum(pos + 1, w).astype(F32)
        ds.append(acc / cnt - u[:, c0:c0 + POOL_GROUP_DIM])
    b = _pool_out(jnp.concatenate(ds, axis=1), pw_ref, ps_ref, szp_ref[...])
    y_ref[...] = _finish(x_ref[...], a_ref[...], b, wo_ref, fg_ref)


def _out_prompt(a3, u3, szp3, x3, pool_w, pool_scale, w_out, final_g, *, tm):
    n, seq_len, d_model = x3.shape
    halo = 16
    nt = seq_len // tm
    blk = lambda w: pl.BlockSpec((None, tm, w), lambda b, i: (b, i, 0))
    const = lambda a: pl.BlockSpec(a.shape, lambda b, i: (0,) * a.ndim)
    return pl.pallas_call(
        functools.partial(_out_prompt_kernel, tm=tm, halo=halo),
        grid=(n, nt),
        in_specs=[blk(D_ATTN), blk(D_POOL),
                  pl.BlockSpec((None, halo, D_POOL), lambda b, i: (b, jnp.maximum(i * (tm // halo) - 1, 0), 0)),
                  blk(D_POOL), blk(d_model),
                  const(pool_w), const(pool_scale), const(w_out), const(final_g)],
        out_specs=blk(d_model),
        out_shape=jax.ShapeDtypeStruct((n, seq_len, d_model), F32),
        scratch_shapes=[pltpu.VMEM((tm + halo, D_POOL), F32)],
        compiler_params=_cparams(("arbitrary", "arbitrary")),
        name="out_prompt",
    )(a3, u3, u3, szp3, x3, pool_w, pool_scale, w_out, final_g)


def _out_sample_kernel(o_ref, sza_ref, u_ref, st_ref, szp_ref, x_ref, pw_ref, ps_ref, wo_ref, fg_ref, y_ref):
    u = u_ref[...]
    ds = []
    for g, w in enumerate(POOL_WINDOWS):
        c0 = g * POOL_GROUP_DIM
        acc = u[:, c0:c0 + POOL_GROUP_DIM]
        for k in range(1, w):
            acc = acc + st_ref[POOL_STATE - k, :, c0:c0 + POOL_GROUP_DIM]
        ds.append(acc / float(w) - u[:, c0:c0 + POOL_GROUP_DIM])
    b = _pool_out(jnp.concatenate(ds, axis=1), pw_ref, ps_ref, szp_ref[...])
    a = (o_ref[...] * sza_ref[...]).astype(BF16)
    y_ref[...] = _finish(x_ref[...], a, b, wo_ref, fg_ref)


def _out_sample(o2, sza, u, state_t, szp, x2, pool_w, pool_scale, w_out, final_g):
    args = (o2, sza, u, state_t, szp, x2, pool_w, pool_scale, w_out, final_g)
    full = lambda a: pl.BlockSpec(a.shape, lambda i: (0,) * a.ndim)
    return pl.pallas_call(
        _out_sample_kernel,
        grid=(1,),
        in_specs=[full(a) for a in args],
        out_specs=full(x2),
        out_shape=jax.ShapeDtypeStruct(x2.shape, F32),
        compiler_params=_cparams(("arbitrary",)),
        name="out_sample",
    )(*args)


def _cmp_sample_kernel(pt_ref, *refs, n_pages, page):
    del pt_ref
    pages = refs[:n_pages]
    qbd_ref, pe_ref, w1_ref, w2_ref, oc_ref, imp_ref, rows_ref = refs[n_pages:]
    nchunk = n_pages * page // CMP_STRIDE
    for p in range(n_pages):
        for kv in range(2):
            rows_ref[kv, p * page:(p + 1) * page, :] = pages[p][kv * LANES:(kv + 1) * LANES, :].T
    fulls = []
    for kv in range(2):
        load = lambda s: rows_ref[kv, pl.ds(s, nchunk, stride=CMP_STRIDE), :]
        h = _compress_hidden(load, pe_ref, w1_ref, kv, nchunk)
        fulls.append(jnp.dot(_silu(h).astype(BF16), w2_ref[kv], preferred_element_type=F32))
    k_c, v_c = fulls
    qbd = qbd_ref[...]
    s = lax.dot_general(qbd, k_c.astype(BF16), (((1,), (1,)), ((), ())), preferred_element_type=F32)
    ok = lax.broadcasted_iota(jnp.int32, s.shape, 1) < nchunk - 1
    s = jnp.where(ok, s, NEG_FILL)
    e = jnp.where(ok, jnp.exp2(s - jnp.max(s, axis=1, keepdims=True)), 0.0)
    pc = (e * (1.0 / jnp.sum(e, axis=1, keepdims=True))).astype(BF16)
    oc_ref[...] = jnp.dot(pc, v_c.astype(BF16), preferred_element_type=F32)
    imp_ref[...] = jnp.dot(pc, _cover_matrix(nchunk, LANES).astype(BF16), preferred_element_type=F32)


def _cmp_sample(page_table, cache_t, qbd, pe_t, w1bd, w2bd):
    nb, n_pages = page_table.shape
    page = cache_t.shape[-1]
    page_spec = lambda p: pl.BlockSpec((None, KV_W, page), lambda b, pt, p=p: (pt[b, p], 0, 0))
    const = lambda a: pl.BlockSpec(a.shape, lambda b, pt: (0,) * a.ndim)
    per_b = pl.BlockSpec((None, N_HEADS, LANES), lambda b, pt: (b, 0, 0))
    grid_spec = pltpu.PrefetchScalarGridSpec(
        num_scalar_prefetch=1,
        grid=(nb,),
        in_specs=[page_spec(p) for p in range(n_pages)] + [per_b, const(pe_t), const(w1bd), const(w2bd)],
        out_specs=(per_b, per_b),
        scratch_shapes=[pltpu.VMEM((2, n_pages * page, LANES), F32)],
    )
    return pl.pallas_call(
        functools.partial(_cmp_sample_kernel, n_pages=n_pages, page=page),
        grid_spec=grid_spec,
        out_shape=(jax.ShapeDtypeStruct((nb, N_HEADS, LANES), F32),
                   jax.ShapeDtypeStruct((nb, N_HEADS, LANES), F32)),
        compiler_params=_cparams(("arbitrary",)),
        name="cmp_sample",
    )(page_table, *([cache_t] * n_pages), qbd, pe_t, w1bd, w2bd)


def _topk_sample_kernel(imp_ref, out_ref, *, t_pos, n_blocks):
    imp = imp_ref[0] + imp_ref[1] + imp_ref[2] + imp_ref[3]
    t = jnp.full((imp.shape[0], 1), t_pos, jnp.int32)
    out_ref[...] = _not_selected(_masked_importance(imp, t, n_blocks, 1), min(N_SEL, n_blocks), 1)


def _topk_sample(imp_g, *, t_pos, n_blocks):
    rows = imp_g.shape[1]
    return pl.pallas_call(
        functools.partial(_topk_sample_kernel, t_pos=t_pos, n_blocks=n_blocks),
        grid=(1,),
        in_specs=[pl.BlockSpec(imp_g.shape, lambda i: (0, 0, 0))],
        out_specs=pl.BlockSpec((rows, LANES), lambda i: (0, 0)),
        out_shape=jax.ShapeDtypeStruct((rows, LANES), F32),
        compiler_params=_cparams(("arbitrary",)),
        name="topk_sample",
    )(imp_g)


def _attn_sample_kernel(pt_ref, *refs, n_pages, page):
    del pt_ref
    pages = refs[:n_pages]
    (win_ref, qbd_ref, ns_ref, snew_ref, wnew_ref, wcol_ref, oc_ref, gate_ref, o_ref, nwin_ref) = refs[n_pages:]
    qbd = qbd_ref[...]
    qf = qbd.astype(F32)
    ns = ns_ref[...]
    nt_ = (((1,), (1,)), ((), ()))

    def new_token(row_ref):
        k_new = row_ref[:, 0:LANES].astype(BF16).astype(F32)
        v_new = row_ref[:, LANES:2 * LANES].astype(BF16).astype(F32)
        return jnp.sum(qf * k_new, axis=1, keepdims=True), v_new

    lane = lax.broadcasted_iota(jnp.int32, (N_HEADS, page), 1)
    per_page = page // SEL_BLOCK
    scores = []
    for p in range(n_pages):
        s = jnp.dot(qbd, pages[p][0:LANES, :].astype(BF16), preferred_element_type=F32)
        flag = ns[:, p * per_page:p * per_page + 1]
        for j in range(1, per_page):
            flag = jnp.where(lane >= j * SEL_BLOCK, ns[:, p * per_page + j:p * per_page + j + 1], flag)
        scores.append(jnp.where(flag > 0.5, NEG_FILL, s))
    nb_cache = n_pages * per_page
    s_new, v_new = new_token(snew_ref)
    s_new = jnp.where(ns[:, nb_cache:nb_cache + 1] > 0.5, NEG_FILL, s_new)
    m = s_new
    for s in scores:
        m = jnp.maximum(m, jnp.max(s, axis=1, keepdims=True))
    e_new = jnp.exp2(s_new - m)
    l = e_new
    acc = e_new * v_new
    for p in range(n_pages):
        e = jnp.exp2(scores[p] - m).astype(BF16)
        l = l + jnp.sum(e.astype(F32), axis=1, keepdims=True)
        acc = acc + lax.dot_general(e, pages[p][LANES:2 * LANES, :].astype(BF16), nt_, preferred_element_type=F32)
    o_s = acc * (1.0 / l)

    win = win_ref[...]
    wlen = win.shape[1]
    s_w = jnp.dot(qbd, win[0:LANES, :].astype(BF16), preferred_element_type=F32)
    first = max(wlen - WINDOW + 1, 0)
    s_w = jnp.where(lax.broadcasted_iota(jnp.int32, s_w.shape, 1) >= first, s_w, NEG_FILL)
    sw_new, vw_new = new_token(wnew_ref)
    m_w = jnp.maximum(sw_new, jnp.max(s_w, axis=1, keepdims=True))
    ew_new = jnp.exp2(sw_new - m_w)
    e_w = jnp.exp2(s_w - m_w).astype(BF16)
    l_w = ew_new + jnp.sum(e_w.astype(F32), axis=1, keepdims=True)
    acc_w = ew_new * vw_new + lax.dot_general(e_w, win[LANES:2 * LANES, :].astype(BF16), nt_,
                                              preferred_element_type=F32)
    o_w = acc_w * (1.0 / l_w)

    gates = gate_ref[...]
    o_ref[...] = gates[:, 0:1] * oc_ref[...] + gates[:, 1:2] * o_s + gates[:, 2:3] * o_w

    shifted = pltpu.roll(win, wlen - 1, 1)
    last = lax.broadcasted_iota(jnp.int32, win.shape, 1) == wlen - 1
    nwin_ref[...] = jnp.where(last, wcol_ref[...], shifted)


def _attn_sample(page_table, cache_t, win_t, qbd, ns8, s_new, w_new, w_col, o_c, gates8):
    nb, n_pages = page_table.shape
    page = cache_t.shape[-1]
    wlen = win_t.shape[-1]
    page_spec = lambda p: pl.BlockSpec((None, KV_W, page), lambda b, pt, p=p: (pt[b, p], 0, 0))
    per_b = lambda a: pl.BlockSpec((None,) + a.shape[1:], lambda b, pt: (b,) + (0,) * (a.ndim - 1))
    grid_spec = pltpu.PrefetchScalarGridSpec(
        num_scalar_prefetch=1,
        grid=(nb,),
        in_specs=[page_spec(p) for p in range(n_pages)]
        + [per_b(a) for a in (win_t, qbd, ns8, s_new, w_new, w_col, o_c, gates8)],
        out_specs=(per_b(o_c), per_b(win_t)),
    )
    return pl.pallas_call(
        functools.partial(_attn_sample_kernel, n_pages=n_pages, page=page),
        grid_spec=grid_spec,
        out_shape=(jax.ShapeDtypeStruct(o_c.shape, F32), jax.ShapeDtypeStruct(win_t.shape, F32)),
        compiler_params=_cparams(("arbitrary",)),
        name="attn_sample",
    )(page_table, *([cache_t] * n_pages), win_t, qbd, ns8, s_new, w_new, w_col, o_c, gates8)


def _prep_w_in(w_in):
    o = D_ATTN + 3 * KV_W
    ng = GQA * N_BRANCH
    gl = w_in[:, o:o + N_KV_HEADS * ng]
    rest = w_in[:, o + N_KV_HEADS * ng:]
    pad = jnp.zeros((w_in.shape[0], LANES - ng), w_in.dtype)
    return jnp.concatenate([w_in[:, :o], rest, gl[:, :ng], pad, gl[:, ng:], pad], axis=1).astype(BF16)


def _prep_compress(cmp_pe, cmp_w1, cmp_w2):
    r = CMP_BLOCK // CMP_STRIDE
    pe_t = jnp.tile(cmp_pe.reshape(2, r, CMP_STRIDE, 1, HEAD_DIM), (1, 1, 1, 1, N_KV_HEADS))
    w1 = cmp_w1.reshape(2, r, CMP_STRIDE, HEAD_DIM, CMP_HIDDEN)
    z1 = jnp.zeros_like(w1)
    top = jnp.concatenate([w1, z1], axis=-1)
    bot = jnp.concatenate([z1, w1], axis=-1)
    w1bd = jnp.concatenate([top, bot], axis=3)
    w1bd = w1bd.reshape(2, r, CMP_STRIDE * LANES, N_KV_HEADS * CMP_HIDDEN).astype(BF16)
    z2 = jnp.zeros_like(cmp_w2)
    w2bd = jnp.concatenate([jnp.concatenate([cmp_w2, z2], axis=-1),
                            jnp.concatenate([z2, cmp_w2], axis=-1)], axis=1).astype(BF16)
    return pe_t, w1bd, w2bd


def _kv_out(kv_t):
    n, _, rows = kv_t.shape
    return jnp.transpose(kv_t.reshape(n, 2, N_KV_HEADS, HEAD_DIM, rows), (0, 4, 1, 2, 3))[None]


def kernel(x_prompt, x_sample, cache_cmp_kv, cache_slc_kv, cache_win_kv, state_pool, page_table, norm_g, w_in,
           cmp_pe, cmp_w1, cmp_w2, pool_w, pool_scale, w_out, final_g):
    n, seq_len, d_model = x_prompt.shape
    nb = x_sample.shape[0]
    n_phys, page = cache_cmp_kv.shape[1], cache_cmp_kv.shape[2]
    n_pages = page_table.shape[1]
    past_len = n_pages * page
    wlen = cache_win_kv.shape[2]

    w_r = _prep_w_in(w_in[0])
    pe_t, w1bd, w2bd = _prep_compress(cmp_pe[0], cmp_w1[0], cmp_w2[0])
    pool_w_b = pool_w[0].astype(BF16)
    w_out_b = w_out[0].astype(BF16)
    fg = final_g.reshape(1, d_model)

    (qt, crm, ckv_t, skv_t, wkv_t, ks, vst, kw, vwt, gates_t, sza, u, szp) = _project(
        x_prompt.reshape(n * seq_len, d_model), norm_g, w_r, tm=256, seq_len=seq_len)
    r3 = lambda a: a.reshape(n, seq_len, a.shape[-1])
    kc, rhs_c = _compress_prompt(crm, n, pe_t, w1bd, w2bd)
    a3 = _attn_prompt(qt, gates_t, r3(sza), kc, rhs_c, ks, vst, kw, vwt)
    y_prompt = _out_prompt(a3, r3(u), r3(szp), x_prompt, pool_w_b, pool_scale, w_out_b, fg, tm=512)

    new_cmp_p = _kv_out(ckv_t)
    new_slc_p = _kv_out(skv_t)
    new_win_p = _kv_out(wkv_t[:, :, seq_len - min(WINDOW, seq_len):])
    new_pool_p = r3(u)[:, seq_len - POOL_STATE:][None]

    (qt_s, _, ckv_ts, skv_ts, wkv_ts, _, _, _, _, gates_ts, sza_s, u_s, szp_s) = _project(
        x_sample.reshape(nb, d_model), norm_g, w_r, tm=nb, seq_len=nb)
    q_s = qt_s.T
    q5 = q_s.reshape(nb, N_KV_HEADS, GQA, 1, HEAD_DIM)
    eye = jnp.eye(N_KV_HEADS, dtype=q_s.dtype).reshape(1, N_KV_HEADS, 1, N_KV_HEADS, 1)
    qbd = (q5 * eye).reshape(nb, N_HEADS, LANES)

    to_pages = lambda c: jnp.transpose(c[0], (0, 2, 3, 4, 1)).reshape(n_phys, KV_W, page)
    o_c, imp8 = _cmp_sample(page_table, to_pages(cache_cmp_kv), qbd, pe_t, w1bd, w2bd)
    imp_g = jnp.transpose(imp8.reshape(nb * N_KV_HEADS, GQA, LANES), (1, 0, 2))
    n_blocks = -(-(past_len + 1) // SEL_BLOCK)
    notsel = _topk_sample(imp_g, t_pos=past_len, n_blocks=n_blocks)
    ns8 = jnp.repeat(notsel.reshape(nb, N_KV_HEADS, 1, LANES), GQA, axis=2).reshape(nb, N_HEADS, LANES)

    gates8 = jnp.transpose(gates_ts[:, :GQA * N_BRANCH, :], (2, 0, 1)).reshape(nb, N_HEADS, N_BRANCH)
    gates8 = jnp.pad(gates8, ((0, 0), (0, 0), (0, LANES - N_BRANCH)))
    win_t = jnp.transpose(cache_win_kv[0], (0, 2, 3, 4, 1)).reshape(nb, KV_W, wlen)
    s_new = skv_ts[0].T.reshape(nb, 1, KV_W)
    w_new = wkv_ts[0].T.reshape(nb, 1, KV_W)
    w_col = w_new.reshape(nb, KV_W, 1)
    o8, nwin_t = _attn_sample(page_table, to_pages(cache_slc_kv), win_t, qbd, ns8, s_new, w_new, w_col, o_c, gates8)
    o5 = o8.reshape(nb, N_KV_HEADS, GQA, N_KV_HEADS, HEAD_DIM)
    o2 = jnp.stack([o5[:, k, :, k, :] for k in range(N_KV_HEADS)], axis=1).reshape(nb, D_ATTN)

    state_t = jnp.transpose(state_pool[0], (1, 0, 2))
    y_sample = _out_sample(o2, sza_s, u_s, state_t, szp_s, x_sample.reshape(nb, d_model),
                           pool_w_b, pool_scale, w_out_b, fg).reshape(nb, 1, d_model)

    kv_out_s = lambda t: jnp.transpose(t.reshape(2, N_KV_HEADS, HEAD_DIM, nb, 1), (3, 4, 0, 1, 2))[None]
    new_cmp_s = kv_out_s(ckv_ts)
    new_slc_s = kv_out_s(skv_ts)
    new_win_s = jnp.transpose(nwin_t.reshape(nb, 2, N_KV_HEADS, HEAD_DIM, wlen), (0, 4, 1, 2, 3))[None]
    new_pool_s = jnp.transpose(jnp.concatenate([state_t[1:], u_s[None]], axis=0), (1, 0, 2))[None]

    return (y_prompt, y_sample, new_cmp_p, new_slc_p, new_win_p, new_pool_p,
            new_cmp_s, new_slc_s, new_win_s, new_pool_s)
```

```python
import functools

import jax
import jax.numpy as jnp
from jax import lax
from jax.experimental import pallas as pl
from jax.experimental.pallas import tpu as pltpu

F32 = jnp.float32
BF16 = jnp.bfloat16

HEAD_DIM = 64
N_KV_HEADS = 2
GQA = 4
N_HEADS = N_KV_HEADS * GQA
D_ATTN = N_HEADS * HEAD_DIM
KV_W = 2 * N_KV_HEADS * HEAD_DIM
N_BRANCH = 3
D_POOL = 512
CMP_BLOCK = 32
CMP_STRIDE = 16
CMP_HIDDEN = 128
SEL_BLOCK = 64
N_SEL = 16
WINDOW = 512
Q_BLOCK = 256
POOL_WINDOWS = (2, 4, 8, 16)
POOL_GROUP_DIM = 128
POOL_STATE = 15
RMS_EPS = 1e-6
LANES = 128
NEG_BIAS = -(2.0 ** 30)
NEG_FILL = -1e30
GATE_COLS = 2 * LANES
GATE_ROWS = 16
BF16_SUBLANES = 16
V_ROWS = HEAD_DIM + BF16_SUBLANES
KEY_TILE = 512
FORCED_VALUE = 1e9
N_FORCED = 3
SEL_CHAINS = 2
LOG2E = 1.4426950408889634
P_PAD = D_ATTN + 3 * KV_W + 3 * 512 + GATE_COLS
VMEM_LIMIT = 48 * 1024 * 1024


def _cparams(sem):
    return pltpu.CompilerParams(dimension_semantics=sem, vmem_limit_bytes=VMEM_LIMIT)


def _silu(z):
    return z * jax.nn.sigmoid(z)


def _proj_kernel(x_ref, g_ref, w_ref, qt_ref, crm_ref, ct_ref, st_ref, wt_ref, ks_ref, vst_ref, kw_ref, vwt_ref,
                 gatet_ref, sza_ref, u_ref, szp_ref, *, tm, seq_len):
    i = pl.program_id(0)
    x = x_ref[...]
    ms = jnp.mean(x * x, axis=-1, keepdims=True)
    xn = (x * lax.rsqrt(ms + RMS_EPS) * g_ref[...]).astype(BF16)

    def mm(c0, c1):
        return jnp.dot(xn, w_ref[:, c0:c1], preferred_element_type=F32)

    qt_ref[...] = (mm(0, D_ATTN) * (HEAD_DIM ** -0.5 * LOG2E)).T.astype(BF16)
    kv = mm(D_ATTN, D_ATTN + 3 * KV_W)
    crm_ref[0] = kv[:, 0:LANES]
    crm_ref[1] = kv[:, LANES:2 * LANES]
    kvt = kv.T
    ct_ref[...] = kvt[0:KV_W, :]
    st_ref[...] = kvt[KV_W:2 * KV_W, :]
    wt_ref[...] = kvt[2 * KV_W:3 * KV_W, :]

    lane = lax.broadcasted_iota(jnp.int32, (tm, LANES), 1)
    pos = (i * tm) % seq_len + lax.broadcasted_iota(jnp.int32, (tm, LANES), 0)
    onehot = jnp.where(lane - HEAD_DIM == (pos // SEL_BLOCK) % HEAD_DIM, NEG_BIAS, 0.0).astype(F32)
    ones_rows = jnp.where(lax.broadcasted_iota(jnp.int32, (V_ROWS - HEAD_DIM, tm), 0) == 0, 1.0, 0.0).astype(BF16)
    for kvh in range(N_KV_HEADS):
        for (base, k_dst, v_dst, k_pad) in ((KV_W, ks_ref, vst_ref, onehot), (2 * KV_W, kw_ref, vwt_ref, 0.0)):
            slab = kv[:, base:base + LANES]
            if kvh == 1:
                slab = pltpu.roll(slab, HEAD_DIM, 1)
            k_dst[kvh, :, :] = jnp.where(lane < HEAD_DIM, slab, k_pad).astype(BF16)
            v0 = base + LANES + kvh * HEAD_DIM
            v_dst[kvh, 0:HEAD_DIM, :] = kvt[v0:v0 + HEAD_DIM, :].astype(BF16)
            v_dst[kvh, HEAD_DIM:V_ROWS, :] = ones_rows

    c = D_ATTN + 3 * KV_W
    sza_ref[...] = _silu(mm(c, c + 512)).astype(BF16)
    u_ref[...] = mm(c + 512, c + 1024)
    szp_ref[...] = _silu(mm(c + 1024, c + 1536)).astype(BF16)
    gate_t = jax.nn.sigmoid(mm(c + 1536, c + 1536 + GATE_COLS)).T
    for kvh in range(N_KV_HEADS):
        gatet_ref[kvh, :, :] = gate_t[kvh * LANES:kvh * LANES + GATE_ROWS, :]


def _project(x2d, norm_g, w_r, *, tm, seq_len):
    rows = x2d.shape[0]
    nt = rows // tm
    tps = seq_len // tm
    row_blk = lambda w: pl.BlockSpec((tm, w), lambda i: (i, 0))
    kv_t = jax.ShapeDtypeStruct((rows // seq_len, KV_W, seq_len), F32)
    kv_t_blk = pl.BlockSpec((None, KV_W, tm), lambda i: (i // tps, 0, i % tps))
    k_rm = jax.ShapeDtypeStruct((N_KV_HEADS, rows, LANES), BF16)
    k_rm_blk = pl.BlockSpec((N_KV_HEADS, tm, LANES), lambda i: (0, i, 0))
    v_t = jax.ShapeDtypeStruct((N_KV_HEADS, V_ROWS, rows), BF16)
    v_t_blk = pl.BlockSpec((N_KV_HEADS, V_ROWS, tm), lambda i: (0, 0, i))
    out_shape = (
        jax.ShapeDtypeStruct((D_ATTN, rows), BF16),
        jax.ShapeDtypeStruct((2, rows, LANES), F32),
        kv_t, kv_t, kv_t,
        k_rm,
        v_t,
        k_rm,
        v_t,
        jax.ShapeDtypeStruct((N_KV_HEADS, GATE_ROWS, rows), F32),
        jax.ShapeDtypeStruct((rows, 512), BF16),
        jax.ShapeDtypeStruct((rows, 512), F32),
        jax.ShapeDtypeStruct((rows, 512), BF16),
    )
    out_specs = (
        pl.BlockSpec((D_ATTN, tm), lambda i: (0, i)),
        pl.BlockSpec((2, tm, LANES), lambda i: (0, i, 0)),
        kv_t_blk, kv_t_blk, kv_t_blk,
        k_rm_blk, v_t_blk, k_rm_blk, v_t_blk,
        pl.BlockSpec((N_KV_HEADS, GATE_ROWS, tm), lambda i: (0, 0, i)),
        row_blk(512), row_blk(512), row_blk(512),
    )
    return pl.pallas_call(
        functools.partial(_proj_kernel, tm=tm, seq_len=seq_len),
        grid=(nt,),
        in_specs=[row_blk(x2d.shape[1]),
                  pl.BlockSpec((1, x2d.shape[1]), lambda i: (0, 0)),
                  pl.BlockSpec(w_r.shape, lambda i: (0, 0))],
        out_specs=out_specs,
        out_shape=out_shape,
        compiler_params=_cparams(("arbitrary",)),
        name="proj",
    )(x2d, norm_g, w_r)


def _compress_hidden(load_rows, pe_ref, w1_ref, kv, m):
    xs = [load_rows(s) for s in range(CMP_STRIDE)]
    hs = []
    for sub in range(CMP_BLOCK // CMP_STRIDE):
        lhs = jnp.concatenate([(xs[s] + pe_ref[kv, sub, s]).astype(BF16) for s in range(CMP_STRIDE)], axis=1)
        hs.append(jnp.dot(lhs, w1_ref[kv, sub], preferred_element_type=F32))
    return hs[0] + pltpu.roll(hs[1], m - 1, 0)


def _cover_matrix(nc_pad, ns_pad):
    c0 = lax.broadcasted_iota(jnp.int32, (nc_pad, ns_pad), 0) * CMP_STRIDE
    s0 = lax.broadcasted_iota(jnp.int32, (nc_pad, ns_pad), 1) * SEL_BLOCK
    return jnp.where((c0 < s0 + SEL_BLOCK) & (c0 + CMP_BLOCK > s0), 1.0, 0.0).astype(F32)


def _compress_prompt_kernel(c_ref, pe_ref, w1_ref, w2_ref, kc_ref, rhs_ref, *, nchunk):
    ones_rows = jnp.where(lax.broadcasted_iota(jnp.int32, (HEAD_DIM, nchunk), 0) == 0, 1.0, 0.0).astype(BF16)
    s0 = lax.broadcasted_iota(jnp.int32, (LANES, nchunk), 0) * SEL_BLOCK
    c0 = lax.broadcasted_iota(jnp.int32, (LANES, nchunk), 1) * CMP_STRIDE
    cover_t = jnp.where((c0 < s0 + SEL_BLOCK) & (c0 + CMP_BLOCK > s0), 1.0, 0.0).astype(BF16)
    for kv in range(2):
        load = lambda s: c_ref[kv, pl.ds(s, nchunk, stride=CMP_STRIDE), :]
        h = _compress_hidden(load, pe_ref, w1_ref, kv, nchunk)
        full = jnp.dot(_silu(h).astype(BF16), w2_ref[kv], preferred_element_type=F32)
        if kv == 0:
            kc_ref[0, :, :] = full[:, 0:HEAD_DIM].astype(BF16)
            kc_ref[1, :, :] = full[:, HEAD_DIM:2 * HEAD_DIM].astype(BF16)
        else:
            full_t = full.T
            for kvh in range(N_KV_HEADS):
                rhs_ref[kvh, 0:HEAD_DIM, :] = full_t[kvh * HEAD_DIM:(kvh + 1) * HEAD_DIM, :].astype(BF16)
                rhs_ref[kvh, HEAD_DIM:LANES, :] = ones_rows
                rhs_ref[kvh, LANES:2 * LANES, :] = cover_t


def _compress_prompt(crm, n, pe_t, w1bd, w2bd):
    seq_len = crm.shape[1] // n
    nchunk = seq_len // CMP_STRIDE
    return pl.pallas_call(
        functools.partial(_compress_prompt_kernel, nchunk=nchunk),
        grid=(n,),
        in_specs=[pl.BlockSpec((2, seq_len, LANES), lambda b: (0, b, 0)),
                  pl.BlockSpec(pe_t.shape, lambda b: (0, 0, 0, 0, 0)),
                  pl.BlockSpec(w1bd.shape, lambda b: (0, 0, 0, 0)),
                  pl.BlockSpec(w2bd.shape, lambda b: (0, 0, 0))],
        out_specs=(pl.BlockSpec((None, N_KV_HEADS, nchunk, HEAD_DIM), lambda b: (b, 0, 0, 0)),
                   pl.BlockSpec((None, N_KV_HEADS, 2 * LANES, nchunk), lambda b: (b, 0, 0, 0))),
        out_shape=(jax.ShapeDtypeStruct((n, N_KV_HEADS, nchunk, HEAD_DIM), BF16),
                   jax.ShapeDtypeStruct((n, N_KV_HEADS, 2 * LANES, nchunk), BF16)),
        compiler_params=_cparams(("arbitrary",)),
        name="compress_prompt",
    )(crm, pe_t, w1bd, w2bd)


def _not_selected(val, n_top, axis):
    blk = lax.broadcasted_iota(jnp.int32, val.shape, axis).astype(F32)
    forced = val >= FORCED_VALUE
    notsel = jnp.where(forced, 0.0, 1.0).astype(F32)
    val = jnp.where(forced, -3e38, val)
    for _ in range(max(n_top - N_FORCED, 0)):
        m = jnp.max(val, axis=axis, keepdims=True)
        idx = jnp.min(jnp.where(val == m, blk, float(LANES)), axis=axis, keepdims=True)
        pick = blk == idx
        notsel = jnp.where(pick, 0.0, notsel)
        val = jnp.where(pick, -3e38, val)
    return notsel


def _masked_importance(imp, t, n_blocks, axis):
    blk = lax.broadcasted_iota(jnp.int32, imp.shape, axis)
    cur = t // SEL_BLOCK
    forced = (blk == 0) | (blk == cur) | (blk == cur - 1)
    val = jnp.where(forced, FORCED_VALUE, jnp.where(blk * SEL_BLOCK <= t, imp, -1e9))
    return jnp.where(blk < n_blocks, val, -2e9)


def _attn_prompt_kernel(qt_ref, gatet_ref, sza_ref, kc_ref, rhs_ref, ks_ref, vst_ref, kw_ref, vwt_ref, out_ref,
                        qsel_ref, sbuf_ref, m_ref, acc_ref, mw_ref, accw_ref, oc_ref,
                        *, n_sel_blocks, seq_len):
    qb = pl.program_id(2)
    qs = qb * Q_BLOCK
    hq = GQA * Q_BLOCK
    t_row = qs + lax.broadcasted_iota(jnp.int32, (1, Q_BLOCK), 1)
    t_all = jnp.concatenate([t_row] * GQA, axis=1)
    qt = qt_ref[...]
    qt_all = jnp.concatenate([qt[g * HEAD_DIM:(g + 1) * HEAD_DIM, :] for g in range(GQA)], axis=1)
    zeros_lo = jnp.zeros((HEAD_DIM, hq), BF16)
    qw = jnp.concatenate([qt_all, zeros_lo], axis=0)

    tile = min(KEY_TILE, seq_len)
    n_full = qs // tile
    lo_tiles = HEAD_DIM * SEL_BLOCK // tile
    wwidth = min(WINDOW, seq_len)
    causal = lambda kpos: kpos <= t_all
    band = lambda kpos: (kpos > t_all - WINDOW) & (kpos <= t_all)
    diag_w = lambda kpos: (kpos <= t_all) & (qs >= wwidth)

    def flash_steps(steps):
        staged = []
        for (q, k_ref, vt_ref, start, width, mask_fn, mref, aref, c) in steps:
            start = pl.multiple_of(start, LANES)
            s = jnp.dot(k_ref[pl.ds(start, width), :], q, preferred_element_type=F32)
            if mask_fn is not None:
                s = jnp.where(mask_fn(start + lax.broadcasted_iota(jnp.int32, (width, hq), 0)), s, NEG_FILL)
            staged.append((s, vt_ref[:, pl.ds(start, width)]))
        softmaxed = []
        for (s, vt), step in zip(staged, steps):
            mref, c = step[6], step[8]
            m_old = mref[c]
            m_new = jnp.maximum(m_old, jnp.max(s, axis=0, keepdims=True))
            softmaxed.append((jnp.exp2(s - m_new).astype(BF16), jnp.exp2(m_old - m_new), m_new, vt))
        for (p, alpha, m_new, vt), step in zip(softmaxed, steps):
            mref, aref, c = step[6], step[7], step[8]
            aref[c] = alpha * aref[c] + jnp.dot(vt, p, preferred_element_type=F32)
            mref[c] = m_new

    m_ref[...] = jnp.full(m_ref.shape, NEG_FILL, F32)
    acc_ref[...] = jnp.zeros(acc_ref.shape, F32)
    mw_ref[...] = jnp.full(mw_ref.shape, NEG_FILL, F32)
    accw_ref[...] = jnp.zeros(accw_ref.shape, F32)

    nc = kc_ref.shape[0]
    c_last = lax.broadcasted_iota(jnp.int32, (nc, hq), 0) * CMP_STRIDE + (CMP_BLOCK - 1)
    ok_c = c_last <= t_all
    s = jnp.dot(kc_ref[...], qt_all, preferred_element_type=F32)
    flash_steps([(qw, kw_ref, vwt_ref, jnp.maximum(qs - wwidth, 0), wwidth, band, mw_ref, accw_ref, 0),
                 (qw, kw_ref, vwt_ref, qs, Q_BLOCK, diag_w, mw_ref, accw_ref, 1)])
    s = jnp.where(ok_c, s, NEG_FILL)
    e = jnp.where(ok_c, jnp.exp2(s - jnp.max(s, axis=0, keepdims=True)), 0.0)
    r = jnp.dot(rhs_ref[...], e.astype(BF16), preferred_element_type=F32)
    inv = 1.0 / jnp.maximum(r[HEAD_DIM:HEAD_DIM + 1, :], 1e-30)
    oc_ref[...] = r[0:HEAD_DIM, :] * inv
    imp_all = r[LANES:2 * LANES, :] * inv
    imp = imp_all[:, 0:Q_BLOCK]
    for g in range(1, GQA):
        imp = imp + imp_all[:, g * Q_BLOCK:(g + 1) * Q_BLOCK]

    notsel = _not_selected(_masked_importance(imp, t_row, n_sel_blocks, 0), min(N_SEL, n_sel_blocks), 0)
    notsel = notsel.astype(BF16)
    q_lo = jnp.concatenate([qt_all, jnp.concatenate([notsel[0:HEAD_DIM, :]] * GQA, axis=1)], axis=0)
    q_hi = jnp.concatenate([qt_all, jnp.concatenate([notsel[HEAD_DIM:2 * HEAD_DIM, :]] * GQA, axis=1)], axis=0)
    qsel_ref[0] = q_lo
    qsel_ref[1] = q_hi

    def scores(b, kt, masked):
        start = pl.multiple_of(kt * tile, LANES)
        q = qsel_ref[jnp.where(kt >= lo_tiles, 1, 0)]
        s = jnp.dot(ks_ref[pl.ds(start, tile), :], q, preferred_element_type=F32)
        if masked:
            s = jnp.where(causal(start + lax.broadcasted_iota(jnp.int32, (tile, hq), 0)), s, NEG_FILL)
        sbuf_ref[b] = s

    def consume(b, kt):
        start = pl.multiple_of(kt * tile, LANES)
        s = sbuf_ref[b]
        m_old = m_ref[b]
        m_new = jnp.maximum(m_old, jnp.max(s, axis=0, keepdims=True))
        p = jnp.exp2(s - m_new).astype(BF16)
        acc_ref[b] = jnp.exp2(m_old - m_new) * acc_ref[b] + jnp.dot(vst_ref[:, pl.ds(start, tile)], p,
                                                                  preferred_element_type=F32)
        m_ref[b] = m_new

    @pl.when(n_full == 0)
    def _():
        scores(0, 0, True)

    @pl.when(n_full > 0)
    def _():
        scores(0, 0, False)

    def pair_body(j, carry):
        scores(1, 2 * j + 1, False)
        consume(0, 2 * j)
        scores(0, 2 * j + 2, False)
        consume(1, 2 * j + 1)
        return carry

    lax.fori_loop(0, jnp.maximum(n_full - 1, 0) // 2, pair_body, 0)

    @pl.when((n_full > 0) & (n_full % 2 == 0))
    def _():
        scores(1, n_full - 1, False)
        consume(0, n_full - 2)
        scores(0, n_full, True)
        consume(1, n_full - 1)
        consume(0, n_full)

    @pl.when(n_full % 2 == 1)
    def _():
        scores(1, n_full, True)
        consume(0, n_full - 1)
        consume(1, n_full)

    @pl.when(n_full == 0)
    def _():
        consume(0, 0)

    def merged(mref, aref, n_chain):
        m = mref[0]
        for c in range(1, n_chain):
            m = jnp.maximum(m, mref[c])
        a = jnp.exp2(mref[0] - m) * aref[0]
        for c in range(1, n_chain):
            a = a + jnp.exp2(mref[c] - m) * aref[c]
        return a[0:HEAD_DIM, :] * (1.0 / a[HEAD_DIM:HEAD_DIM + 1, :])

    gates = gatet_ref[...]
    sza = sza_ref[...]
    o_s = merged(m_ref, acc_ref, SEL_CHAINS)
    o_w = merged(mw_ref, accw_ref, 2)
    o_c = oc_ref[...]
    outs = []
    for g in range(GQA):
        cols = slice(g * Q_BLOCK, (g + 1) * Q_BLOCK)
        gc = gates[g * N_BRANCH + 0:g * N_BRANCH + 1, :]
        gs = gates[g * N_BRANCH + 1:g * N_BRANCH + 2, :]
        gw = gates[g * N_BRANCH + 2:g * N_BRANCH + 3, :]
        outs.append((gc * o_c[:, cols] + gs * o_s[:, cols] + gw * o_w[:, cols]).T)
    out_ref[...] = (jnp.concatenate(outs, axis=1) * sza).astype(BF16)


def _attn_prompt(qt, gates_t, sza3, kc, rhs_c, ks, vst, kw, vwt):
    n, seq_len, _ = sza3.shape
    nqb = seq_len // Q_BLOCK
    nchunk = kc.shape[2]
    hw = GQA * HEAD_DIM
    k_blk = pl.BlockSpec((None, seq_len, LANES), lambda b, k, i: (k, b, 0))
    vt_blk = pl.BlockSpec((None, V_ROWS, seq_len), lambda b, k, i: (k, 0, b))
    return pl.pallas_call(
        functools.partial(_attn_prompt_kernel, n_sel_blocks=-(-seq_len // SEL_BLOCK), seq_len=seq_len),
        grid=(n, N_KV_HEADS, nqb),
        in_specs=[
            pl.BlockSpec((hw, Q_BLOCK), lambda b, k, i: (k, b * nqb + i)),
            pl.BlockSpec((None, GATE_ROWS, Q_BLOCK), lambda b, k, i: (k, 0, b * nqb + i)),
            pl.BlockSpec((None, Q_BLOCK, hw), lambda b, k, i: (b, i, k)),
            pl.BlockSpec((None, None, nchunk, HEAD_DIM), lambda b, k, i: (b, k, 0, 0)),
            pl.BlockSpec((None, None, 2 * LANES, nchunk), lambda b, k, i: (b, k, 0, 0)),
            k_blk, vt_blk, k_blk, vt_blk,
        ],
        out_specs=pl.BlockSpec((None, Q_BLOCK, hw), lambda b, k, i: (b, i, k)),
        out_shape=jax.ShapeDtypeStruct((n, seq_len, D_ATTN), BF16),
        scratch_shapes=[
            pltpu.VMEM((2, 2 * HEAD_DIM, GQA * Q_BLOCK), BF16),
            pltpu.VMEM((SEL_CHAINS, min(KEY_TILE, seq_len), GQA * Q_BLOCK), F32),
            pltpu.VMEM((SEL_CHAINS, 1, GQA * Q_BLOCK), F32),
            pltpu.VMEM((SEL_CHAINS, V_ROWS, GQA * Q_BLOCK), F32),
            pltpu.VMEM((2, 1, GQA * Q_BLOCK), F32),
            pltpu.VMEM((2, V_ROWS, GQA * Q_BLOCK), F32),
            pltpu.VMEM((HEAD_DIM, GQA * Q_BLOCK), F32),
        ],
        compiler_params=_cparams(("arbitrary", "arbitrary", "arbitrary")),
        name="attn_prompt",
    )(qt, gates_t, sza3, kc, rhs_c, ks, vst, kw, vwt)


def _pool_out(d, pw_ref, ps_ref, szp):
    ys = [jnp.dot(d[:, g * POOL_GROUP_DIM:(g + 1) * POOL_GROUP_DIM].astype(BF16), pw_ref[g],
                  preferred_element_type=F32) for g in range(len(POOL_WINDOWS))]
    return jnp.concatenate(ys, axis=1) * ps_ref[...] * szp


def _finish(x, a_bf16, b, wo_ref, fg_ref):
    mix = jnp.concatenate([a_bf16, b.astype(BF16)], axis=1)
    y = x + jnp.dot(mix, wo_ref[...], preferred_element_type=F32)
    ms = jnp.mean(y * y, axis=-1, keepdims=True)
    return y * lax.rsqrt(ms + RMS_EPS) * fg_ref[...]


def _out_prompt_kernel(a_ref, u_ref, halo_ref, szp_ref, x_ref, pw_ref, ps_ref, wo_ref, fg_ref, y_ref, ext_ref,
                       *, tm, halo):
    i = pl.program_id(1)
    u = u_ref[...]
    ext_ref[0:halo, :] = jnp.where(i > 0, halo_ref[...], 0.0)
    ext_ref[halo:halo + tm, :] = u
    pos = i * tm + lax.broadcasted_iota(jnp.int32, (tm, POOL_GROUP_DIM), 0)
    ds = []
    for g, w in enumerate(POOL_WINDOWS):
        c0 = g * POOL_GROUP_DIM
        acc = u[:, c0:c0 + POOL_GROUP_DIM]
        for k in range(1, w):
            acc = acc + ext_ref[halo - k:halo - k + tm, c0:c0 + POOL_GROUP_DIM]
        cnt = jnp.minimum(pos + 1, w).astype(F32)
        ds.append(acc / cnt - u[:, c0:c0 + POOL_GROUP_DIM])
    b = _pool_out(jnp.concatenate(ds, axis=1), pw_ref, ps_ref, szp_ref[...])
    y_ref[...] = _finish(x_ref[...], a_ref[...], b, wo_ref, fg_ref)


def _out_prompt(a3, u3, szp3, x3, pool_w, pool_scale, w_out, final_g, *, tm):
    n, seq_len, d_model = x3.shape
    halo = 16
    nt = seq_len // tm
    blk = lambda w: pl.BlockSpec((None, tm, w), lambda b, i: (b, i, 0))
    const = lambda a: pl.BlockSpec(a.shape, lambda b, i: (0,) * a.ndim)
    return pl.pallas_call(
        functools.partial(_out_prompt_kernel, tm=tm, halo=halo),
        grid=(n, nt),
        in_specs=[blk(D_ATTN), blk(D_POOL),
                  pl.BlockSpec((None, halo, D_POOL), lambda b, i: (b, jnp.maximum(i * (tm // halo) - 1, 0), 0)),
                  blk(D_POOL), blk(d_model),
                  const(pool_w), const(pool_scale), const(w_out), const(final_g)],
        out_specs=blk(d_model),
        out_shape=jax.ShapeDtypeStruct((n, seq_len, d_model), F32),
        scratch_shapes=[pltpu.VMEM((tm + halo, D_POOL), F32)],
        compiler_params=_cparams(("arbitrary", "arbitrary")),
        name="out_prompt",
    )(a3, u3, u3, szp3, x3, pool_w, pool_scale, w_out, final_g)


def _out_sample_kernel(o_ref, sza_ref, u_ref, st_ref, szp_ref, x_ref, pw_ref, ps_ref, wo_ref, fg_ref, y_ref):
    u = u_ref[...]
    ds = []
    for g, w in enumerate(POOL_WINDOWS):
        c0 = g * POOL_GROUP_DIM
        acc = u[:, c0:c0 + POOL_GROUP_DIM]
        for k in range(1, w):
            acc = acc + st_ref[POOL_STATE - k, :, c0:c0 + POOL_GROUP_DIM]
        ds.append(acc / float(w) - u[:, c0:c0 + POOL_GROUP_DIM])
    b = _pool_out(jnp.concatenate(ds, axis=1), pw_ref, ps_ref, szp_ref[...])
    a = (o_ref[...] * sza_ref[...]).astype(BF16)
    y_ref[...] = _finish(x_ref[...], a, b, wo_ref, fg_ref)


def _out_sample(o2, sza, u, state_t, szp, x2, pool_w, pool_scale, w_out, final_g):
    args = (o2, sza, u, state_t, szp, x2, pool_w, pool_scale, w_out, final_g)
    full = lambda a: pl.BlockSpec(a.shape, lambda i: (0,) * a.ndim)
    return pl.pallas_call(
        _out_sample_kernel,
        grid=(1,),
        in_specs=[full(a) for a in args],
        out_specs=full(x2),
        out_shape=jax.ShapeDtypeStruct(x2.shape, F32),
        compiler_params=_cparams(("arbitrary",)),
        name="out_sample",
    )(*args)


def _cmp_sample_kernel(pt_ref, *refs, n_pages, page):
    del pt_ref
    pages = refs[:n_pages]
    qbd_ref, pe_ref, w1_ref, w2_ref, oc_ref, imp_ref, rows_ref = refs[n_pages:]
    nchunk = n_pages * page // CMP_STRIDE
    for p in range(n_pages):
        for kv in range(2):
            rows_ref[kv, p * page:(p + 1) * page, :] = pages[p][kv * LANES:(kv + 1) * LANES, :].T
    fulls = []
    for kv in range(2):
        load = lambda s: rows_ref[kv, pl.ds(s, nchunk, stride=CMP_STRIDE), :]
        h = _compress_hidden(load, pe_ref, w1_ref, kv, nchunk)
        fulls.append(jnp.dot(_silu(h).astype(BF16), w2_ref[kv], preferred_element_type=F32))
    k_c, v_c = fulls
    qbd = qbd_ref[...]
    s = lax.dot_general(qbd, k_c.astype(BF16), (((1,), (1,)), ((), ())), preferred_element_type=F32)
    ok = lax.broadcasted_iota(jnp.int32, s.shape, 1) < nchunk - 1
    s = jnp.where(ok, s, NEG_FILL)
    e = jnp.where(ok, jnp.exp2(s - jnp.max(s, axis=1, keepdims=True)), 0.0)
    pc = (e * (1.0 / jnp.sum(e, axis=1, keepdims=True))).astype(BF16)
    oc_ref[...] = jnp.dot(pc, v_c.astype(BF16), preferred_element_type=F32)
    imp_ref[...] = jnp.dot(pc, _cover_matrix(nchunk, LANES).astype(BF16), preferred_element_type=F32)


def _cmp_sample(page_table, cache_t, qbd, pe_t, w1bd, w2bd):
    nb, n_pages = page_table.shape
    page = cache_t.shape[-1]
    page_spec = lambda p: pl.BlockSpec((None, KV_W, page), lambda b, pt, p=p: (pt[b, p], 0, 0))
    const = lambda a: pl.BlockSpec(a.shape, lambda b, pt: (0,) * a.ndim)
    per_b = pl.BlockSpec((None, N_HEADS, LANES), lambda b, pt: (b, 0, 0))
    grid_spec = pltpu.PrefetchScalarGridSpec(
        num_scalar_prefetch=1,
        grid=(nb,),
        in_specs=[page_spec(p) for p in range(n_pages)] + [per_b, const(pe_t), const(w1bd), const(w2bd)],
        out_specs=(per_b, per_b),
        scratch_shapes=[pltpu.VMEM((2, n_pages * page, LANES), F32)],
    )
    return pl.pallas_call(
        functools.partial(_cmp_sample_kernel, n_pages=n_pages, page=page),
        grid_spec=grid_spec,
        out_shape=(jax.ShapeDtypeStruct((nb, N_HEADS, LANES), F32),
                   jax.ShapeDtypeStruct((nb, N_HEADS, LANES), F32)),
        compiler_params=_cparams(("arbitrary",)),
        name="cmp_sample",
    )(page_table, *([cache_t] * n_pages), qbd, pe_t, w1bd, w2bd)


def _topk_sample_kernel(imp_ref, out_ref, *, t_pos, n_blocks):
    imp = imp_ref[0] + imp_ref[1] + imp_ref[2] + imp_ref[3]
    t = jnp.full((imp.shape[0], 1), t_pos, jnp.int32)
    out_ref[...] = _not_selected(_masked_importance(imp, t, n_blocks, 1), min(N_SEL, n_blocks), 1)


def _topk_sample(imp_g, *, t_pos, n_blocks):
    rows = imp_g.shape[1]
    return pl.pallas_call(
        functools.partial(_topk_sample_kernel, t_pos=t_pos, n_blocks=n_blocks),
        grid=(1,),
        in_specs=[pl.BlockSpec(imp_g.shape, lambda i: (0, 0, 0))],
        out_specs=pl.BlockSpec((rows, LANES), lambda i: (0, 0)),
        out_shape=jax.ShapeDtypeStruct((rows, LANES), F32),
        compiler_params=_cparams(("arbitrary",)),
        name="topk_sample",
    )(imp_g)


def _attn_sample_kernel(pt_ref, *refs, n_pages, page):
    del pt_ref
    pages = refs[:n_pages]
    (win_ref, qbd_ref, ns_ref, snew_ref, wnew_ref, wcol_ref, oc_ref, gate_ref, o_ref, nwin_ref) = refs[n_pages:]
    qbd = qbd_ref[...]
    qf = qbd.astype(F32)
    ns = ns_ref[...]
    nt_ = (((1,), (1,)), ((), ()))

    def new_token(row_ref):
        k_new = row_ref[:, 0:LANES].astype(BF16).astype(F32)
        v_new = row_ref[:, LANES:2 * LANES].astype(BF16).astype(F32)
        return jnp.sum(qf * k_new, axis=1, keepdims=True), v_new

    lane = lax.broadcasted_iota(jnp.int32, (N_HEADS, page), 1)
    per_page = page // SEL_BLOCK
    scores = []
    for p in range(n_pages):
        s = jnp.dot(qbd, pages[p][0:LANES, :].astype(BF16), preferred_element_type=F32)
        flag = ns[:, p * per_page:p * per_page + 1]
        for j in range(1, per_page):
            flag = jnp.where(lane >= j * SEL_BLOCK, ns[:, p * per_page + j:p * per_page + j + 1], flag)
        scores.append(jnp.where(flag > 0.5, NEG_FILL, s))
    nb_cache = n_pages * per_page
    s_new, v_new = new_token(snew_ref)
    s_new = jnp.where(ns[:, nb_cache:nb_cache + 1] > 0.5, NEG_FILL, s_new)
    m = s_new
    for s in scores:
        m = jnp.maximum(m, jnp.max(s, axis=1, keepdims=True))
    e_new = jnp.exp2(s_new - m)
    l = e_new
    acc = e_new * v_new
    for p in range(n_pages):
        e = jnp.exp2(scores[p] - m).astype(BF16)
        l = l + jnp.sum(e.astype(F32), axis=1, keepdims=True)
        acc = acc + lax.dot_general(e, pages[p][LANES:2 * LANES, :].astype(BF16), nt_, preferred_element_type=F32)
    o_s = acc * (1.0 / l)

    win = win_ref[...]
    wlen = win.shape[1]
    s_w = jnp.dot(qbd, win[0:LANES, :].astype(BF16), preferred_element_type=F32)
    first = max(wlen - WINDOW + 1, 0)
    s_w = jnp.where(lax.broadcasted_iota(jnp.int32, s_w.shape, 1) >= first, s_w, NEG_FILL)
    sw_new, vw_new = new_token(wnew_ref)
    m_w = jnp.maximum(sw_new, jnp.max(s_w, axis=1, keepdims=True))
    ew_new = jnp.exp2(sw_new - m_w)
    e_w = jnp.exp2(s_w - m_w).astype(BF16)
    l_w = ew_new + jnp.sum(e_w.astype(F32), axis=1, keepdims=True)
    acc_w = ew_new * vw_new + lax.dot_general(e_w, win[LANES:2 * LANES, :].astype(BF16), nt_,
                                              preferred_element_type=F32)
    o_w = acc_w * (1.0 / l_w)

    gates = gate_ref[...]
    o_ref[...] = gates[:, 0:1] * oc_ref[...] + gates[:, 1:2] * o_s + gates[:, 2:3] * o_w

    shifted = pltpu.roll(win, wlen - 1, 1)
    last = lax.broadcasted_iota(jnp.int32, win.shape, 1) == wlen - 1
    nwin_ref[...] = jnp.where(last, wcol_ref[...], shifted)


def _attn_sample(page_table, cache_t, win_t, qbd, ns8, s_new, w_new, w_col, o_c, gates8):
    nb, n_pages = page_table.shape
    page = cache_t.shape[-1]
    wlen = win_t.shape[-1]
    page_spec = lambda p: pl.BlockSpec((None, KV_W, page), lambda b, pt, p=p: (pt[b, p], 0, 0))
    per_b = lambda a: pl.BlockSpec((None,) + a.shape[1:], lambda b, pt: (b,) + (0,) * (a.ndim - 1))
    grid_spec = pltpu.PrefetchScalarGridSpec(
        num_scalar_prefetch=1,
        grid=(nb,),
        in_specs=[page_spec(p) for p in range(n_pages)]
        + [per_b(a) for a in (win_t, qbd, ns8, s_new, w_new, w_col, o_c, gates8)],
        out_specs=(per_b(o_c), per_b(win_t)),
    )
    return pl.pallas_call(
        functools.partial(_attn_sample_kernel, n_pages=n_pages, page=page),
        grid_spec=grid_spec,
        out_shape=(jax.ShapeDtypeStruct(o_c.shape, F32), jax.ShapeDtypeStruct(win_t.shape, F32)),
        compiler_params=_cparams(("arbitrary",)),
        name="attn_sample",
    )(page_table, *([cache_t] * n_pages), win_t, qbd, ns8, s_new, w_new, w_col, o_c, gates8)


def _prep_w_in(w_in):
    o = D_ATTN + 3 * KV_W
    ng = GQA * N_BRANCH
    gl = w_in[:, o:o + N_KV_HEADS * ng]
    rest = w_in[:, o + N_KV_HEADS * ng:]
    pad = jnp.zeros((w_in.shape[0], LANES - ng), w_in.dtype)
    return jnp.concatenate([w_in[:, :o], rest, gl[:, :ng], pad, gl[:, ng:], pad], axis=1).astype(BF16)


def _prep_compress(cmp_pe, cmp_w1, cmp_w2):
    r = CMP_BLOCK // CMP_STRIDE
    pe_t = jnp.tile(cmp_pe.reshape(2, r, CMP_STRIDE, 1, HEAD_DIM), (1, 1, 1, 1, N_KV_HEADS))
    w1 = cmp_w1.reshape(2, r, CMP_STRIDE, HEAD_DIM, CMP_HIDDEN)
    z1 = jnp.zeros_like(w1)
    top = jnp.concatenate([w1, z1], axis=-1)
    bot = jnp.concatenate([z1, w1], axis=-1)
    w1bd = jnp.concatenate([top, bot], axis=3)
    w1bd = w1bd.reshape(2, r, CMP_STRIDE * LANES, N_KV_HEADS * CMP_HIDDEN).astype(BF16)
    z2 = jnp.zeros_like(cmp_w2)
    w2bd = jnp.concatenate([jnp.concatenate([cmp_w2, z2], axis=-1),
                            jnp.concatenate([z2, cmp_w2], axis=-1)], axis=1).astype(BF16)
    return pe_t, w1bd, w2bd


def _kv_out(kv_t):
    n, _, rows = kv_t.shape
    return jnp.transpose(kv_t.reshape(n, 2, N_KV_HEADS, HEAD_DIM, rows), (0, 4, 1, 2, 3))[None]


def kernel(x_prompt, x_sample, cache_cmp_kv, cache_slc_kv, cache_win_kv, state_pool, page_table, norm_g, w_in,
           cmp_pe, cmp_w1, cmp_w2, pool_w, pool_scale, w_out, final_g):
    n, seq_len, d_model = x_prompt.shape
    nb = x_sample.shape[0]
    n_phys, page = cache_cmp_kv.shape[1], cache_cmp_kv.shape[2]
    n_pages = page_table.shape[1]
    past_len = n_pages * page
    wlen = cache_win_kv.shape[2]

    w_r = _prep_w_in(w_in[0])
    pe_t, w1bd, w2bd = _prep_compress(cmp_pe[0], cmp_w1[0], cmp_w2[0])
    pool_w_b = pool_w[0].astype(BF16)
    w_out_b = w_out[0].astype(BF16)
    fg = final_g.reshape(1, d_model)

    (qt, crm, ckv_t, skv_t, wkv_t, ks, vst, kw, vwt, gates_t, sza, u, szp) = _project(
        x_prompt.reshape(n * seq_len, d_model), norm_g, w_r, tm=512, seq_len=seq_len)
    r3 = lambda a: a.reshape(n, seq_len, a.shape[-1])
    kc, rhs_c = _compress_prompt(crm, n, pe_t, w1bd, w2bd)
    a3 = _attn_prompt(qt, gates_t, r3(sza), kc, rhs_c, ks, vst, kw, vwt)
    y_prompt = _out_prompt(a3, r3(u), r3(szp), x_prompt, pool_w_b, pool_scale, w_out_b, fg, tm=512)

    new_cmp_p = _kv_out(ckv_t)
    new_slc_p = _kv_out(skv_t)
    new_win_p = _kv_out(wkv_t[:, :, seq_len - min(WINDOW, seq_len):])
    new_pool_p = r3(u)[:, seq_len - POOL_STATE:][None]

    (qt_s, _, ckv_ts, skv_ts, wkv_ts, _, _, _, _, gates_ts, sza_s, u_s, szp_s) = _project(
        x_sample.reshape(nb, d_model), norm_g, w_r, tm=nb, seq_len=nb)
    q_s = qt_s.T
    q5 = q_s.reshape(nb, N_KV_HEADS, GQA, 1, HEAD_DIM)
    eye = jnp.eye(N_KV_HEADS, dtype=q_s.dtype).reshape(1, N_KV_HEADS, 1, N_KV_HEADS, 1)
    qbd = (q5 * eye).reshape(nb, N_HEADS, LANES)

    to_pages = lambda c: jnp.transpose(c[0], (0, 2, 3, 4, 1)).reshape(n_phys, KV_W, page)
    o_c, imp8 = _cmp_sample(page_table, to_pages(cache_cmp_kv), qbd, pe_t, w1bd, w2bd)
    imp_g = jnp.transpose(imp8.reshape(nb * N_KV_HEADS, GQA, LANES), (1, 0, 2))
    n_blocks = -(-(past_len + 1) // SEL_BLOCK)
    notsel = _topk_sample(imp_g, t_pos=past_len, n_blocks=n_blocks)
    ns8 = jnp.repeat(notsel.reshape(nb, N_KV_HEADS, 1, LANES), GQA, axis=2).reshape(nb, N_HEADS, LANES)

    gates8 = jnp.transpose(gates_ts[:, :GQA * N_BRANCH, :], (2, 0, 1)).reshape(nb, N_HEADS, N_BRANCH)
    gates8 = jnp.pad(gates8, ((0, 0), (0, 0), (0, LANES - N_BRANCH)))
    win_t = jnp.transpose(cache_win_kv[0], (0, 2, 3, 4, 1)).reshape(nb, KV_W, wlen)
    s_new = skv_ts[0].T.reshape(nb, 1, KV_W)
    w_new = wkv_ts[0].T.reshape(nb, 1, KV_W)
    w_col = w_new.reshape(nb, KV_W, 1)
    o8, nwin_t = _attn_sample(page_table, to_pages(cache_slc_kv), win_t, qbd, ns8, s_new, w_new, w_col, o_c, gates8)
    o5 = o8.reshape(nb, N_KV_HEADS, GQA, N_KV_HEADS, HEAD_DIM)
    o2 = jnp.stack([o5[:, k, :, k, :] for k in range(N_KV_HEADS)], axis=1).reshape(nb, D_ATTN)

    state_t = jnp.transpose(state_pool[0], (1, 0, 2))
    y_sample = _out_sample(o2, sza_s, u_s, state_t, szp_s, x_sample.reshape(nb, d_model),
                           pool_w_b, pool_scale, w_out_b, fg).reshape(nb, 1, d_model)

    kv_out_s = lambda t: jnp.transpose(t.reshape(2, N_KV_HEADS, HEAD_DIM, nb, 1), (3, 4, 0, 1, 2))[None]
    new_cmp_s = kv_out_s(ckv_ts)
    new_slc_s = kv_out_s(skv_ts)
    new_win_s = jnp.transpose(nwin_t.reshape(nb, 2, N_KV_HEADS, HEAD_DIM, wlen), (0, 4, 1, 2, 3))[None]
    new_pool_s = jnp.transpose(jnp.concatenate([state_t[1:], u_s[None]], axis=0), (1, 0, 2))[None]

    return (y_prompt, y_sample, new_cmp_p, new_slc_p, new_win_p, new_pool_p,
            new_cmp_s, new_slc_s, new_win_s, new_pool_s)
```

```python
import functools

import jax
import jax.numpy as jnp
from jax import lax
from jax.experimental import pallas as pl
from jax.experimental.pallas import tpu as pltpu

F32 = jnp.float32
BF16 = jnp.bfloat16

HEAD_DIM = 64
N_KV_HEADS = 2
GQA = 4
N_HEADS = N_KV_HEADS * GQA
D_ATTN = N_HEADS * HEAD_DIM
KV_W = 2 * N_KV_HEADS * HEAD_DIM
N_BRANCH = 3
D_POOL = 512
CMP_BLOCK = 32
CMP_STRIDE = 16
CMP_HIDDEN = 128
SEL_BLOCK = 64
N_SEL = 16
WINDOW = 512
Q_BLOCK = 256
POOL_WINDOWS = (2, 4, 8, 16)
POOL_GROUP_DIM = 128
POOL_STATE = 15
RMS_EPS = 1e-6
LANES = 128
NEG_BIAS = -(2.0 ** 30)
NEG_FILL = -1e30
GATE_COLS = 2 * LANES
GATE_ROWS = 16
BF16_SUBLANES = 16
V_ROWS = HEAD_DIM + BF16_SUBLANES
KEY_TILE = 512
FORCED_VALUE = 1e9
N_FORCED = 3
SAMPLE_GROUP = 4
SEL_CHAINS = 2
LOG2E = 1.4426950408889634
P_PAD = D_ATTN + 3 * KV_W + 3 * 512 + GATE_COLS
VMEM_LIMIT = 48 * 1024 * 1024


def _cparams(sem):
    return pltpu.CompilerParams(dimension_semantics=sem, vmem_limit_bytes=VMEM_LIMIT)


def _silu(z):
    return z * jax.nn.sigmoid(z)


def _proj_kernel(x_ref, g_ref, w_ref, qt_ref, crm_ref, ct_ref, st_ref, wt_ref, ks_ref, vst_ref, kw_ref, vwt_ref,
                 gatet_ref, sza_ref, u_ref, szp_ref, *, tm, seq_len):
    i = pl.program_id(0)
    x = x_ref[...]
    ms = jnp.mean(x * x, axis=-1, keepdims=True)
    xn = (x * lax.rsqrt(ms + RMS_EPS) * g_ref[...]).astype(BF16)

    def mm(c0, c1):
        return jnp.dot(xn, w_ref[:, c0:c1], preferred_element_type=F32)

    qt_ref[...] = (mm(0, D_ATTN) * (HEAD_DIM ** -0.5 * LOG2E)).T.astype(BF16)
    kv = mm(D_ATTN, D_ATTN + 3 * KV_W)
    crm_ref[0] = kv[:, 0:LANES]
    crm_ref[1] = kv[:, LANES:2 * LANES]
    kvt = kv.T
    ct_ref[...] = kvt[0:KV_W, :]
    st_ref[...] = kvt[KV_W:2 * KV_W, :]
    wt_ref[...] = kvt[2 * KV_W:3 * KV_W, :]

    lane = lax.broadcasted_iota(jnp.int32, (tm, LANES), 1)
    pos = (i * tm) % seq_len + lax.broadcasted_iota(jnp.int32, (tm, LANES), 0)
    onehot = jnp.where(lane - HEAD_DIM == (pos // SEL_BLOCK) % HEAD_DIM, NEG_BIAS, 0.0).astype(F32)
    ones_rows = jnp.where(lax.broadcasted_iota(jnp.int32, (V_ROWS - HEAD_DIM, tm), 0) == 0, 1.0, 0.0).astype(BF16)
    for kvh in range(N_KV_HEADS):
        for (base, k_dst, v_dst, k_pad) in ((KV_W, ks_ref, vst_ref, onehot), (2 * KV_W, kw_ref, vwt_ref, 0.0)):
            slab = kv[:, base:base + LANES]
            if kvh == 1:
                slab = pltpu.roll(slab, HEAD_DIM, 1)
            k_dst[kvh, :, :] = jnp.where(lane < HEAD_DIM, slab, k_pad).astype(BF16)
            v0 = base + LANES + kvh * HEAD_DIM
            v_dst[kvh, 0:HEAD_DIM, :] = kvt[v0:v0 + HEAD_DIM, :].astype(BF16)
            v_dst[kvh, HEAD_DIM:V_ROWS, :] = ones_rows

    c = D_ATTN + 3 * KV_W
    sza_ref[...] = _silu(mm(c, c + 512)).astype(BF16)
    u_ref[...] = mm(c + 512, c + 1024)
    szp_ref[...] = _silu(mm(c + 1024, c + 1536)).astype(BF16)
    gate_t = jax.nn.sigmoid(mm(c + 1536, c + 1536 + GATE_COLS)).T
    for kvh in range(N_KV_HEADS):
        gatet_ref[kvh, :, :] = gate_t[kvh * LANES:kvh * LANES + GATE_ROWS, :]


def _project(x2d, norm_g, w_r, *, tm, seq_len):
    rows = x2d.shape[0]
    nt = rows // tm
    tps = seq_len // tm
    row_blk = lambda w: pl.BlockSpec((tm, w), lambda i: (i, 0))
    kv_t = jax.ShapeDtypeStruct((rows // seq_len, KV_W, seq_len), F32)
    kv_t_blk = pl.BlockSpec((None, KV_W, tm), lambda i: (i // tps, 0, i % tps))
    k_rm = jax.ShapeDtypeStruct((N_KV_HEADS, rows, LANES), BF16)
    k_rm_blk = pl.BlockSpec((N_KV_HEADS, tm, LANES), lambda i: (0, i, 0))
    v_t = jax.ShapeDtypeStruct((N_KV_HEADS, V_ROWS, rows), BF16)
    v_t_blk = pl.BlockSpec((N_KV_HEADS, V_ROWS, tm), lambda i: (0, 0, i))
    out_shape = (
        jax.ShapeDtypeStruct((D_ATTN, rows), BF16),
        jax.ShapeDtypeStruct((2, rows, LANES), F32),
        kv_t, kv_t, kv_t,
        k_rm,
        v_t,
        k_rm,
        v_t,
        jax.ShapeDtypeStruct((N_KV_HEADS, GATE_ROWS, rows), F32),
        jax.ShapeDtypeStruct((rows, 512), BF16),
        jax.ShapeDtypeStruct((rows, 512), F32),
        jax.ShapeDtypeStruct((rows, 512), BF16),
    )
    out_specs = (
        pl.BlockSpec((D_ATTN, tm), lambda i: (0, i)),
        pl.BlockSpec((2, tm, LANES), lambda i: (0, i, 0)),
        kv_t_blk, kv_t_blk, kv_t_blk,
        k_rm_blk, v_t_blk, k_rm_blk, v_t_blk,
        pl.BlockSpec((N_KV_HEADS, GATE_ROWS, tm), lambda i: (0, 0, i)),
        row_blk(512), row_blk(512), row_blk(512),
    )
    return pl.pallas_call(
        functools.partial(_proj_kernel, tm=tm, seq_len=seq_len),
        grid=(nt,),
        in_specs=[row_blk(x2d.shape[1]),
                  pl.BlockSpec((1, x2d.shape[1]), lambda i: (0, 0)),
                  pl.BlockSpec(w_r.shape, lambda i: (0, 0))],
        out_specs=out_specs,
        out_shape=out_shape,
        compiler_params=_cparams(("arbitrary",)),
        name="proj",
    )(x2d, norm_g, w_r)


def _compress_hidden(load_rows, pe_ref, w1_ref, kv, m):
    xs = [load_rows(s) for s in range(CMP_STRIDE)]
    hs = []
    for sub in range(CMP_BLOCK // CMP_STRIDE):
        lhs = jnp.concatenate([(xs[s] + pe_ref[kv, sub, s]).astype(BF16) for s in range(CMP_STRIDE)], axis=1)
        hs.append(jnp.dot(lhs, w1_ref[kv, sub], preferred_element_type=F32))
    return hs[0] + pltpu.roll(hs[1], m - 1, 0)


def _cover_matrix(nc_pad, ns_pad):
    c0 = lax.broadcasted_iota(jnp.int32, (nc_pad, ns_pad), 0) * CMP_STRIDE
    s0 = lax.broadcasted_iota(jnp.int32, (nc_pad, ns_pad), 1) * SEL_BLOCK
    return jnp.where((c0 < s0 + SEL_BLOCK) & (c0 + CMP_BLOCK > s0), 1.0, 0.0).astype(F32)


def _compress_prompt_kernel(c_ref, pe_ref, w1_ref, w2_ref, kc_ref, rhs_ref, *, nchunk):
    ones_rows = jnp.where(lax.broadcasted_iota(jnp.int32, (HEAD_DIM, nchunk), 0) == 0, 1.0, 0.0).astype(BF16)
    s0 = lax.broadcasted_iota(jnp.int32, (LANES, nchunk), 0) * SEL_BLOCK
    c0 = lax.broadcasted_iota(jnp.int32, (LANES, nchunk), 1) * CMP_STRIDE
    cover_t = jnp.where((c0 < s0 + SEL_BLOCK) & (c0 + CMP_BLOCK > s0), 1.0, 0.0).astype(BF16)
    for kv in range(2):
        load = lambda s: c_ref[kv, pl.ds(s, nchunk, stride=CMP_STRIDE), :]
        h = _compress_hidden(load, pe_ref, w1_ref, kv, nchunk)
        full = jnp.dot(_silu(h).astype(BF16), w2_ref[kv], preferred_element_type=F32)
        if kv == 0:
            kc_ref[0, :, :] = full[:, 0:HEAD_DIM].astype(BF16)
            kc_ref[1, :, :] = full[:, HEAD_DIM:2 * HEAD_DIM].astype(BF16)
        else:
            full_t = full.T
            for kvh in range(N_KV_HEADS):
                rhs_ref[kvh, 0:HEAD_DIM, :] = full_t[kvh * HEAD_DIM:(kvh + 1) * HEAD_DIM, :].astype(BF16)
                rhs_ref[kvh, HEAD_DIM:LANES, :] = ones_rows
                rhs_ref[kvh, LANES:2 * LANES, :] = cover_t


def _compress_prompt(crm, n, pe_t, w1bd, w2bd):
    seq_len = crm.shape[1] // n
    nchunk = seq_len // CMP_STRIDE
    return pl.pallas_call(
        functools.partial(_compress_prompt_kernel, nchunk=nchunk),
        grid=(n,),
        in_specs=[pl.BlockSpec((2, seq_len, LANES), lambda b: (0, b, 0)),
                  pl.BlockSpec(pe_t.shape, lambda b: (0, 0, 0, 0, 0)),
                  pl.BlockSpec(w1bd.shape, lambda b: (0, 0, 0, 0)),
                  pl.BlockSpec(w2bd.shape, lambda b: (0, 0, 0))],
        out_specs=(pl.BlockSpec((None, N_KV_HEADS, nchunk, HEAD_DIM), lambda b: (b, 0, 0, 0)),
                   pl.BlockSpec((None, N_KV_HEADS, 2 * LANES, nchunk), lambda b: (b, 0, 0, 0))),
        out_shape=(jax.ShapeDtypeStruct((n, N_KV_HEADS, nchunk, HEAD_DIM), BF16),
                   jax.ShapeDtypeStruct((n, N_KV_HEADS, 2 * LANES, nchunk), BF16)),
        compiler_params=_cparams(("arbitrary",)),
        name="compress_prompt",
    )(crm, pe_t, w1bd, w2bd)


def _not_selected(val, n_top, axis):
    blk = lax.broadcasted_iota(jnp.int32, val.shape, axis).astype(F32)
    forced = val >= FORCED_VALUE
    notsel = jnp.where(forced, 0.0, 1.0).astype(F32)
    val = jnp.where(forced, -3e38, val)
    for _ in range(max(n_top - N_FORCED, 0)):
        m = jnp.max(val, axis=axis, keepdims=True)
        idx = jnp.min(jnp.where(val == m, blk, float(LANES)), axis=axis, keepdims=True)
        pick = blk == idx
        notsel = jnp.where(pick, 0.0, notsel)
        val = jnp.where(pick, -3e38, val)
    return notsel


def _masked_importance(imp, t, n_blocks, axis):
    blk = lax.broadcasted_iota(jnp.int32, imp.shape, axis)
    cur = t // SEL_BLOCK
    forced = (blk == 0) | (blk == cur) | (blk == cur - 1)
    val = jnp.where(forced, FORCED_VALUE, jnp.where(blk * SEL_BLOCK <= t, imp, -1e9))
    return jnp.where(blk < n_blocks, val, -2e9)


def _attn_prompt_kernel(qt_ref, gatet_ref, sza_ref, kc_ref, rhs_ref, ks_ref, vst_ref, kw_ref, vwt_ref, out_ref,
                        qsel_ref, sbuf_ref, m_ref, acc_ref, mw_ref, accw_ref, oc_ref,
                        *, n_sel_blocks, seq_len):
    qb = pl.program_id(2)
    qs = qb * Q_BLOCK
    hq = GQA * Q_BLOCK
    t_row = qs + lax.broadcasted_iota(jnp.int32, (1, Q_BLOCK), 1)
    t_all = jnp.concatenate([t_row] * GQA, axis=1)
    qt = qt_ref[...]
    qt_all = jnp.concatenate([qt[g * HEAD_DIM:(g + 1) * HEAD_DIM, :] for g in range(GQA)], axis=1)
    zeros_lo = jnp.zeros((HEAD_DIM, hq), BF16)
    qw = jnp.concatenate([qt_all, zeros_lo], axis=0)

    tile = min(KEY_TILE, seq_len)
    n_full = qs // tile
    lo_tiles = HEAD_DIM * SEL_BLOCK // tile
    wwidth = min(WINDOW, seq_len)
    causal = lambda kpos: kpos <= t_all
    band = lambda kpos: (kpos > t_all - WINDOW) & (kpos <= t_all)
    diag_w = lambda kpos: (kpos <= t_all) & (qs >= wwidth)

    def flash_steps(steps):
        staged = []
        for (q, k_ref, vt_ref, start, width, mask_fn, mref, aref, c) in steps:
            start = pl.multiple_of(start, LANES)
            s = jnp.dot(k_ref[pl.ds(start, width), :], q, preferred_element_type=F32)
            if mask_fn is not None:
                s = jnp.where(mask_fn(start + lax.broadcasted_iota(jnp.int32, (width, hq), 0)), s, NEG_FILL)
            staged.append((s, vt_ref[:, pl.ds(start, width)]))
        softmaxed = []
        for (s, vt), step in zip(staged, steps):
            mref, c = step[6], step[8]
            m_old = mref[c]
            m_new = jnp.maximum(m_old, jnp.max(s, axis=0, keepdims=True))
            softmaxed.append((jnp.exp2(s - m_new).astype(BF16), jnp.exp2(m_old - m_new), m_new, vt))
        for (p, alpha, m_new, vt), step in zip(softmaxed, steps):
            mref, aref, c = step[6], step[7], step[8]
            aref[c] = alpha * aref[c] + jnp.dot(vt, p, preferred_element_type=F32)
            mref[c] = m_new

    m_ref[...] = jnp.full(m_ref.shape, NEG_FILL, F32)
    acc_ref[...] = jnp.zeros(acc_ref.shape, F32)
    mw_ref[...] = jnp.full(mw_ref.shape, NEG_FILL, F32)
    accw_ref[...] = jnp.zeros(accw_ref.shape, F32)

    nc = kc_ref.shape[0]
    c_last = lax.broadcasted_iota(jnp.int32, (nc, hq), 0) * CMP_STRIDE + (CMP_BLOCK - 1)
    ok_c = c_last <= t_all
    s = jnp.dot(kc_ref[...], qt_all, preferred_element_type=F32)
    flash_steps([(qw, kw_ref, vwt_ref, jnp.maximum(qs - wwidth, 0), wwidth, band, mw_ref, accw_ref, 0),
                 (qw, kw_ref, vwt_ref, qs, Q_BLOCK, diag_w, mw_ref, accw_ref, 1)])
    s = jnp.where(ok_c, s, NEG_FILL)
    e = jnp.where(ok_c, jnp.exp2(s - jnp.max(s, axis=0, keepdims=True)), 0.0)
    r = jnp.dot(rhs_ref[...], e.astype(BF16), preferred_element_type=F32)
    inv = 1.0 / jnp.maximum(r[HEAD_DIM:HEAD_DIM + 1, :], 1e-30)
    oc_ref[...] = r[0:HEAD_DIM, :] * inv
    imp_all = r[LANES:2 * LANES, :] * inv
    imp = imp_all[:, 0:Q_BLOCK]
    for g in range(1, GQA):
        imp = imp + imp_all[:, g * Q_BLOCK:(g + 1) * Q_BLOCK]

    notsel = _not_selected(_masked_importance(imp, t_row, n_sel_blocks, 0), min(N_SEL, n_sel_blocks), 0)
    notsel = notsel.astype(BF16)
    q_lo = jnp.concatenate([qt_all, jnp.concatenate([notsel[0:HEAD_DIM, :]] * GQA, axis=1)], axis=0)
    q_hi = jnp.concatenate([qt_all, jnp.concatenate([notsel[HEAD_DIM:2 * HEAD_DIM, :]] * GQA, axis=1)], axis=0)
    qsel_ref[0] = q_lo
    qsel_ref[1] = q_hi

    def scores(b, kt, masked):
        start = pl.multiple_of(kt * tile, LANES)
        q = qsel_ref[jnp.where(kt >= lo_tiles, 1, 0)]
        s = jnp.dot(ks_ref[pl.ds(start, tile), :], q, preferred_element_type=F32)
        if masked:
            s = jnp.where(causal(start + lax.broadcasted_iota(jnp.int32, (tile, hq), 0)), s, NEG_FILL)
        sbuf_ref[b] = s

    def consume(b, kt):
        start = pl.multiple_of(kt * tile, LANES)
        s = sbuf_ref[b]
        m_old = m_ref[b]
        m_new = jnp.maximum(m_old, jnp.max(s, axis=0, keepdims=True))
        p = jnp.exp2(s - m_new).astype(BF16)
        acc_ref[b] = jnp.exp2(m_old - m_new) * acc_ref[b] + jnp.dot(vst_ref[:, pl.ds(start, tile)], p,
                                                                  preferred_element_type=F32)
        m_ref[b] = m_new

    @pl.when(n_full == 0)
    def _():
        scores(0, 0, True)

    @pl.when(n_full > 0)
    def _():
        scores(0, 0, False)

    def pair_body(j, carry):
        scores(1, 2 * j + 1, False)
        consume(0, 2 * j)
        scores(0, 2 * j + 2, False)
        consume(1, 2 * j + 1)
        return carry

    lax.fori_loop(0, jnp.maximum(n_full - 1, 0) // 2, pair_body, 0)

    @pl.when((n_full > 0) & (n_full % 2 == 0))
    def _():
        scores(1, n_full - 1, False)
        consume(0, n_full - 2)
        scores(0, n_full, True)
        consume(1, n_full - 1)
        consume(0, n_full)

    @pl.when(n_full % 2 == 1)
    def _():
        scores(1, n_full, True)
        consume(0, n_full - 1)
        consume(1, n_full)

    @pl.when(n_full == 0)
    def _():
        consume(0, 0)

    def merged(mref, aref, n_chain):
        m = mref[0]
        for c in range(1, n_chain):
            m = jnp.maximum(m, mref[c])
        a = jnp.exp2(mref[0] - m) * aref[0]
        for c in range(1, n_chain):
            a = a + jnp.exp2(mref[c] - m) * aref[c]
        return a[0:HEAD_DIM, :] * (1.0 / a[HEAD_DIM:HEAD_DIM + 1, :])

    gates = gatet_ref[...]
    sza = sza_ref[...]
    o_s = merged(m_ref, acc_ref, SEL_CHAINS)
    o_w = merged(mw_ref, accw_ref, 2)
    o_c = oc_ref[...]
    outs = []
    for g in range(GQA):
        cols = slice(g * Q_BLOCK, (g + 1) * Q_BLOCK)
        gc = gates[g * N_BRANCH + 0:g * N_BRANCH + 1, :]
        gs = gates[g * N_BRANCH + 1:g * N_BRANCH + 2, :]
        gw = gates[g * N_BRANCH + 2:g * N_BRANCH + 3, :]
        outs.append((gc * o_c[:, cols] + gs * o_s[:, cols] + gw * o_w[:, cols]).T)
    out_ref[...] = (jnp.concatenate(outs, axis=1) * sza).astype(BF16)


def _attn_prompt(qt, gates_t, sza3, kc, rhs_c, ks, vst, kw, vwt):
    n, seq_len, _ = sza3.shape
    nqb = seq_len // Q_BLOCK
    nchunk = kc.shape[2]
    hw = GQA * HEAD_DIM
    k_blk = pl.BlockSpec((None, seq_len, LANES), lambda b, k, i: (k, b, 0))
    vt_blk = pl.BlockSpec((None, V_ROWS, seq_len), lambda b, k, i: (k, 0, b))
    return pl.pallas_call(
        functools.partial(_attn_prompt_kernel, n_sel_blocks=-(-seq_len // SEL_BLOCK), seq_len=seq_len),
        grid=(n, N_KV_HEADS, nqb),
        in_specs=[
            pl.BlockSpec((hw, Q_BLOCK), lambda b, k, i: (k, b * nqb + i)),
            pl.BlockSpec((None, GATE_ROWS, Q_BLOCK), lambda b, k, i: (k, 0, b * nqb + i)),
            pl.BlockSpec((None, Q_BLOCK, hw), lambda b, k, i: (b, i, k)),
            pl.BlockSpec((None, None, nchunk, HEAD_DIM), lambda b, k, i: (b, k, 0, 0)),
            pl.BlockSpec((None, None, 2 * LANES, nchunk), lambda b, k, i: (b, k, 0, 0)),
            k_blk, vt_blk, k_blk, vt_blk,
        ],
        out_specs=pl.BlockSpec((None, Q_BLOCK, hw), lambda b, k, i: (b, i, k)),
        out_shape=jax.ShapeDtypeStruct((n, seq_len, D_ATTN), BF16),
        scratch_shapes=[
            pltpu.VMEM((2, 2 * HEAD_DIM, GQA * Q_BLOCK), BF16),
            pltpu.VMEM((SEL_CHAINS, min(KEY_TILE, seq_len), GQA * Q_BLOCK), F32),
            pltpu.VMEM((SEL_CHAINS, 1, GQA * Q_BLOCK), F32),
            pltpu.VMEM((SEL_CHAINS, V_ROWS, GQA * Q_BLOCK), F32),
            pltpu.VMEM((2, 1, GQA * Q_BLOCK), F32),
            pltpu.VMEM((2, V_ROWS, GQA * Q_BLOCK), F32),
            pltpu.VMEM((HEAD_DIM, GQA * Q_BLOCK), F32),
        ],
        compiler_params=_cparams(("arbitrary", "arbitrary", "arbitrary")),
        name="attn_prompt",
    )(qt, gates_t, sza3, kc, rhs_c, ks, vst, kw, vwt)


def _pool_out(d, pw_ref, ps_ref, szp):
    ys = [jnp.dot(d[:, g * POOL_GROUP_DIM:(g + 1) * POOL_GROUP_DIM].astype(BF16), pw_ref[g],
                  preferred_element_type=F32) for g in range(len(POOL_WINDOWS))]
    return jnp.concatenate(ys, axis=1) * ps_ref[...] * szp


def _finish(x, a_bf16, b, wo_ref, fg_ref):
    mix = jnp.concatenate([a_bf16, b.astype(BF16)], axis=1)
    y = x + jnp.dot(mix, wo_ref[...], preferred_element_type=F32)
    ms = jnp.mean(y * y, axis=-1, keepdims=True)
    return y * lax.rsqrt(ms + RMS_EPS) * fg_ref[...]


def _out_prompt_kernel(a_ref, u_ref, halo_ref, szp_ref, x_ref, pw_ref, ps_ref, wo_ref, fg_ref, y_ref, ext_ref,
                       *, tm, halo):
    i = pl.program_id(1)
    u = u_ref[...]
    ext_ref[0:halo, :] = jnp.where(i > 0, halo_ref[...], 0.0)
    ext_ref[halo:halo + tm, :] = u
    pos = i * tm + lax.broadcasted_iota(jnp.int32, (tm, POOL_GROUP_DIM), 0)
    ds = []
    for g, w in enumerate(POOL_WINDOWS):
        c0 = g * POOL_GROUP_DIM
        acc = u[:, c0:c0 + POOL_GROUP_DIM]
        for k in range(1, w):
            acc = acc + ext_ref[halo - k:halo - k + tm, c0:c0 + POOL_GROUP_DIM]
        cnt = jnp.minimum(pos + 1, w).astype(F32)
        ds.append(acc / cnt - u[:, c0:c0 + POOL_GROUP_DIM])
    b = _pool_out(jnp.concatenate(ds, axis=1), pw_ref, ps_ref, szp_ref[...])
    y_ref[...] = _finish(x_ref[...], a_ref[...], b, wo_ref, fg_ref)


def _out_prompt(a3, u3, szp3, x3, pool_w, pool_scale, w_out, final_g, *, tm):
    n, seq_len, d_model = x3.shape
    halo = 16
    nt = seq_len // tm
    blk = lambda w: pl.BlockSpec((None, tm, w), lambda b, i: (b, i, 0))
    const = lambda a: pl.BlockSpec(a.shape, lambda b, i: (0,) * a.ndim)
    return pl.pallas_call(
        functools.partial(_out_prompt_kernel, tm=tm, halo=halo),
        grid=(n, nt),
        in_specs=[blk(D_ATTN), blk(D_POOL),
                  pl.BlockSpec((None, halo, D_POOL), lambda b, i: (b, jnp.maximum(i * (tm // halo) - 1, 0), 0)),
                  blk(D_POOL), blk(d_model),
                  const(pool_w), const(pool_scale), const(w_out), const(final_g)],
        out_specs=blk(d_model),
        out_shape=jax.ShapeDtypeStruct((n, seq_len, d_model), F32),
        scratch_shapes=[pltpu.VMEM((tm + halo, D_POOL), F32)],
        compiler_params=_cparams(("arbitrary", "arbitrary")),
        name="out_prompt",
    )(a3, u3, u3, szp3, x3, pool_w, pool_scale, w_out, final_g)


def _out_sample_kernel(o_ref, sza_ref, u_ref, st_ref, szp_ref, x_ref, pw_ref, ps_ref, wo_ref, fg_ref, y_ref):
    u = u_ref[...]
    ds = []
    for g, w in enumerate(POOL_WINDOWS):
        c0 = g * POOL_GROUP_DIM
        acc = u[:, c0:c0 + POOL_GROUP_DIM]
        for k in range(1, w):
            acc = acc + st_ref[POOL_STATE - k, :, c0:c0 + POOL_GROUP_DIM]
        ds.append(acc / float(w) - u[:, c0:c0 + POOL_GROUP_DIM])
    b = _pool_out(jnp.concatenate(ds, axis=1), pw_ref, ps_ref, szp_ref[...])
    a = (o_ref[...] * sza_ref[...]).astype(BF16)
    y_ref[...] = _finish(x_ref[...], a, b, wo_ref, fg_ref)


def _out_sample(o2, sza, u, state_t, szp, x2, pool_w, pool_scale, w_out, final_g):
    args = (o2, sza, u, state_t, szp, x2, pool_w, pool_scale, w_out, final_g)
    full = lambda a: pl.BlockSpec(a.shape, lambda i: (0,) * a.ndim)
    return pl.pallas_call(
        _out_sample_kernel,
        grid=(1,),
        in_specs=[full(a) for a in args],
        out_specs=full(x2),
        out_shape=jax.ShapeDtypeStruct(x2.shape, F32),
        compiler_params=_cparams(("arbitrary",)),
        name="out_sample",
    )(*args)


def _cmp_sample_kernel(pt_ref, *refs, n_pages, page, n_seq):
    del pt_ref
    pages = refs[:n_seq * n_pages]
    qbd_ref, pe_ref, w1_ref, w2_ref, oc_ref, imp_ref, rows_ref = refs[n_seq * n_pages:]
    nchunk = n_pages * page // CMP_STRIDE
    for i, page_ref in enumerate(pages):
        for kv in range(2):
            rows_ref[kv, i * page:(i + 1) * page, :] = page_ref[kv * LANES:(kv + 1) * LANES, :].T
    fulls = []
    for kv in range(2):
        load = lambda s: rows_ref[kv, pl.ds(s, n_seq * nchunk, stride=CMP_STRIDE), :]
        h = _compress_hidden(load, pe_ref, w1_ref, kv, n_seq * nchunk)
        fulls.append(jnp.dot(_silu(h).astype(BF16), w2_ref[kv], preferred_element_type=F32).astype(BF16))
    k_c, v_c = fulls
    nrow = n_seq * N_HEADS
    qbd = qbd_ref[...].reshape(nrow, LANES).astype(BF16)
    s = lax.dot_general(qbd, k_c, (((1,), (1,)), ((), ())), preferred_element_type=F32)
    col = lax.broadcasted_iota(jnp.int32, s.shape, 1)
    own = col // nchunk == lax.broadcasted_iota(jnp.int32, s.shape, 0) // N_HEADS
    ok = own & (col % nchunk < nchunk - 1)
    s = jnp.where(ok, s, NEG_FILL)
    e = jnp.where(ok, jnp.exp2(s - jnp.max(s, axis=1, keepdims=True)), 0.0)
    pc = (e * (1.0 / jnp.sum(e, axis=1, keepdims=True))).astype(BF16)
    cover = jnp.concatenate([_cover_matrix(nchunk, LANES).astype(BF16)] * n_seq, axis=0)
    oc_ref[...] = jnp.dot(pc, v_c, preferred_element_type=F32).reshape(n_seq, N_HEADS, LANES)
    imp_ref[...] = jnp.dot(pc, cover, preferred_element_type=F32).reshape(n_seq, N_HEADS, LANES)


def _cmp_sample(page_table, cache_t, qbd, pe_t, w1bd, w2bd):
    nb, n_pages = page_table.shape
    page = cache_t.shape[-1]
    n_seq = SAMPLE_GROUP
    page_spec = lambda g, p: pl.BlockSpec((None, KV_W, page), lambda b, pt: (pt[b * n_seq + g, p], 0, 0))
    const = lambda a: pl.BlockSpec(a.shape, lambda b, pt: (0,) * a.ndim)
    per_b = pl.BlockSpec((n_seq, N_HEADS, LANES), lambda b, pt: (b, 0, 0))
    grid_spec = pltpu.PrefetchScalarGridSpec(
        num_scalar_prefetch=1,
        grid=(nb // n_seq,),
        in_specs=[page_spec(g, p) for g in range(n_seq) for p in range(n_pages)]
        + [per_b, const(pe_t), const(w1bd), const(w2bd)],
        out_specs=(per_b, per_b),
        scratch_shapes=[pltpu.VMEM((2, n_seq * n_pages * page, LANES), F32)],
    )
    return pl.pallas_call(
        functools.partial(_cmp_sample_kernel, n_pages=n_pages, page=page, n_seq=n_seq),
        grid_spec=grid_spec,
        out_shape=(jax.ShapeDtypeStruct((nb, N_HEADS, LANES), F32),
                   jax.ShapeDtypeStruct((nb, N_HEADS, LANES), F32)),
        compiler_params=_cparams(("arbitrary",)),
        name="cmp_sample",
    )(page_table, *([cache_t] * (n_seq * n_pages)), qbd, pe_t, w1bd, w2bd)


def _topk_sample_kernel(imp_ref, out_ref, *, t_pos, n_blocks):
    imp = imp_ref[0] + imp_ref[1] + imp_ref[2] + imp_ref[3]
    t = jnp.full((imp.shape[0], 1), t_pos, jnp.int32)
    out_ref[...] = _not_selected(_masked_importance(imp, t, n_blocks, 1), min(N_SEL, n_blocks), 1)


def _topk_sample(imp_g, *, t_pos, n_blocks):
    rows = imp_g.shape[1]
    return pl.pallas_call(
        functools.partial(_topk_sample_kernel, t_pos=t_pos, n_blocks=n_blocks),
        grid=(1,),
        in_specs=[pl.BlockSpec(imp_g.shape, lambda i: (0, 0, 0))],
        out_specs=pl.BlockSpec((rows, LANES), lambda i: (0, 0)),
        out_shape=jax.ShapeDtypeStruct((rows, LANES), F32),
        compiler_params=_cparams(("arbitrary",)),
        name="topk_sample",
    )(imp_g)


def _attn_sample_kernel(pt_ref, *refs, n_pages, page, n_seq):
    del pt_ref
    pages = refs[:n_seq * n_pages]
    (win_ref, qbd_ref, ns_ref, snew_ref, wnew_ref, wcol_ref, oc_ref, gate_ref, o_ref, nwin_ref) = refs[n_seq * n_pages:]
    nrow = n_seq * N_HEADS
    qf = qbd_ref[...].reshape(nrow, LANES)
    qbd = qf.astype(BF16)
    ns = ns_ref[...].reshape(nrow, LANES)
    row_seq = lax.broadcasted_iota(jnp.int32, (nrow, 1), 0) // N_HEADS
    nt_ = (((1,), (1,)), ((), ()))

    def new_token(rows_ref):
        rows = jnp.concatenate([jnp.broadcast_to(rows_ref[g], (N_HEADS, KV_W)) for g in range(n_seq)], axis=0)
        k_new = rows[:, 0:LANES].astype(BF16).astype(F32)
        v_new = rows[:, LANES:2 * LANES].astype(BF16).astype(F32)
        return jnp.sum(qf * k_new, axis=1, keepdims=True), v_new

    lane = lax.broadcasted_iota(jnp.int32, (nrow, page), 1)
    per_page = page // SEL_BLOCK
    scores = []
    for i, page_ref in enumerate(pages):
        g, p = divmod(i, n_pages)
        s = jnp.dot(qbd, page_ref[0:LANES, :].astype(BF16), preferred_element_type=F32)
        flag = ns[:, p * per_page:p * per_page + 1]
        for j in range(1, per_page):
            flag = jnp.where(lane >= j * SEL_BLOCK, ns[:, p * per_page + j:p * per_page + j + 1], flag)
        scores.append(jnp.where((row_seq == g) & (flag < 0.5), s, NEG_FILL))
    nb_cache = n_pages * per_page
    s_new, v_new = new_token(snew_ref)
    s_new = jnp.where(ns[:, nb_cache:nb_cache + 1] > 0.5, NEG_FILL, s_new)
    s_max = scores[0]
    for s in scores[1:]:
        s_max = jnp.maximum(s_max, s)
    m = jnp.maximum(s_new, jnp.max(s_max, axis=1, keepdims=True))
    e_new = jnp.exp2(s_new - m)
    e_sum = jnp.zeros((nrow, page), F32)
    acc = e_new * v_new
    for s, page_ref in zip(scores, pages):
        e = jnp.exp2(s - m).astype(BF16)
        e_sum = e_sum + e.astype(F32)
        acc = acc + lax.dot_general(e, page_ref[LANES:2 * LANES, :].astype(BF16), nt_, preferred_element_type=F32)
    o_s = acc * (1.0 / (e_new + jnp.sum(e_sum, axis=1, keepdims=True)))

    wlen = win_ref.shape[2]
    first = max(wlen - WINDOW + 1, 0)
    in_win = lax.broadcasted_iota(jnp.int32, (nrow, wlen), 1) >= first
    scores_w = []
    for g in range(n_seq):
        s_w = jnp.dot(qbd, win_ref[g, 0:LANES, :].astype(BF16), preferred_element_type=F32)
        scores_w.append(jnp.where((row_seq == g) & in_win, s_w, NEG_FILL))
    sw_new, vw_new = new_token(wnew_ref)
    sw_max = scores_w[0]
    for s_w in scores_w[1:]:
        sw_max = jnp.maximum(sw_max, s_w)
    m_w = jnp.maximum(sw_new, jnp.max(sw_max, axis=1, keepdims=True))
    ew_new = jnp.exp2(sw_new - m_w)
    ew_sum = jnp.zeros((nrow, wlen), F32)
    acc_w = ew_new * vw_new
    for g in range(n_seq):
        e_w = jnp.exp2(scores_w[g] - m_w).astype(BF16)
        ew_sum = ew_sum + e_w.astype(F32)
        acc_w = acc_w + lax.dot_general(e_w, win_ref[g, LANES:2 * LANES, :].astype(BF16), nt_,
                                        preferred_element_type=F32)
    o_w = acc_w * (1.0 / (ew_new + jnp.sum(ew_sum, axis=1, keepdims=True)))

    gates = gate_ref[...].reshape(nrow, LANES)
    o = gates[:, 0:1] * oc_ref[...].reshape(nrow, LANES) + gates[:, 1:2] * o_s + gates[:, 2:3] * o_w
    o_ref[...] = o.reshape(n_seq, N_HEADS, LANES)

    last = lax.broadcasted_iota(jnp.int32, (KV_W, wlen), 1) == wlen - 1
    for g in range(n_seq):
        nwin_ref[g] = jnp.where(last, wcol_ref[g], pltpu.roll(win_ref[g], wlen - 1, 1))


def _attn_sample(page_table, cache_t, win_t, qbd, ns8, s_new, w_new, w_col, o_c, gates8):
    nb, n_pages = page_table.shape
    page = cache_t.shape[-1]
    n_seq = SAMPLE_GROUP
    page_spec = lambda g, p: pl.BlockSpec((None, KV_W, page), lambda b, pt: (pt[b * n_seq + g, p], 0, 0))
    per_b = lambda a: pl.BlockSpec((n_seq,) + a.shape[1:], lambda b, pt: (b,) + (0,) * (a.ndim - 1))
    grid_spec = pltpu.PrefetchScalarGridSpec(
        num_scalar_prefetch=1,
        grid=(nb // n_seq,),
        in_specs=[page_spec(g, p) for g in range(n_seq) for p in range(n_pages)]
        + [per_b(a) for a in (win_t, qbd, ns8, s_new, w_new, w_col, o_c, gates8)],
        out_specs=(per_b(o_c), per_b(win_t)),
    )
    return pl.pallas_call(
        functools.partial(_attn_sample_kernel, n_pages=n_pages, page=page, n_seq=n_seq),
        grid_spec=grid_spec,
        out_shape=(jax.ShapeDtypeStruct(o_c.shape, F32), jax.ShapeDtypeStruct(win_t.shape, F32)),
        compiler_params=_cparams(("arbitrary",)),
        name="attn_sample",
    )(page_table, *([cache_t] * (n_seq * n_pages)), win_t, qbd, ns8, s_new, w_new, w_col, o_c, gates8)


def _prep_w_in(w_in):
    o = D_ATTN + 3 * KV_W
    ng = GQA * N_BRANCH
    gl = w_in[:, o:o + N_KV_HEADS * ng]
    rest = w_in[:, o + N_KV_HEADS * ng:]
    pad = jnp.zeros((w_in.shape[0], LANES - ng), w_in.dtype)
    return jnp.concatenate([w_in[:, :o], rest, gl[:, :ng], pad, gl[:, ng:], pad], axis=1).astype(BF16)


def _prep_compress(cmp_pe, cmp_w1, cmp_w2):
    r = CMP_BLOCK // CMP_STRIDE
    pe_t = jnp.tile(cmp_pe.reshape(2, r, CMP_STRIDE, 1, HEAD_DIM), (1, 1, 1, 1, N_KV_HEADS))
    w1 = cmp_w1.reshape(2, r, CMP_STRIDE, HEAD_DIM, CMP_HIDDEN)
    z1 = jnp.zeros_like(w1)
    top = jnp.concatenate([w1, z1], axis=-1)
    bot = jnp.concatenate([z1, w1], axis=-1)
    w1bd = jnp.concatenate([top, bot], axis=3)
    w1bd = w1bd.reshape(2, r, CMP_STRIDE * LANES, N_KV_HEADS * CMP_HIDDEN).astype(BF16)
    z2 = jnp.zeros_like(cmp_w2)
    w2bd = jnp.concatenate([jnp.concatenate([cmp_w2, z2], axis=-1),
                            jnp.concatenate([z2, cmp_w2], axis=-1)], axis=1).astype(BF16)
    return pe_t, w1bd, w2bd


def _kv_out(kv_t):
    n, _, rows = kv_t.shape
    return jnp.transpose(kv_t.reshape(n, 2, N_KV_HEADS, HEAD_DIM, rows), (0, 4, 1, 2, 3))[None]


def kernel(x_prompt, x_sample, cache_cmp_kv, cache_slc_kv, cache_win_kv, state_pool, page_table, norm_g, w_in,
           cmp_pe, cmp_w1, cmp_w2, pool_w, pool_scale, w_out, final_g):
    n, seq_len, d_model = x_prompt.shape
    nb = x_sample.shape[0]
    n_phys, page = cache_cmp_kv.shape[1], cache_cmp_kv.shape[2]
    n_pages = page_table.shape[1]
    past_len = n_pages * page
    wlen = cache_win_kv.shape[2]

    w_r = _prep_w_in(w_in[0])
    pe_t, w1bd, w2bd = _prep_compress(cmp_pe[0], cmp_w1[0], cmp_w2[0])
    pool_w_b = pool_w[0].astype(BF16)
    w_out_b = w_out[0].astype(BF16)
    fg = final_g.reshape(1, d_model)

    (qt, crm, ckv_t, skv_t, wkv_t, ks, vst, kw, vwt, gates_t, sza, u, szp) = _project(
        x_prompt.reshape(n * seq_len, d_model), norm_g, w_r, tm=512, seq_len=seq_len)
    r3 = lambda a: a.reshape(n, seq_len, a.shape[-1])
    kc, rhs_c = _compress_prompt(crm, n, pe_t, w1bd, w2bd)
    a3 = _attn_prompt(qt, gates_t, r3(sza), kc, rhs_c, ks, vst, kw, vwt)
    y_prompt = _out_prompt(a3, r3(u), r3(szp), x_prompt, pool_w_b, pool_scale, w_out_b, fg, tm=512)

    new_cmp_p = _kv_out(ckv_t)
    new_slc_p = _kv_out(skv_t)
    new_win_p = _kv_out(wkv_t[:, :, seq_len - min(WINDOW, seq_len):])
    new_pool_p = r3(u)[:, seq_len - POOL_STATE:][None]

    (qt_s, _, ckv_ts, skv_ts, wkv_ts, _, _, _, _, gates_ts, sza_s, u_s, szp_s) = _project(
        x_sample.reshape(nb, d_model), norm_g, w_r, tm=nb, seq_len=nb)
    q_s = qt_s.T
    q5 = q_s.reshape(nb, N_KV_HEADS, GQA, 1, HEAD_DIM)
    eye = jnp.eye(N_KV_HEADS, dtype=q_s.dtype).reshape(1, N_KV_HEADS, 1, N_KV_HEADS, 1)
    qbd = (q5 * eye).reshape(nb, N_HEADS, LANES).astype(F32)

    to_pages = lambda c: jnp.transpose(c[0], (0, 2, 3, 4, 1)).reshape(n_phys, KV_W, page)
    o_c, imp8 = _cmp_sample(page_table, to_pages(cache_cmp_kv), qbd, pe_t, w1bd, w2bd)
    imp_g = jnp.transpose(imp8.reshape(nb * N_KV_HEADS, GQA, LANES), (1, 0, 2))
    n_blocks = -(-(past_len + 1) // SEL_BLOCK)
    notsel = _topk_sample(imp_g, t_pos=past_len, n_blocks=n_blocks)
    ns8 = jnp.repeat(notsel.reshape(nb, N_KV_HEADS, 1, LANES), GQA, axis=2).reshape(nb, N_HEADS, LANES)

    gates8 = jnp.transpose(gates_ts[:, :GQA * N_BRANCH, :], (2, 0, 1)).reshape(nb, N_HEADS, N_BRANCH)
    gates8 = jnp.pad(gates8, ((0, 0), (0, 0), (0, LANES - N_BRANCH)))
    win_t = jnp.transpose(cache_win_kv[0], (0, 2, 3, 4, 1)).reshape(nb, KV_W, wlen)
    s_new = skv_ts[0].T.reshape(nb, 1, KV_W)
    w_new = wkv_ts[0].T.reshape(nb, 1, KV_W)
    w_col = w_new.reshape(nb, KV_W, 1)
    o8, nwin_t = _attn_sample(page_table, to_pages(cache_slc_kv), win_t, qbd, ns8, s_new, w_new, w_col, o_c, gates8)
    o5 = o8.reshape(nb, N_KV_HEADS, GQA, N_KV_HEADS, HEAD_DIM)
    o2 = jnp.stack([o5[:, k, :, k, :] for k in range(N_KV_HEADS)], axis=1).reshape(nb, D_ATTN)

    state_t = jnp.transpose(state_pool[0], (1, 0, 2))
    y_sample = _out_sample(o2, sza_s, u_s, state_t, szp_s, x_sample.reshape(nb, d_model),
                           pool_w_b, pool_scale, w_out_b, fg).reshape(nb, 1, d_model)

    kv_out_s = lambda t: jnp.transpose(t.reshape(2, N_KV_HEADS, HEAD_DIM, nb, 1), (3, 4, 0, 1, 2))[None]
    new_cmp_s = kv_out_s(ckv_ts)
    new_slc_s = kv_out_s(skv_ts)
    new_win_s = jnp.transpose(nwin_t.reshape(nb, 2, N_KV_HEADS, HEAD_DIM, wlen), (0, 4, 1, 2, 3))[None]
    new_pool_s = jnp.transpose(jnp.concatenate([state_t[1:], u_s[None]], axis=0), (1, 0, 2))[None]

    return (y_prompt, y_sample, new_cmp_p, new_slc_p, new_win_p, new_pool_p,
            new_cmp_s, new_slc_s, new_win_s, new_pool_s)
```

```python
import functools

import jax
import jax.numpy as jnp
from jax import lax
from jax.experimental import pallas as pl
from jax.experimental.pallas import tpu as pltpu

F32 = jnp.float32
BF16 = jnp.bfloat16

HEAD_DIM = 64
N_KV_HEADS = 2
GQA = 4
N_HEADS = N_KV_HEADS * GQA
D_ATTN = N_HEADS * HEAD_DIM
KV_W = 2 * N_KV_HEADS * HEAD_DIM
N_BRANCH = 3
D_POOL = 512
CMP_BLOCK = 32
CMP_STRIDE = 16
CMP_HIDDEN = 128
SEL_BLOCK = 64
N_SEL = 16
WINDOW = 512
Q_BLOCK = 256
POOL_WINDOWS = (2, 4, 8, 16)
POOL_GROUP_DIM = 128
POOL_STATE = 15
RMS_EPS = 1e-6
LANES = 128
NEG_BIAS = -(2.0 ** 30)
NEG_FILL = -1e30
GATE_COLS = 2 * LANES
GATE_ROWS = 16
BF16_SUBLANES = 16
V_ROWS = HEAD_DIM + BF16_SUBLANES
KEY_TILE = 512
FORCED_VALUE = 1e9
N_FORCED = 3
SAMPLE_GROUP = 4
SEL_CHAINS = 2
LOG2E = 1.4426950408889634
P_PAD = D_ATTN + 3 * KV_W + 3 * 512 + GATE_COLS
VMEM_LIMIT = 48 * 1024 * 1024


def _cparams(sem):
    return pltpu.CompilerParams(dimension_semantics=sem, vmem_limit_bytes=VMEM_LIMIT)


def _silu(z):
    return z * jax.nn.sigmoid(z)


def _proj_kernel(x_ref, g_ref, w_ref, qt_ref, crm_ref, ct_ref, st_ref, wt_ref, ks_ref, vst_ref, kw_ref, vwt_ref,
                 gatet_ref, sza_ref, u_ref, szp_ref, *, tm, seq_len):
    i = pl.program_id(0)
    x = x_ref[...]
    ms = jnp.mean(x * x, axis=-1, keepdims=True)
    xn = (x * lax.rsqrt(ms + RMS_EPS) * g_ref[...]).astype(BF16)

    def mm(c0, c1):
        return jnp.dot(xn, w_ref[:, c0:c1], preferred_element_type=F32)

    qt_ref[...] = (mm(0, D_ATTN) * (HEAD_DIM ** -0.5 * LOG2E)).T.astype(BF16)
    kv = mm(D_ATTN, D_ATTN + 3 * KV_W)
    crm_ref[0] = kv[:, 0:LANES]
    crm_ref[1] = kv[:, LANES:2 * LANES]
    kvt = kv.T
    ct_ref[...] = kvt[0:KV_W, :]
    st_ref[...] = kvt[KV_W:2 * KV_W, :]
    wt_ref[...] = kvt[2 * KV_W:3 * KV_W, :]

    lane = lax.broadcasted_iota(jnp.int32, (tm, LANES), 1)
    pos = (i * tm) % seq_len + lax.broadcasted_iota(jnp.int32, (tm, LANES), 0)
    onehot = jnp.where(lane - HEAD_DIM == (pos // SEL_BLOCK) % HEAD_DIM, NEG_BIAS, 0.0).astype(F32)
    ones_rows = jnp.where(lax.broadcasted_iota(jnp.int32, (V_ROWS - HEAD_DIM, tm), 0) == 0, 1.0, 0.0).astype(BF16)
    for kvh in range(N_KV_HEADS):
        for (base, k_dst, v_dst, k_pad) in ((KV_W, ks_ref, vst_ref, onehot), (2 * KV_W, kw_ref, vwt_ref, 0.0)):
            slab = kv[:, base:base + LANES]
            if kvh == 1:
                slab = pltpu.roll(slab, HEAD_DIM, 1)
            k_dst[kvh, :, :] = jnp.where(lane < HEAD_DIM, slab, k_pad).astype(BF16)
            v0 = base + LANES + kvh * HEAD_DIM
            v_dst[kvh, 0:HEAD_DIM, :] = kvt[v0:v0 + HEAD_DIM, :].astype(BF16)
            v_dst[kvh, HEAD_DIM:V_ROWS, :] = ones_rows

    c = D_ATTN + 3 * KV_W
    sza_ref[...] = _silu(mm(c, c + 512)).astype(BF16)
    u_ref[...] = mm(c + 512, c + 1024)
    szp_ref[...] = _silu(mm(c + 1024, c + 1536)).astype(BF16)
    gate_t = jax.nn.sigmoid(mm(c + 1536, c + 1536 + GATE_COLS)).T
    for kvh in range(N_KV_HEADS):
        gatet_ref[kvh, :, :] = gate_t[kvh * LANES:kvh * LANES + GATE_ROWS, :]


def _project(x2d, norm_g, w_r, *, tm, seq_len):
    rows = x2d.shape[0]
    nt = rows // tm
    tps = seq_len // tm
    row_blk = lambda w: pl.BlockSpec((tm, w), lambda i: (i, 0))
    kv_t = jax.ShapeDtypeStruct((rows // seq_len, KV_W, seq_len), F32)
    kv_t_blk = pl.BlockSpec((None, KV_W, tm), lambda i: (i // tps, 0, i % tps))
    k_rm = jax.ShapeDtypeStruct((N_KV_HEADS, rows, LANES), BF16)
    k_rm_blk = pl.BlockSpec((N_KV_HEADS, tm, LANES), lambda i: (0, i, 0))
    v_t = jax.ShapeDtypeStruct((N_KV_HEADS, V_ROWS, rows), BF16)
    v_t_blk = pl.BlockSpec((N_KV_HEADS, V_ROWS, tm), lambda i: (0, 0, i))
    out_shape = (
        jax.ShapeDtypeStruct((D_ATTN, rows), BF16),
        jax.ShapeDtypeStruct((2, rows, LANES), F32),
        kv_t, kv_t, kv_t,
        k_rm,
        v_t,
        k_rm,
        v_t,
        jax.ShapeDtypeStruct((N_KV_HEADS, GATE_ROWS, rows), F32),
        jax.ShapeDtypeStruct((rows, 512), BF16),
        jax.ShapeDtypeStruct((rows, 512), F32),
        jax.ShapeDtypeStruct((rows, 512), BF16),
    )
    out_specs = (
        pl.BlockSpec((D_ATTN, tm), lambda i: (0, i)),
        pl.BlockSpec((2, tm, LANES), lambda i: (0, i, 0)),
        kv_t_blk, kv_t_blk, kv_t_blk,
        k_rm_blk, v_t_blk, k_rm_blk, v_t_blk,
        pl.BlockSpec((N_KV_HEADS, GATE_ROWS, tm), lambda i: (0, 0, i)),
        row_blk(512), row_blk(512), row_blk(512),
    )
    return pl.pallas_call(
        functools.partial(_proj_kernel, tm=tm, seq_len=seq_len),
        grid=(nt,),
        in_specs=[row_blk(x2d.shape[1]),
                  pl.BlockSpec((1, x2d.shape[1]), lambda i: (0, 0)),
                  pl.BlockSpec(w_r.shape, lambda i: (0, 0))],
        out_specs=out_specs,
        out_shape=out_shape,
        compiler_params=_cparams(("arbitrary",)),
        name="proj",
    )(x2d, norm_g, w_r)


def _compress_hidden(load_rows, pe_ref, w1_ref, kv, m):
    xs = [load_rows(s) for s in range(CMP_STRIDE)]
    hs = []
    for sub in range(CMP_BLOCK // CMP_STRIDE):
        lhs = jnp.concatenate([(xs[s] + pe_ref[kv, sub, s]).astype(BF16) for s in range(CMP_STRIDE)], axis=1)
        hs.append(jnp.dot(lhs, w1_ref[kv, sub], preferred_element_type=F32))
    return hs[0] + pltpu.roll(hs[1], m - 1, 0)


def _cover_matrix(nc_pad, ns_pad):
    c0 = lax.broadcasted_iota(jnp.int32, (nc_pad, ns_pad), 0) * CMP_STRIDE
    s0 = lax.broadcasted_iota(jnp.int32, (nc_pad, ns_pad), 1) * SEL_BLOCK
    return jnp.where((c0 < s0 + SEL_BLOCK) & (c0 + CMP_BLOCK > s0), 1.0, 0.0).astype(F32)


def _compress_prompt_kernel(c_ref, pe_ref, w1_ref, w2_ref, kc_ref, rhs_ref, *, nchunk):
    ones_rows = jnp.where(lax.broadcasted_iota(jnp.int32, (HEAD_DIM, nchunk), 0) == 0, 1.0, 0.0).astype(BF16)
    s0 = lax.broadcasted_iota(jnp.int32, (LANES, nchunk), 0) * SEL_BLOCK
    c0 = lax.broadcasted_iota(jnp.int32, (LANES, nchunk), 1) * CMP_STRIDE
    cover_t = jnp.where((c0 < s0 + SEL_BLOCK) & (c0 + CMP_BLOCK > s0), 1.0, 0.0).astype(BF16)
    for kv in range(2):
        load = lambda s: c_ref[kv, pl.ds(s, nchunk, stride=CMP_STRIDE), :]
        h = _compress_hidden(load, pe_ref, w1_ref, kv, nchunk)
        full = jnp.dot(_silu(h).astype(BF16), w2_ref[kv], preferred_element_type=F32)
        if kv == 0:
            kc_ref[0, :, :] = full[:, 0:HEAD_DIM].astype(BF16)
            kc_ref[1, :, :] = full[:, HEAD_DIM:2 * HEAD_DIM].astype(BF16)
        else:
            full_t = full.T
            for kvh in range(N_KV_HEADS):
                rhs_ref[kvh, 0:HEAD_DIM, :] = full_t[kvh * HEAD_DIM:(kvh + 1) * HEAD_DIM, :].astype(BF16)
                rhs_ref[kvh, HEAD_DIM:LANES, :] = ones_rows
                rhs_ref[kvh, LANES:2 * LANES, :] = cover_t


def _compress_prompt(crm, n, pe_t, w1bd, w2bd):
    seq_len = crm.shape[1] // n
    nchunk = seq_len // CMP_STRIDE
    return pl.pallas_call(
        functools.partial(_compress_prompt_kernel, nchunk=nchunk),
        grid=(n,),
        in_specs=[pl.BlockSpec((2, seq_len, LANES), lambda b: (0, b, 0)),
                  pl.BlockSpec(pe_t.shape, lambda b: (0, 0, 0, 0, 0)),
                  pl.BlockSpec(w1bd.shape, lambda b: (0, 0, 0, 0)),
                  pl.BlockSpec(w2bd.shape, lambda b: (0, 0, 0))],
        out_specs=(pl.BlockSpec((None, N_KV_HEADS, nchunk, HEAD_DIM), lambda b: (b, 0, 0, 0)),
                   pl.BlockSpec((None, N_KV_HEADS, 2 * LANES, nchunk), lambda b: (b, 0, 0, 0))),
        out_shape=(jax.ShapeDtypeStruct((n, N_KV_HEADS, nchunk, HEAD_DIM), BF16),
                   jax.ShapeDtypeStruct((n, N_KV_HEADS, 2 * LANES, nchunk), BF16)),
        compiler_params=_cparams(("arbitrary",)),
        name="compress_prompt",
    )(crm, pe_t, w1bd, w2bd)


def _not_selected(val, n_top, axis):
    blk = lax.broadcasted_iota(jnp.int32, val.shape, axis).astype(F32)
    forced = val >= FORCED_VALUE
    notsel = jnp.where(forced, 0.0, 1.0).astype(F32)
    val = jnp.where(forced, -3e38, val)
    for _ in range(max(n_top - N_FORCED, 0)):
        m = jnp.max(val, axis=axis, keepdims=True)
        idx = jnp.min(jnp.where(val == m, blk, float(LANES)), axis=axis, keepdims=True)
        pick = blk == idx
        notsel = jnp.where(pick, 0.0, notsel)
        val = jnp.where(pick, -3e38, val)
    return notsel


def _masked_importance(imp, t, n_blocks, axis):
    blk = lax.broadcasted_iota(jnp.int32, imp.shape, axis)
    cur = t // SEL_BLOCK
    forced = (blk == 0) | (blk == cur) | (blk == cur - 1)
    val = jnp.where(forced, FORCED_VALUE, jnp.where(blk * SEL_BLOCK <= t, imp, -1e9))
    return jnp.where(blk < n_blocks, val, -2e9)


def _attn_prompt_kernel(qt_ref, gatet_ref, sza_ref, kc_ref, rhs_ref, ks_ref, vst_ref, kw_ref, vwt_ref, out_ref,
                        qsel_ref, sbuf_ref, m_ref, acc_ref, mw_ref, accw_ref, oc_ref,
                        *, n_sel_blocks, seq_len):
    qb = pl.program_id(2)
    qs = qb * Q_BLOCK
    hq = GQA * Q_BLOCK
    t_row = qs + lax.broadcasted_iota(jnp.int32, (1, Q_BLOCK), 1)
    t_all = jnp.concatenate([t_row] * GQA, axis=1)
    qt = qt_ref[...]
    qt_all = jnp.concatenate([qt[g * HEAD_DIM:(g + 1) * HEAD_DIM, :] for g in range(GQA)], axis=1)
    zeros_lo = jnp.zeros((HEAD_DIM, hq), BF16)
    qw = jnp.concatenate([qt_all, zeros_lo], axis=0)

    tile = min(KEY_TILE, seq_len)
    n_full = qs // tile
    lo_tiles = HEAD_DIM * SEL_BLOCK // tile
    wwidth = min(WINDOW, seq_len)
    causal = lambda kpos: kpos <= t_all
    band = lambda kpos: (kpos > t_all - WINDOW) & (kpos <= t_all)
    diag_w = lambda kpos: (kpos <= t_all) & (qs >= wwidth)

    def flash_steps(steps):
        staged = []
        for (q, k_ref, vt_ref, start, width, mask_fn, mref, aref, c) in steps:
            start = pl.multiple_of(start, LANES)
            s = jnp.dot(k_ref[pl.ds(start, width), :], q, preferred_element_type=F32)
            if mask_fn is not None:
                s = jnp.where(mask_fn(start + lax.broadcasted_iota(jnp.int32, (width, hq), 0)), s, NEG_FILL)
            staged.append((s, vt_ref[:, pl.ds(start, width)]))
        softmaxed = []
        for (s, vt), step in zip(staged, steps):
            mref, c = step[6], step[8]
            m_old = mref[c]
            m_new = jnp.maximum(m_old, jnp.max(s, axis=0, keepdims=True))
            softmaxed.append((jnp.exp2(s - m_new).astype(BF16), jnp.exp2(m_old - m_new), m_new, vt))
        for (p, alpha, m_new, vt), step in zip(softmaxed, steps):
            mref, aref, c = step[6], step[7], step[8]
            aref[c] = alpha * aref[c] + jnp.dot(vt, p, preferred_element_type=F32)
            mref[c] = m_new

    m_ref[...] = jnp.full(m_ref.shape, NEG_FILL, F32)
    acc_ref[...] = jnp.zeros(acc_ref.shape, F32)
    mw_ref[...] = jnp.full(mw_ref.shape, NEG_FILL, F32)
    accw_ref[...] = jnp.zeros(accw_ref.shape, F32)

    nc = kc_ref.shape[0]
    c_last = lax.broadcasted_iota(jnp.int32, (nc, hq), 0) * CMP_STRIDE + (CMP_BLOCK - 1)
    ok_c = c_last <= t_all
    s = jnp.dot(kc_ref[...], qt_all, preferred_element_type=F32)
    flash_steps([(qw, kw_ref, vwt_ref, qs, Q_BLOCK, diag_w, mw_ref, accw_ref, 1)])
    s = jnp.where(ok_c, s, NEG_FILL)
    e = jnp.where(ok_c, jnp.exp2(s - jnp.max(s, axis=0, keepdims=True)), 0.0)
    r = jnp.dot(rhs_ref[...], e.astype(BF16), preferred_element_type=F32)
    inv = 1.0 / jnp.maximum(r[HEAD_DIM:HEAD_DIM + 1, :], 1e-30)
    qw_late = jnp.where(inv > -1.0, qw, jnp.zeros_like(qw))
    flash_steps([(qw_late, kw_ref, vwt_ref, jnp.maximum(qs - wwidth, 0), wwidth, band, mw_ref, accw_ref, 0)])
    oc_ref[...] = r[0:HEAD_DIM, :] * inv
    imp_all = r[LANES:2 * LANES, :] * inv
    imp = imp_all[:, 0:Q_BLOCK]
    for g in range(1, GQA):
        imp = imp + imp_all[:, g * Q_BLOCK:(g + 1) * Q_BLOCK]

    notsel = _not_selected(_masked_importance(imp, t_row, n_sel_blocks, 0), min(N_SEL, n_sel_blocks), 0)
    notsel = notsel.astype(BF16)
    q_lo = jnp.concatenate([qt_all, jnp.concatenate([notsel[0:HEAD_DIM, :]] * GQA, axis=1)], axis=0)
    q_hi = jnp.concatenate([qt_all, jnp.concatenate([notsel[HEAD_DIM:2 * HEAD_DIM, :]] * GQA, axis=1)], axis=0)
    qsel_ref[0] = q_lo
    qsel_ref[1] = q_hi

    def scores(b, kt, masked):
        start = pl.multiple_of(kt * tile, LANES)
        q = qsel_ref[jnp.where(kt >= lo_tiles, 1, 0)]
        s = jnp.dot(ks_ref[pl.ds(start, tile), :], q, preferred_element_type=F32)
        if masked:
            s = jnp.where(causal(start + lax.broadcasted_iota(jnp.int32, (tile, hq), 0)), s, NEG_FILL)
        sbuf_ref[b] = s

    def consume(b, kt):
        start = pl.multiple_of(kt * tile, LANES)
        s = sbuf_ref[b]
        m_old = m_ref[b]
        m_new = jnp.maximum(m_old, jnp.max(s, axis=0, keepdims=True))
        p = jnp.exp2(s - m_new).astype(BF16)
        acc_ref[b] = jnp.exp2(m_old - m_new) * acc_ref[b] + jnp.dot(vst_ref[:, pl.ds(start, tile)], p,
                                                                  preferred_element_type=F32)
        m_ref[b] = m_new

    @pl.when(n_full == 0)
    def _():
        scores(0, 0, True)

    @pl.when(n_full > 0)
    def _():
        scores(0, 0, False)

    def pair_body(j, carry):
        scores(1, 2 * j + 1, False)
        consume(0, 2 * j)
        scores(0, 2 * j + 2, False)
        consume(1, 2 * j + 1)
        return carry

    lax.fori_loop(0, jnp.maximum(n_full - 1, 0) // 2, pair_body, 0)

    @pl.when((n_full > 0) & (n_full % 2 == 0))
    def _():
        scores(1, n_full - 1, False)
        consume(0, n_full - 2)
        scores(0, n_full, True)
        consume(1, n_full - 1)
        consume(0, n_full)

    @pl.when(n_full % 2 == 1)
    def _():
        scores(1, n_full, True)
        consume(0, n_full - 1)
        consume(1, n_full)

    @pl.when(n_full == 0)
    def _():
        consume(0, 0)

    def merged(mref, aref, n_chain):
        m = mref[0]
        for c in range(1, n_chain):
            m = jnp.maximum(m, mref[c])
        a = jnp.exp2(mref[0] - m) * aref[0]
        for c in range(1, n_chain):
            a = a + jnp.exp2(mref[c] - m) * aref[c]
        return a[0:HEAD_DIM, :] * (1.0 / a[HEAD_DIM:HEAD_DIM + 1, :])

    gates = gatet_ref[...]
    sza = sza_ref[...]
    o_s = merged(m_ref, acc_ref, SEL_CHAINS)
    o_w = merged(mw_ref, accw_ref, 2)
    o_c = oc_ref[...]
    outs = []
    for g in range(GQA):
        cols = slice(g * Q_BLOCK, (g + 1) * Q_BLOCK)
        gc = gates[g * N_BRANCH + 0:g * N_BRANCH + 1, :]
        gs = gates[g * N_BRANCH + 1:g * N_BRANCH + 2, :]
        gw = gates[g * N_BRANCH + 2:g * N_BRANCH + 3, :]
        outs.append((gc * o_c[:, cols] + gs * o_s[:, cols] + gw * o_w[:, cols]).T)
    out_ref[...] = (jnp.concatenate(outs, axis=1) * sza).astype(BF16)


def _attn_prompt(qt, gates_t, sza3, kc, rhs_c, ks, vst, kw, vwt):
    n, seq_len, _ = sza3.shape
    nqb = seq_len // Q_BLOCK
    nchunk = kc.shape[2]
    hw = GQA * HEAD_DIM
    k_blk = pl.BlockSpec((None, seq_len, LANES), lambda b, k, i: (k, b, 0))
    vt_blk = pl.BlockSpec((None, V_ROWS, seq_len), lambda b, k, i: (k, 0, b))
    return pl.pallas_call(
        functools.partial(_attn_prompt_kernel, n_sel_blocks=-(-seq_len // SEL_BLOCK), seq_len=seq_len),
        grid=(n, N_KV_HEADS, nqb),
        in_specs=[
            pl.BlockSpec((hw, Q_BLOCK), lambda b, k, i: (k, b * nqb + i)),
            pl.BlockSpec((None, GATE_ROWS, Q_BLOCK), lambda b, k, i: (k, 0, b * nqb + i)),
            pl.BlockSpec((None, Q_BLOCK, hw), lambda b, k, i: (b, i, k)),
            pl.BlockSpec((None, None, nchunk, HEAD_DIM), lambda b, k, i: (b, k, 0, 0)),
            pl.BlockSpec((None, None, 2 * LANES, nchunk), lambda b, k, i: (b, k, 0, 0)),
            k_blk, vt_blk, k_blk, vt_blk,
        ],
        out_specs=pl.BlockSpec((None, Q_BLOCK, hw), lambda b, k, i: (b, i, k)),
        out_shape=jax.ShapeDtypeStruct((n, seq_len, D_ATTN), BF16),
        scratch_shapes=[
            pltpu.VMEM((2, 2 * HEAD_DIM, GQA * Q_BLOCK), BF16),
            pltpu.VMEM((SEL_CHAINS, min(KEY_TILE, seq_len), GQA * Q_BLOCK), F32),
            pltpu.VMEM((SEL_CHAINS, 1, GQA * Q_BLOCK), F32),
            pltpu.VMEM((SEL_CHAINS, V_ROWS, GQA * Q_BLOCK), F32),
            pltpu.VMEM((2, 1, GQA * Q_BLOCK), F32),
            pltpu.VMEM((2, V_ROWS, GQA * Q_BLOCK), F32),
            pltpu.VMEM((HEAD_DIM, GQA * Q_BLOCK), F32),
        ],
        compiler_params=_cparams(("arbitrary", "arbitrary", "arbitrary")),
        name="attn_prompt",
    )(qt, gates_t, sza3, kc, rhs_c, ks, vst, kw, vwt)


def _pool_out(d, pw_ref, ps_ref, szp):
    ys = [jnp.dot(d[:, g * POOL_GROUP_DIM:(g + 1) * POOL_GROUP_DIM].astype(BF16), pw_ref[g],
                  preferred_element_type=F32) for g in range(len(POOL_WINDOWS))]
    return jnp.concatenate(ys, axis=1) * ps_ref[...] * szp


def _finish(x, a_bf16, b, wo_ref, fg_ref):
    mix = jnp.concatenate([a_bf16, b.astype(BF16)], axis=1)
    y = x + jnp.dot(mix, wo_ref[...], preferred_element_type=F32)
    ms = jnp.mean(y * y, axis=-1, keepdims=True)
    return y * lax.rsqrt(ms + RMS_EPS) * fg_ref[...]


def _out_prompt_kernel(a_ref, u_ref, halo_ref, szp_ref, x_ref, pw_ref, ps_ref, wo_ref, fg_ref, y_ref, ext_ref,
                       *, tm, halo):
    i = pl.program_id(1)
    u = u_ref[...]
    ext_ref[0:halo, :] = jnp.where(i > 0, halo_ref[...], 0.0)
    ext_ref[halo:halo + tm, :] = u
    pos = i * tm + lax.broadcasted_iota(jnp.int32, (tm, POOL_GROUP_DIM), 0)
    ds = []
    for g, w in enumerate(POOL_WINDOWS):
        c0 = g * POOL_GROUP_DIM
        acc = u[:, c0:c0 + POOL_GROUP_DIM]
        for k in range(1, w):
            acc = acc + ext_ref[halo - k:halo - k + tm, c0:c0 + POOL_GROUP_DIM]
        cnt = jnp.minimum(pos + 1, w).astype(F32)
        ds.append(acc / cnt - u[:, c0:c0 + POOL_GROUP_DIM])
    b = _pool_out(jnp.concatenate(ds, axis=1), pw_ref, ps_ref, szp_ref[...])
    y_ref[...] = _finish(x_ref[...], a_ref[...], b, wo_ref, fg_ref)


def _out_prompt(a3, u3, szp3, x3, pool_w, pool_scale, w_out, final_g, *, tm):
    n, seq_len, d_model = x3.shape
    halo = 16
    nt = seq_len // tm
    blk = lambda w: pl.BlockSpec((None, tm, w), lambda b, i: (b, i, 0))
    const = lambda a: pl.BlockSpec(a.shape, lambda b, i: (0,) * a.ndim)
    return pl.pallas_call(
        functools.partial(_out_prompt_kernel, tm=tm, halo=halo),
        grid=(n, nt),
        in_specs=[blk(D_ATTN), blk(D_POOL),
                  pl.BlockSpec((None, halo, D_POOL), lambda b, i: (b, jnp.maximum(i * (tm // halo) - 1, 0), 0)),
                  blk(D_POOL), blk(d_model),
                  const(pool_w), const(pool_scale), const(w_out), const(final_g)],
        out_specs=blk(d_model),
        out_shape=jax.ShapeDtypeStruct((n, seq_len, d_model), F32),
        scratch_shapes=[pltpu.VMEM((tm + halo, D_POOL), F32)],
        compiler_params=_cparams(("arbitrary", "arbitrary")),
        name="out_prompt",
    )(a3, u3, u3, szp3, x3, pool_w, pool_scale, w_out, final_g)


def _out_sample_kernel(o_ref, sza_ref, u_ref, st_ref, szp_ref, x_ref, pw_ref, ps_ref, wo_ref, fg_ref, y_ref):
    u = u_ref[...]
    ds = []
    for g, w in enumerate(POOL_WINDOWS):
        c0 = g * POOL_GROUP_DIM
        acc = u[:, c0:c0 + POOL_GROUP_DIM]
        for k in range(1, w):
            acc = acc + st_ref[POOL_STATE - k, :, c0:c0 + POOL_GROUP_DIM]
        ds.append(acc / float(w) - u[:, c0:c0 + POOL_GROUP_DIM])
    b = _pool_out(jnp.concatenate(ds, axis=1), pw_ref, ps_ref, szp_ref[...])
    a = (o_ref[...] * sza_ref[...]).astype(BF16)
    y_ref[...] = _finish(x_ref[...], a, b, wo_ref, fg_ref)


def _out_sample(o2, sza, u, state_t, szp, x2, pool_w, pool_scale, w_out, final_g):
    args = (o2, sza, u, state_t, szp, x2, pool_w, pool_scale, w_out, final_g)
    full = lambda a: pl.BlockSpec(a.shape, lambda i: (0,) * a.ndim)
    return pl.pallas_call(
        _out_sample_kernel,
        grid=(1,),
        in_specs=[full(a) for a in args],
        out_specs=full(x2),
        out_shape=jax.ShapeDtypeStruct(x2.shape, F32),
        compiler_params=_cparams(("arbitrary",)),
        name="out_sample",
    )(*args)


def _page_copy(pt_ref, cache_ref, buf_ref, sem_ref, step, slot, j, n_pages, n_seq):
    g, p = divmod(j, n_pages)
    return pltpu.make_async_copy(cache_ref.at[pt_ref[step * n_seq + g, p]], buf_ref.at[slot, j], sem_ref.at[slot])


def _fetch_pages(pt_ref, cache_ref, buf_ref, sem_ref, n_pages, n_seq):
    i = pl.program_id(0)
    slot = i % 2
    n_copy = n_seq * n_pages

    @pl.when(i == 0)
    def _():
        for j in range(n_copy):
            _page_copy(pt_ref, cache_ref, buf_ref, sem_ref, 0, 0, j, n_pages, n_seq).start()

    @pl.when(i + 1 < pl.num_programs(0))
    def _():
        for j in range(n_copy):
            _page_copy(pt_ref, cache_ref, buf_ref, sem_ref, i + 1, 1 - slot, j, n_pages, n_seq).start()

    for j in range(n_copy):
        _page_copy(pt_ref, cache_ref, buf_ref, sem_ref, i, slot, j, n_pages, n_seq).wait()
    return slot


def _cmp_sample_kernel(pt_ref, cache_ref, qbd_ref, pe_ref, w1_ref, w2_ref, oc_ref, imp_ref, rows_ref, buf_ref,
                       sem_ref, *, n_pages, page, n_seq):
    slot = _fetch_pages(pt_ref, cache_ref, buf_ref, sem_ref, n_pages, n_seq)
    pages = [buf_ref.at[slot, j] for j in range(n_seq * n_pages)]
    nchunk = n_pages * page // CMP_STRIDE
    for i, page_ref in enumerate(pages):
        for kv in range(2):
            rows_ref[kv, i * page:(i + 1) * page, :] = page_ref[kv * LANES:(kv + 1) * LANES, :].T
    fulls = []
    for kv in range(2):
        load = lambda s: rows_ref[kv, pl.ds(s, n_seq * nchunk, stride=CMP_STRIDE), :]
        h = _compress_hidden(load, pe_ref, w1_ref, kv, n_seq * nchunk)
        fulls.append(jnp.dot(_silu(h).astype(BF16), w2_ref[kv], preferred_element_type=F32).astype(BF16))
    k_c, v_c = fulls
    nrow = n_seq * N_HEADS
    qbd = qbd_ref[...].reshape(nrow, LANES).astype(BF16)
    s = lax.dot_general(qbd, k_c, (((1,), (1,)), ((), ())), preferred_element_type=F32)
    col = lax.broadcasted_iota(jnp.int32, s.shape, 1)
    own = col // nchunk == lax.broadcasted_iota(jnp.int32, s.shape, 0) // N_HEADS
    ok = own & (col % nchunk < nchunk - 1)
    s = jnp.where(ok, s, NEG_FILL)
    e = jnp.where(ok, jnp.exp2(s - jnp.max(s, axis=1, keepdims=True)), 0.0)
    pc = (e * (1.0 / jnp.sum(e, axis=1, keepdims=True))).astype(BF16)
    cover = jnp.concatenate([_cover_matrix(nchunk, LANES).astype(BF16)] * n_seq, axis=0)
    oc_ref[...] = jnp.dot(pc, v_c, preferred_element_type=F32).reshape(n_seq, N_HEADS, LANES)
    imp_ref[...] = jnp.dot(pc, cover, preferred_element_type=F32).reshape(n_seq, N_HEADS, LANES)


def _cmp_sample(page_table, cache_t, qbd, pe_t, w1bd, w2bd):
    nb, n_pages = page_table.shape
    page = cache_t.shape[-1]
    n_seq = SAMPLE_GROUP
    const = lambda a: pl.BlockSpec(a.shape, lambda b, pt: (0,) * a.ndim)
    per_b = pl.BlockSpec((n_seq, N_HEADS, LANES), lambda b, pt: (b, 0, 0))
    grid_spec = pltpu.PrefetchScalarGridSpec(
        num_scalar_prefetch=1,
        grid=(nb // n_seq,),
        in_specs=[pl.BlockSpec(memory_space=pl.ANY), per_b, const(pe_t), const(w1bd), const(w2bd)],
        out_specs=(per_b, per_b),
        scratch_shapes=[pltpu.VMEM((2, n_seq * n_pages * page, LANES), F32),
                        pltpu.VMEM((2, n_seq * n_pages, KV_W, page), F32),
                        pltpu.SemaphoreType.DMA((2,))],
    )
    return pl.pallas_call(
        functools.partial(_cmp_sample_kernel, n_pages=n_pages, page=page, n_seq=n_seq),
        grid_spec=grid_spec,
        out_shape=(jax.ShapeDtypeStruct((nb, N_HEADS, LANES), F32),
                   jax.ShapeDtypeStruct((nb, N_HEADS, LANES), F32)),
        compiler_params=_cparams(("arbitrary",)),
        name="cmp_sample",
    )(page_table, cache_t, qbd, pe_t, w1bd, w2bd)


def _topk_sample_kernel(imp_ref, out_ref, *, t_pos, n_blocks):
    imp = imp_ref[0] + imp_ref[1] + imp_ref[2] + imp_ref[3]
    t = jnp.full((imp.shape[0], 1), t_pos, jnp.int32)
    out_ref[...] = _not_selected(_masked_importance(imp, t, n_blocks, 1), min(N_SEL, n_blocks), 1)


def _topk_sample(imp_g, *, t_pos, n_blocks):
    rows = imp_g.shape[1]
    return pl.pallas_call(
        functools.partial(_topk_sample_kernel, t_pos=t_pos, n_blocks=n_blocks),
        grid=(1,),
        in_specs=[pl.BlockSpec(imp_g.shape, lambda i: (0, 0, 0))],
        out_specs=pl.BlockSpec((rows, LANES), lambda i: (0, 0)),
        out_shape=jax.ShapeDtypeStruct((rows, LANES), F32),
        compiler_params=_cparams(("arbitrary",)),
        name="topk_sample",
    )(imp_g)


def _attn_sample_kernel(pt_ref, *refs, n_pages, page, n_seq):
    (cache_ref, win_ref, qbd_ref, ns_ref, snew_ref, wnew_ref, wcol_ref, oc_ref, gate_ref, o_ref, nwin_ref,
     buf_ref, sem_ref) = refs
    slot = _fetch_pages(pt_ref, cache_ref, buf_ref, sem_ref, n_pages, n_seq)
    pages = [buf_ref.at[slot, j] for j in range(n_seq * n_pages)]
    nrow = n_seq * N_HEADS
    qf = qbd_ref[...].reshape(nrow, LANES)
    qbd = qf.astype(BF16)
    ns = ns_ref[...].reshape(nrow, LANES)
    row_seq = lax.broadcasted_iota(jnp.int32, (nrow, 1), 0) // N_HEADS
    nt_ = (((1,), (1,)), ((), ()))

    def new_token(rows_ref):
        rows = jnp.concatenate([jnp.broadcast_to(rows_ref[g], (N_HEADS, KV_W)) for g in range(n_seq)], axis=0)
        k_new = rows[:, 0:LANES].astype(BF16).astype(F32)
        v_new = rows[:, LANES:2 * LANES].astype(BF16).astype(F32)
        return jnp.sum(qf * k_new, axis=1, keepdims=True), v_new

    lane = lax.broadcasted_iota(jnp.int32, (nrow, page), 1)
    per_page = page // SEL_BLOCK
    scores = []
    for i, page_ref in enumerate(pages):
        g, p = divmod(i, n_pages)
        s = jnp.dot(qbd, page_ref[0:LANES, :].astype(BF16), preferred_element_type=F32)
        flag = ns[:, p * per_page:p * per_page + 1]
        for j in range(1, per_page):
            flag = jnp.where(lane >= j * SEL_BLOCK, ns[:, p * per_page + j:p * per_page + j + 1], flag)
        scores.append(jnp.where((row_seq == g) & (flag < 0.5), s, NEG_FILL))
    nb_cache = n_pages * per_page
    s_new, v_new = new_token(snew_ref)
    s_new = jnp.where(ns[:, nb_cache:nb_cache + 1] > 0.5, NEG_FILL, s_new)
    s_max = scores[0]
    for s in scores[1:]:
        s_max = jnp.maximum(s_max, s)
    m = jnp.maximum(s_new, jnp.max(s_max, axis=1, keepdims=True))
    e_new = jnp.exp2(s_new - m)
    e_sum = jnp.zeros((nrow, page), F32)
    acc = e_new * v_new
    for s, page_ref in zip(scores, pages):
        e = jnp.exp2(s - m).astype(BF16)
        e_sum = e_sum + e.astype(F32)
        acc = acc + lax.dot_general(e, page_ref[LANES:2 * LANES, :].astype(BF16), nt_, preferred_element_type=F32)
    o_s = acc * (1.0 / (e_new + jnp.sum(e_sum, axis=1, keepdims=True)))

    wlen = win_ref.shape[2]
    first = max(wlen - WINDOW + 1, 0)
    in_win = lax.broadcasted_iota(jnp.int32, (nrow, wlen), 1) >= first
    scores_w = []
    for g in range(n_seq):
        s_w = jnp.dot(qbd, win_ref[g, 0:LANES, :].astype(BF16), preferred_element_type=F32)
        scores_w.append(jnp.where((row_seq == g) & in_win, s_w, NEG_FILL))
    sw_new, vw_new = new_token(wnew_ref)
    sw_max = scores_w[0]
    for s_w in scores_w[1:]:
        sw_max = jnp.maximum(sw_max, s_w)
    m_w = jnp.maximum(sw_new, jnp.max(sw_max, axis=1, keepdims=True))
    ew_new = jnp.exp2(sw_new - m_w)
    ew_sum = jnp.zeros((nrow, wlen), F32)
    acc_w = ew_new * vw_new
    for g in range(n_seq):
        e_w = jnp.exp2(scores_w[g] - m_w).astype(BF16)
        ew_sum = ew_sum + e_w.astype(F32)
        acc_w = acc_w + lax.dot_general(e_w, win_ref[g, LANES:2 * LANES, :].astype(BF16), nt_,
                                        preferred_element_type=F32)
    o_w = acc_w * (1.0 / (ew_new + jnp.sum(ew_sum, axis=1, keepdims=True)))

    gates = gate_ref[...].reshape(nrow, LANES)
    o = gates[:, 0:1] * oc_ref[...].reshape(nrow, LANES) + gates[:, 1:2] * o_s + gates[:, 2:3] * o_w
    o_ref[...] = o.reshape(n_seq, N_HEADS, LANES)

    last = lax.broadcasted_iota(jnp.int32, (KV_W, wlen), 1) == wlen - 1
    for g in range(n_seq):
        nwin_ref[g] = jnp.where(last, wcol_ref[g], pltpu.roll(win_ref[g], wlen - 1, 1))


def _attn_sample(page_table, cache_t, win_t, qbd, ns8, s_new, w_new, w_col, o_c, gates8):
    nb, n_pages = page_table.shape
    page = cache_t.shape[-1]
    n_seq = SAMPLE_GROUP
    per_b = lambda a: pl.BlockSpec((n_seq,) + a.shape[1:], lambda b, pt: (b,) + (0,) * (a.ndim - 1))
    grid_spec = pltpu.PrefetchScalarGridSpec(
        num_scalar_prefetch=1,
        grid=(nb // n_seq,),
        in_specs=[pl.BlockSpec(memory_space=pl.ANY)]
        + [per_b(a) for a in (win_t, qbd, ns8, s_new, w_new, w_col, o_c, gates8)],
        out_specs=(per_b(o_c), per_b(win_t)),
        scratch_shapes=[pltpu.VMEM((2, n_seq * n_pages, KV_W, page), F32),
                        pltpu.SemaphoreType.DMA((2,))],
    )
    return pl.pallas_call(
        functools.partial(_attn_sample_kernel, n_pages=n_pages, page=page, n_seq=n_seq),
        grid_spec=grid_spec,
        out_shape=(jax.ShapeDtypeStruct(o_c.shape, F32), jax.ShapeDtypeStruct(win_t.shape, F32)),
        compiler_params=_cparams(("arbitrary",)),
        name="attn_sample",
    )(page_table, cache_t, win_t, qbd, ns8, s_new, w_new, w_col, o_c, gates8)


def _prep_w_in(w_in):
    o = D_ATTN + 3 * KV_W
    ng = GQA * N_BRANCH
    gl = w_in[:, o:o + N_KV_HEADS * ng]
    rest = w_in[:, o + N_KV_HEADS * ng:]
    pad = jnp.zeros((w_in.shape[0], LANES - ng), w_in.dtype)
    return jnp.concatenate([w_in[:, :o], rest, gl[:, :ng], pad, gl[:, ng:], pad], axis=1).astype(BF16)


def _prep_compress(cmp_pe, cmp_w1, cmp_w2):
    r = CMP_BLOCK // CMP_STRIDE
    pe_t = jnp.tile(cmp_pe.reshape(2, r, CMP_STRIDE, 1, HEAD_DIM), (1, 1, 1, 1, N_KV_HEADS))
    w1 = cmp_w1.reshape(2, r, CMP_STRIDE, HEAD_DIM, CMP_HIDDEN)
    z1 = jnp.zeros_like(w1)
    top = jnp.concatenate([w1, z1], axis=-1)
    bot = jnp.concatenate([z1, w1], axis=-1)
    w1bd = jnp.concatenate([top, bot], axis=3)
    w1bd = w1bd.reshape(2, r, CMP_STRIDE * LANES, N_KV_HEADS * CMP_HIDDEN).astype(BF16)
    z2 = jnp.zeros_like(cmp_w2)
    w2bd = jnp.concatenate([jnp.concatenate([cmp_w2, z2], axis=-1),
                            jnp.concatenate([z2, cmp_w2], axis=-1)], axis=1).astype(BF16)
    return pe_t, w1bd, w2bd


def _kv_out(kv_t):
    n, _, rows = kv_t.shape
    return jnp.transpose(kv_t.reshape(n, 2, N_KV_HEADS, HEAD_DIM, rows), (0, 4, 1, 2, 3))[None]


def kernel(x_prompt, x_sample, cache_cmp_kv, cache_slc_kv, cache_win_kv, state_pool, page_table, norm_g, w_in,
           cmp_pe, cmp_w1, cmp_w2, pool_w, pool_scale, w_out, final_g):
    n, seq_len, d_model = x_prompt.shape
    nb = x_sample.shape[0]
    n_phys, page = cache_cmp_kv.shape[1], cache_cmp_kv.shape[2]
    n_pages = page_table.shape[1]
    past_len = n_pages * page
    wlen = cache_win_kv.shape[2]

    w_r = _prep_w_in(w_in[0])
    pe_t, w1bd, w2bd = _prep_compress(cmp_pe[0], cmp_w1[0], cmp_w2[0])
    pool_w_b = pool_w[0].astype(BF16)
    w_out_b = w_out[0].astype(BF16)
    fg = final_g.reshape(1, d_model)

    (qt, crm, ckv_t, skv_t, wkv_t, ks, vst, kw, vwt, gates_t, sza, u, szp) = _project(
        x_prompt.reshape(n * seq_len, d_model), norm_g, w_r, tm=512, seq_len=seq_len)
    r3 = lambda a: a.reshape(n, seq_len, a.shape[-1])
    kc, rhs_c = _compress_prompt(crm, n, pe_t, w1bd, w2bd)
    a3 = _attn_prompt(qt, gates_t, r3(sza), kc, rhs_c, ks, vst, kw, vwt)
    y_prompt = _out_prompt(a3, r3(u), r3(szp), x_prompt, pool_w_b, pool_scale, w_out_b, fg, tm=512)

    new_cmp_p = _kv_out(ckv_t)
    new_slc_p = _kv_out(skv_t)
    new_win_p = _kv_out(wkv_t[:, :, seq_len - min(WINDOW, seq_len):])
    new_pool_p = r3(u)[:, seq_len - POOL_STATE:][None]

    (qt_s, _, ckv_ts, skv_ts, wkv_ts, _, _, _, _, gates_ts, sza_s, u_s, szp_s) = _project(
        x_sample.reshape(nb, d_model), norm_g, w_r, tm=nb, seq_len=nb)
    q_s = qt_s.T
    q5 = q_s.reshape(nb, N_KV_HEADS, GQA, 1, HEAD_DIM)
    eye = jnp.eye(N_KV_HEADS, dtype=q_s.dtype).reshape(1, N_KV_HEADS, 1, N_KV_HEADS, 1)
    qbd = (q5 * eye).reshape(nb, N_HEADS, LANES).astype(F32)

    to_pages = lambda c: jnp.transpose(c[0], (0, 2, 3, 4, 1)).reshape(n_phys, KV_W, page)
    o_c, imp8 = _cmp_sample(page_table, to_pages(cache_cmp_kv), qbd, pe_t, w1bd, w2bd)
    imp_g = jnp.transpose(imp8.reshape(nb * N_KV_HEADS, GQA, LANES), (1, 0, 2))
    n_blocks = -(-(past_len + 1) // SEL_BLOCK)
    notsel = _topk_sample(imp_g, t_pos=past_len, n_blocks=n_blocks)
    ns8 = jnp.repeat(notsel.reshape(nb, N_KV_HEADS, 1, LANES), GQA, axis=2).reshape(nb, N_HEADS, LANES)

    gates8 = jnp.transpose(gates_ts[:, :GQA * N_BRANCH, :], (2, 0, 1)).reshape(nb, N_HEADS, N_BRANCH)
    gates8 = jnp.pad(gates8, ((0, 0), (0, 0), (0, LANES - N_BRANCH)))
    win_t = jnp.transpose(cache_win_kv[0], (0, 2, 3, 4, 1)).reshape(nb, KV_W, wlen)
    s_new = skv_ts[0].T.reshape(nb, 1, KV_W)
    w_new = wkv_ts[0].T.reshape(nb, 1, KV_W)
    w_col = w_new.reshape(nb, KV_W, 1)
    o8, nwin_t = _attn_sample(page_table, to_pages(cache_slc_kv), win_t, qbd, ns8, s_new, w_new, w_col, o_c, gates8)
    o5 = o8.reshape(nb, N_KV_HEADS, GQA, N_KV_HEADS, HEAD_DIM)
    o2 = jnp.stack([o5[:, k, :, k, :] for k in range(N_KV_HEADS)], axis=1).reshape(nb, D_ATTN)

    state_t = jnp.transpose(state_pool[0], (1, 0, 2))
    y_sample = _out_sample(o2, sza_s, u_s, state_t, szp_s, x_sample.reshape(nb, d_model),
                           pool_w_b, pool_scale, w_out_b, fg).reshape(nb, 1, d_model)

    kv_out_s = lambda t: jnp.transpose(t.reshape(2, N_KV_HEADS, HEAD_DIM, nb, 1), (3, 4, 0, 1, 2))[None]
    new_cmp_s = kv_out_s(ckv_ts)
    new_slc_s = kv_out_s(skv_ts)
    new_win_s = jnp.transpose(nwin_t.reshape(nb, 2, N_KV_HEADS, HEAD_DIM, wlen), (0, 4, 1, 2, 3))[None]
    new_pool_s = jnp.transpose(jnp.concatenate([state_t[1:], u_s[None]], axis=0), (1, 0, 2))[None]

    return (y_prompt, y_sample, new_cmp_p, new_slc_p, new_win_p, new_pool_p,
            new_cmp_s, new_slc_s, new_win_s, new_pool_s)
```

```python
import functools

import jax
import jax.numpy as jnp
from jax import lax
from jax.experimental import pallas as pl
from jax.experimental.pallas import tpu as pltpu

F32 = jnp.float32
BF16 = jnp.bfloat16

HEAD_DIM = 64
N_KV_HEADS = 2
GQA = 4
N_HEADS = N_KV_HEADS * GQA
D_ATTN = N_HEADS * HEAD_DIM
KV_W = 2 * N_KV_HEADS * HEAD_DIM
N_BRANCH = 3
D_POOL = 512
CMP_BLOCK = 32
CMP_STRIDE = 16
CMP_HIDDEN = 128
SEL_BLOCK = 64
N_SEL = 16
WINDOW = 512
Q_BLOCK = 256
POOL_WINDOWS = (2, 4, 8, 16)
POOL_GROUP_DIM = 128
POOL_STATE = 15
RMS_EPS = 1e-6
LANES = 128
NEG_BIAS = -(2.0 ** 30)
NEG_FILL = -1e30
GATE_COLS = 2 * LANES
GATE_ROWS = 16
BF16_SUBLANES = 16
V_ROWS = HEAD_DIM + BF16_SUBLANES
KEY_TILE = 512
FORCED_VALUE = 1e9
N_FORCED = 3
CHUNK_PITCH = 20
SAMPLE_GROUP = 4
SEL_CHAINS = 2
LOG2E = 1.4426950408889634
P_PAD = D_ATTN + 3 * KV_W + 3 * 512 + GATE_COLS
VMEM_LIMIT = 48 * 1024 * 1024


def _cparams(sem):
    return pltpu.CompilerParams(dimension_semantics=sem, vmem_limit_bytes=VMEM_LIMIT)


def _silu(z):
    return z * jax.nn.sigmoid(z)


def _proj_kernel(x_ref, g_ref, w_ref, qt_ref, crm_ref, ct_ref, st_ref, wt_ref, ks_ref, vst_ref, kw_ref, vwt_ref,
                 gatet_ref, sza_ref, u_ref, szp_ref, *, tm, seq_len):
    i = pl.program_id(0)
    x = x_ref[...]
    ms = jnp.mean(x * x, axis=-1, keepdims=True)
    xn = (x * lax.rsqrt(ms + RMS_EPS) * g_ref[...]).astype(BF16)

    def mm(c0, c1):
        return jnp.dot(xn, w_ref[:, c0:c1], preferred_element_type=F32)

    qt_ref[...] = (mm(0, D_ATTN) * (HEAD_DIM ** -0.5 * LOG2E)).T.astype(BF16)
    kv = mm(D_ATTN, D_ATTN + 3 * KV_W)
    crm_ref[0] = kv[:, 0:LANES]
    crm_ref[1] = kv[:, LANES:2 * LANES]
    kvt = kv.T
    ct_ref[...] = kvt[0:KV_W, :]
    st_ref[...] = kvt[KV_W:2 * KV_W, :]
    wt_ref[...] = kvt[2 * KV_W:3 * KV_W, :]

    lane = lax.broadcasted_iota(jnp.int32, (tm, LANES), 1)
    pos = (i * tm) % seq_len + lax.broadcasted_iota(jnp.int32, (tm, LANES), 0)
    onehot = jnp.where(lane - HEAD_DIM == (pos // SEL_BLOCK) % HEAD_DIM, NEG_BIAS, 0.0).astype(F32)
    ones_rows = jnp.where(lax.broadcasted_iota(jnp.int32, (V_ROWS - HEAD_DIM, tm), 0) == 0, 1.0, 0.0).astype(BF16)
    for kvh in range(N_KV_HEADS):
        for (base, k_dst, v_dst, k_pad) in ((KV_W, ks_ref, vst_ref, onehot), (2 * KV_W, kw_ref, vwt_ref, 0.0)):
            slab = kv[:, base:base + LANES]
            if kvh == 1:
                slab = pltpu.roll(slab, HEAD_DIM, 1)
            k_dst[kvh, :, :] = jnp.where(lane < HEAD_DIM, slab, k_pad).astype(BF16)
            v0 = base + LANES + kvh * HEAD_DIM
            v_dst[kvh, 0:HEAD_DIM, :] = kvt[v0:v0 + HEAD_DIM, :].astype(BF16)
            v_dst[kvh, HEAD_DIM:V_ROWS, :] = ones_rows

    c = D_ATTN + 3 * KV_W
    sza_ref[...] = _silu(mm(c, c + 512)).astype(BF16)
    u_ref[...] = mm(c + 512, c + 1024)
    szp_ref[...] = _silu(mm(c + 1024, c + 1536)).astype(BF16)
    gate_t = jax.nn.sigmoid(mm(c + 1536, c + 1536 + GATE_COLS)).T
    for kvh in range(N_KV_HEADS):
        gatet_ref[kvh, :, :] = gate_t[kvh * LANES:kvh * LANES + GATE_ROWS, :]


def _project(x2d, norm_g, w_r, *, tm, seq_len):
    rows = x2d.shape[0]
    nt = rows // tm
    tps = seq_len // tm
    row_blk = lambda w: pl.BlockSpec((tm, w), lambda i: (i, 0))
    kv_t = jax.ShapeDtypeStruct((rows // seq_len, KV_W, seq_len), F32)
    kv_t_blk = pl.BlockSpec((None, KV_W, tm), lambda i: (i // tps, 0, i % tps))
    k_rm = jax.ShapeDtypeStruct((N_KV_HEADS, rows, LANES), BF16)
    k_rm_blk = pl.BlockSpec((N_KV_HEADS, tm, LANES), lambda i: (0, i, 0))
    v_t = jax.ShapeDtypeStruct((N_KV_HEADS, V_ROWS, rows), BF16)
    v_t_blk = pl.BlockSpec((N_KV_HEADS, V_ROWS, tm), lambda i: (0, 0, i))
    out_shape = (
        jax.ShapeDtypeStruct((D_ATTN, rows), BF16),
        jax.ShapeDtypeStruct((2, rows, LANES), F32),
        kv_t, kv_t, kv_t,
        k_rm,
        v_t,
        k_rm,
        v_t,
        jax.ShapeDtypeStruct((N_KV_HEADS, GATE_ROWS, rows), F32),
        jax.ShapeDtypeStruct((rows, 512), BF16),
        jax.ShapeDtypeStruct((rows, 512), F32),
        jax.ShapeDtypeStruct((rows, 512), BF16),
    )
    out_specs = (
        pl.BlockSpec((D_ATTN, tm), lambda i: (0, i)),
        pl.BlockSpec((2, tm, LANES), lambda i: (0, i, 0)),
        kv_t_blk, kv_t_blk, kv_t_blk,
        k_rm_blk, v_t_blk, k_rm_blk, v_t_blk,
        pl.BlockSpec((N_KV_HEADS, GATE_ROWS, tm), lambda i: (0, 0, i)),
        row_blk(512), row_blk(512), row_blk(512),
    )
    return pl.pallas_call(
        functools.partial(_proj_kernel, tm=tm, seq_len=seq_len),
        grid=(nt,),
        in_specs=[row_blk(x2d.shape[1]),
                  pl.BlockSpec((1, x2d.shape[1]), lambda i: (0, 0)),
                  pl.BlockSpec(w_r.shape, lambda i: (0, 0))],
        out_specs=out_specs,
        out_shape=out_shape,
        compiler_params=_cparams(("arbitrary",)),
        name="proj",
    )(x2d, norm_g, w_r)


def _compress_hidden(load_rows, pe_ref, w1_ref, kv, m):
    xs = [load_rows(s) for s in range(CMP_STRIDE)]
    hs = []
    for sub in range(CMP_BLOCK // CMP_STRIDE):
        lhs = jnp.concatenate([(xs[s] + pe_ref[kv, sub, s]).astype(BF16) for s in range(CMP_STRIDE)], axis=1)
        hs.append(jnp.dot(lhs, w1_ref[kv, sub], preferred_element_type=F32))
    return hs[0] + pltpu.roll(hs[1], m - 1, 0)


def _cover_matrix(nc_pad, ns_pad):
    c0 = lax.broadcasted_iota(jnp.int32, (nc_pad, ns_pad), 0) * CMP_STRIDE
    s0 = lax.broadcasted_iota(jnp.int32, (nc_pad, ns_pad), 1) * SEL_BLOCK
    return jnp.where((c0 < s0 + SEL_BLOCK) & (c0 + CMP_BLOCK > s0), 1.0, 0.0).astype(F32)


def _compress_prompt_kernel(c_ref, pe_ref, w1_ref, w2_ref, kc_ref, rhs_ref, *, nchunk):
    ones_rows = jnp.where(lax.broadcasted_iota(jnp.int32, (HEAD_DIM, nchunk), 0) == 0, 1.0, 0.0).astype(BF16)
    s0 = lax.broadcasted_iota(jnp.int32, (LANES, nchunk), 0) * SEL_BLOCK
    c0 = lax.broadcasted_iota(jnp.int32, (LANES, nchunk), 1) * CMP_STRIDE
    cover_t = jnp.where((c0 < s0 + SEL_BLOCK) & (c0 + CMP_BLOCK > s0), 1.0, 0.0).astype(BF16)
    for kv in range(2):
        load = lambda s: c_ref[kv, pl.ds(s, nchunk, stride=CMP_STRIDE), :]
        h = _compress_hidden(load, pe_ref, w1_ref, kv, nchunk)
        full = jnp.dot(_silu(h).astype(BF16), w2_ref[kv], preferred_element_type=F32)
        if kv == 0:
            kc_ref[0, :, :] = full[:, 0:HEAD_DIM].astype(BF16)
            kc_ref[1, :, :] = full[:, HEAD_DIM:2 * HEAD_DIM].astype(BF16)
        else:
            full_t = full.T
            for kvh in range(N_KV_HEADS):
                rhs_ref[kvh, 0:HEAD_DIM, :] = full_t[kvh * HEAD_DIM:(kvh + 1) * HEAD_DIM, :].astype(BF16)
                rhs_ref[kvh, HEAD_DIM:LANES, :] = ones_rows
                rhs_ref[kvh, LANES:2 * LANES, :] = cover_t


def _compress_prompt(crm, n, pe_t, w1bd, w2bd):
    seq_len = crm.shape[1] // n
    nchunk = seq_len // CMP_STRIDE
    return pl.pallas_call(
        functools.partial(_compress_prompt_kernel, nchunk=nchunk),
        grid=(n,),
        in_specs=[pl.BlockSpec((2, seq_len, LANES), lambda b: (0, b, 0)),
                  pl.BlockSpec(pe_t.shape, lambda b: (0, 0, 0, 0, 0)),
                  pl.BlockSpec(w1bd.shape, lambda b: (0, 0, 0, 0)),
                  pl.BlockSpec(w2bd.shape, lambda b: (0, 0, 0))],
        out_specs=(pl.BlockSpec((None, N_KV_HEADS, nchunk, HEAD_DIM), lambda b: (b, 0, 0, 0)),
                   pl.BlockSpec((None, N_KV_HEADS, 2 * LANES, nchunk), lambda b: (b, 0, 0, 0))),
        out_shape=(jax.ShapeDtypeStruct((n, N_KV_HEADS, nchunk, HEAD_DIM), BF16),
                   jax.ShapeDtypeStruct((n, N_KV_HEADS, 2 * LANES, nchunk), BF16)),
        compiler_params=_cparams(("arbitrary",)),
        name="compress_prompt",
    )(crm, pe_t, w1bd, w2bd)


def _not_selected(val, n_top, axis):
    blk = lax.broadcasted_iota(jnp.int32, val.shape, axis).astype(F32)
    forced = val >= FORCED_VALUE
    notsel = jnp.where(forced, 0.0, 1.0).astype(F32)
    val = jnp.where(forced, -3e38, val)
    for _ in range(max(n_top - N_FORCED, 0)):
        m = jnp.max(val, axis=axis, keepdims=True)
        idx = jnp.min(jnp.where(val == m, blk, float(LANES)), axis=axis, keepdims=True)
        pick = blk == idx
        notsel = jnp.where(pick, 0.0, notsel)
        val = jnp.where(pick, -3e38, val)
    return notsel


def _masked_importance(imp, t, n_blocks, axis):
    blk = lax.broadcasted_iota(jnp.int32, imp.shape, axis)
    cur = t // SEL_BLOCK
    forced = (blk == 0) | (blk == cur) | (blk == cur - 1)
    val = jnp.where(forced, FORCED_VALUE, jnp.where(blk * SEL_BLOCK <= t, imp, -1e9))
    return jnp.where(blk < n_blocks, val, -2e9)


def _attn_prompt_kernel(qt_ref, gatet_ref, sza_ref, kc_ref, rhs_ref, ks_ref, vst_ref, kw_ref, vwt_ref, out_ref,
                        qsel_ref, sbuf_ref, m_ref, acc_ref, mw_ref, accw_ref, oc_ref,
                        *, n_sel_blocks, seq_len):
    qb = pl.program_id(2)
    qs = qb * Q_BLOCK
    hq = GQA * Q_BLOCK
    t_row = qs + lax.broadcasted_iota(jnp.int32, (1, Q_BLOCK), 1)
    t_all = jnp.concatenate([t_row] * GQA, axis=1)
    qt = qt_ref[...]
    qt_all = jnp.concatenate([qt[g * HEAD_DIM:(g + 1) * HEAD_DIM, :] for g in range(GQA)], axis=1)
    zeros_lo = jnp.zeros((HEAD_DIM, hq), BF16)
    qw = jnp.concatenate([qt_all, zeros_lo], axis=0)

    tile = min(KEY_TILE, seq_len)
    n_full = qs // tile
    lo_tiles = HEAD_DIM * SEL_BLOCK // tile
    wwidth = min(WINDOW, seq_len)
    causal = lambda kpos: kpos <= t_all
    band = lambda kpos: (kpos > t_all - WINDOW) & (kpos <= t_all)
    diag_w = lambda kpos: (kpos <= t_all) & (qs >= wwidth)

    def flash_steps(steps):
        staged = []
        for (q, k_ref, vt_ref, start, width, mask_fn, mref, aref, c) in steps:
            start = pl.multiple_of(start, LANES)
            s = jnp.dot(k_ref[pl.ds(start, width), :], q, preferred_element_type=F32)
            if mask_fn is not None:
                s = jnp.where(mask_fn(start + lax.broadcasted_iota(jnp.int32, (width, hq), 0)), s, NEG_FILL)
            staged.append((s, vt_ref[:, pl.ds(start, width)]))
        softmaxed = []
        for (s, vt), step in zip(staged, steps):
            mref, c = step[6], step[8]
            m_old = mref[c]
            m_new = jnp.maximum(m_old, jnp.max(s, axis=0, keepdims=True))
            softmaxed.append((jnp.exp2(s - m_new).astype(BF16), jnp.exp2(m_old - m_new), m_new, vt))
        for (p, alpha, m_new, vt), step in zip(softmaxed, steps):
            mref, aref, c = step[6], step[7], step[8]
            aref[c] = alpha * aref[c] + jnp.dot(vt, p, preferred_element_type=F32)
            mref[c] = m_new

    m_ref[...] = jnp.full(m_ref.shape, NEG_FILL, F32)
    acc_ref[...] = jnp.zeros(acc_ref.shape, F32)
    mw_ref[...] = jnp.full(mw_ref.shape, NEG_FILL, F32)
    accw_ref[...] = jnp.zeros(accw_ref.shape, F32)

    nc = kc_ref.shape[0]
    c_last = lax.broadcasted_iota(jnp.int32, (nc, hq), 0) * CMP_STRIDE + (CMP_BLOCK - 1)
    ok_c = c_last <= t_all
    s = jnp.dot(kc_ref[...], qt_all, preferred_element_type=F32)
    flash_steps([(qw, kw_ref, vwt_ref, qs, Q_BLOCK, diag_w, mw_ref, accw_ref, 1)])
    s = jnp.where(ok_c, s, NEG_FILL)
    e = jnp.exp2(s - jnp.max(s, axis=0, keepdims=True))
    r = jnp.dot(rhs_ref[...], e.astype(BF16), preferred_element_type=F32)
    inv = jnp.where(t_all >= CMP_BLOCK - 1, 1.0 / r[HEAD_DIM:HEAD_DIM + 1, :], 0.0)
    qw_late = jnp.where(inv > -1.0, qw, jnp.zeros_like(qw))
    flash_steps([(qw_late, kw_ref, vwt_ref, jnp.maximum(qs - wwidth, 0), wwidth, band, mw_ref, accw_ref, 0)])
    oc_ref[...] = r[0:HEAD_DIM, :] * inv
    imp_all = r[LANES:2 * LANES, :] * inv
    imp = imp_all[:, 0:Q_BLOCK]
    for g in range(1, GQA):
        imp = imp + imp_all[:, g * Q_BLOCK:(g + 1) * Q_BLOCK]

    notsel = _not_selected(_masked_importance(imp, t_row, n_sel_blocks, 0), min(N_SEL, n_sel_blocks), 0)
    notsel = notsel.astype(BF16)
    q_lo = jnp.concatenate([qt_all, jnp.concatenate([notsel[0:HEAD_DIM, :]] * GQA, axis=1)], axis=0)
    q_hi = jnp.concatenate([qt_all, jnp.concatenate([notsel[HEAD_DIM:2 * HEAD_DIM, :]] * GQA, axis=1)], axis=0)
    qsel_ref[0] = q_lo
    qsel_ref[1] = q_hi

    def scores(b, kt, masked):
        start = pl.multiple_of(kt * tile, LANES)
        q = qsel_ref[jnp.where(kt >= lo_tiles, 1, 0)]
        s = jnp.dot(ks_ref[pl.ds(start, tile), :], q, preferred_element_type=F32)
        if masked:
            s = jnp.where(causal(start + lax.broadcasted_iota(jnp.int32, (tile, hq), 0)), s, NEG_FILL)
        sbuf_ref[b] = s

    def consume(b, kt):
        start = pl.multiple_of(kt * tile, LANES)
        s = sbuf_ref[b]
        m_old = m_ref[b]
        m_new = jnp.maximum(m_old, jnp.max(s, axis=0, keepdims=True))
        p = jnp.exp2(s - m_new).astype(BF16)
        acc_ref[b] = jnp.exp2(m_old - m_new) * acc_ref[b] + jnp.dot(vst_ref[:, pl.ds(start, tile)], p,
                                                                  preferred_element_type=F32)
        m_ref[b] = m_new

    @pl.when(n_full == 0)
    def _():
        scores(0, 0, True)

    @pl.when(n_full > 0)
    def _():
        scores(0, 0, False)

    def pair_body(j, carry):
        scores(1, 2 * j + 1, False)
        consume(0, 2 * j)
        scores(0, 2 * j + 2, False)
        consume(1, 2 * j + 1)
        return carry

    lax.fori_loop(0, jnp.maximum(n_full - 1, 0) // 2, pair_body, 0)

    @pl.when((n_full > 0) & (n_full % 2 == 0))
    def _():
        scores(1, n_full - 1, False)
        consume(0, n_full - 2)
        scores(0, n_full, True)
        consume(1, n_full - 1)
        consume(0, n_full)

    @pl.when(n_full % 2 == 1)
    def _():
        scores(1, n_full, True)
        consume(0, n_full - 1)
        consume(1, n_full)

    @pl.when(n_full == 0)
    def _():
        consume(0, 0)

    def merged(mref, aref, n_chain):
        m = mref[0]
        for c in range(1, n_chain):
            m = jnp.maximum(m, mref[c])
        a = jnp.exp2(mref[0] - m) * aref[0]
        for c in range(1, n_chain):
            a = a + jnp.exp2(mref[c] - m) * aref[c]
        return a[0:HEAD_DIM, :] * (1.0 / a[HEAD_DIM:HEAD_DIM + 1, :])

    gates = gatet_ref[...]
    sza = sza_ref[...]
    o_s = merged(m_ref, acc_ref, SEL_CHAINS)
    o_w = merged(mw_ref, accw_ref, 2)
    o_c = oc_ref[...]
    outs = []
    for g in range(GQA):
        cols = slice(g * Q_BLOCK, (g + 1) * Q_BLOCK)
        gc = gates[g * N_BRANCH + 0:g * N_BRANCH + 1, :]
        gs = gates[g * N_BRANCH + 1:g * N_BRANCH + 2, :]
        gw = gates[g * N_BRANCH + 2:g * N_BRANCH + 3, :]
        outs.append((gc * o_c[:, cols] + gs * o_s[:, cols] + gw * o_w[:, cols]).T)
    out_ref[...] = (jnp.concatenate(outs, axis=1) * sza).astype(BF16)


def _attn_prompt(qt, gates_t, sza3, kc, rhs_c, ks, vst, kw, vwt):
    n, seq_len, _ = sza3.shape
    nqb = seq_len // Q_BLOCK
    nchunk = kc.shape[2]
    hw = GQA * HEAD_DIM
    k_blk = pl.BlockSpec((None, seq_len, LANES), lambda b, k, i: (k, b, 0))
    vt_blk = pl.BlockSpec((None, V_ROWS, seq_len), lambda b, k, i: (k, 0, b))
    return pl.pallas_call(
        functools.partial(_attn_prompt_kernel, n_sel_blocks=-(-seq_len // SEL_BLOCK), seq_len=seq_len),
        grid=(n, N_KV_HEADS, nqb),
        in_specs=[
            pl.BlockSpec((hw, Q_BLOCK), lambda b, k, i: (k, b * nqb + i)),
            pl.BlockSpec((None, GATE_ROWS, Q_BLOCK), lambda b, k, i: (k, 0, b * nqb + i)),
            pl.BlockSpec((None, Q_BLOCK, hw), lambda b, k, i: (b, i, k)),
            pl.BlockSpec((None, None, nchunk, HEAD_DIM), lambda b, k, i: (b, k, 0, 0)),
            pl.BlockSpec((None, None, 2 * LANES, nchunk), lambda b, k, i: (b, k, 0, 0)),
            k_blk, vt_blk, k_blk, vt_blk,
        ],
        out_specs=pl.BlockSpec((None, Q_BLOCK, hw), lambda b, k, i: (b, i, k)),
        out_shape=jax.ShapeDtypeStruct((n, seq_len, D_ATTN), BF16),
        scratch_shapes=[
            pltpu.VMEM((2, 2 * HEAD_DIM, GQA * Q_BLOCK), BF16),
            pltpu.VMEM((SEL_CHAINS, min(KEY_TILE, seq_len), GQA * Q_BLOCK), F32),
            pltpu.VMEM((SEL_CHAINS, 1, GQA * Q_BLOCK), F32),
            pltpu.VMEM((SEL_CHAINS, V_ROWS, GQA * Q_BLOCK), F32),
            pltpu.VMEM((2, 1, GQA * Q_BLOCK), F32),
            pltpu.VMEM((2, V_ROWS, GQA * Q_BLOCK), F32),
            pltpu.VMEM((HEAD_DIM, GQA * Q_BLOCK), F32),
        ],
        compiler_params=_cparams(("arbitrary", "arbitrary", "arbitrary")),
        name="attn_prompt",
    )(qt, gates_t, sza3, kc, rhs_c, ks, vst, kw, vwt)


def _pool_out(d, pw_ref, ps_ref, szp):
    ys = [jnp.dot(d[:, g * POOL_GROUP_DIM:(g + 1) * POOL_GROUP_DIM].astype(BF16), pw_ref[g],
                  preferred_element_type=F32) for g in range(len(POOL_WINDOWS))]
    return jnp.concatenate(ys, axis=1) * ps_ref[...] * szp


def _finish(x, a_bf16, b, wo_ref, fg_ref):
    mix = jnp.concatenate([a_bf16, b.astype(BF16)], axis=1)
    y = x + jnp.dot(mix, wo_ref[...], preferred_element_type=F32)
    ms = jnp.mean(y * y, axis=-1, keepdims=True)
    return y * lax.rsqrt(ms + RMS_EPS) * fg_ref[...]


def _out_prompt_kernel(a_ref, u_ref, halo_ref, szp_ref, x_ref, pw_ref, ps_ref, wo_ref, fg_ref, y_ref, ext_ref,
                       *, tm, halo):
    i = pl.program_id(1)
    u = u_ref[...]
    ext_ref[0:halo, :] = jnp.where(i > 0, halo_ref[...], 0.0)
    ext_ref[halo:halo + tm, :] = u
    pos = i * tm + lax.broadcasted_iota(jnp.int32, (tm, POOL_GROUP_DIM), 0)
    ds = []
    for g, w in enumerate(POOL_WINDOWS):
        c0 = g * POOL_GROUP_DIM
        acc = u[:, c0:c0 + POOL_GROUP_DIM]
        for k in range(1, w):
            acc = acc + ext_ref[halo - k:halo - k + tm, c0:c0 + POOL_GROUP_DIM]
        cnt = jnp.minimum(pos + 1, w).astype(F32)
        ds.append(acc / cnt - u[:, c0:c0 + POOL_GROUP_DIM])
    b = _pool_out(jnp.concatenate(ds, axis=1), pw_ref, ps_ref, szp_ref[...])
    y_ref[...] = _finish(x_ref[...], a_ref[...], b, wo_ref, fg_ref)


def _out_prompt(a3, u3, szp3, x3, pool_w, pool_scale, w_out, final_g, *, tm):
    n, seq_len, d_model = x3.shape
    halo = 16
    nt = seq_len // tm
    blk = lambda w: pl.BlockSpec((None, tm, w), lambda b, i: (b, i, 0))
    const = lambda a: pl.BlockSpec(a.shape, lambda b, i: (0,) * a.ndim)
    return pl.pallas_call(
        functools.partial(_out_prompt_kernel, tm=tm, halo=halo),
        grid=(n, nt),
        in_specs=[blk(D_ATTN), blk(D_POOL),
                  pl.BlockSpec((None, halo, D_POOL), lambda b, i: (b, jnp.maximum(i * (tm // halo) - 1, 0), 0)),
                  blk(D_POOL), blk(d_model),
                  const(pool_w), const(pool_scale), const(w_out), const(final_g)],
        out_specs=blk(d_model),
        out_shape=jax.ShapeDtypeStruct((n, seq_len, d_model), F32),
        scratch_shapes=[pltpu.VMEM((tm + halo, D_POOL), F32)],
        compiler_params=_cparams(("arbitrary", "arbitrary")),
        name="out_prompt",
    )(a3, u3, u3, szp3, x3, pool_w, pool_scale, w_out, final_g)


def _out_sample_kernel(o_ref, sza_ref, u_ref, st_ref, szp_ref, x_ref, pw_ref, ps_ref, wo_ref, fg_ref, y_ref):
    u = u_ref[...]
    ds = []
    for g, w in enumerate(POOL_WINDOWS):
        c0 = g * POOL_GROUP_DIM
        acc = u[:, c0:c0 + POOL_GROUP_DIM]
        for k in range(1, w):
            acc = acc + st_ref[POOL_STATE - k, :, c0:c0 + POOL_GROUP_DIM]
        ds.append(acc / float(w) - u[:, c0:c0 + POOL_GROUP_DIM])
    b = _pool_out(jnp.concatenate(ds, axis=1), pw_ref, ps_ref, szp_ref[...])
    a = (o_ref[...] * sza_ref[...]).astype(BF16)
    y_ref[...] = _finish(x_ref[...], a, b, wo_ref, fg_ref)


def _out_sample(o2, sza, u, state_t, szp, x2, pool_w, pool_scale, w_out, final_g):
    args = (o2, sza, u, state_t, szp, x2, pool_w, pool_scale, w_out, final_g)
    full = lambda a: pl.BlockSpec(a.shape, lambda i: (0,) * a.ndim)
    return pl.pallas_call(
        _out_sample_kernel,
        grid=(1,),
        in_specs=[full(a) for a in args],
        out_specs=full(x2),
        out_shape=jax.ShapeDtypeStruct(x2.shape, F32),
        compiler_params=_cparams(("arbitrary",)),
        name="out_sample",
    )(*args)


def _page_copy(pt_ref, cache_ref, buf_ref, sem_ref, step, slot, j, n_pages, n_seq):
    g, p = divmod(j, n_pages)
    return pltpu.make_async_copy(cache_ref.at[pt_ref[step * n_seq + g, p]], buf_ref.at[slot, j], sem_ref.at[slot])


def _fetch_pages(pt_ref, cache_ref, buf_ref, sem_ref, n_pages, n_seq):
    i = pl.program_id(0)
    slot = i % 2
    n_copy = n_seq * n_pages

    @pl.when(i == 0)
    def _():
        for j in range(n_copy):
            _page_copy(pt_ref, cache_ref, buf_ref, sem_ref, 0, 0, j, n_pages, n_seq).start()

    @pl.when(i + 1 < pl.num_programs(0))
    def _():
        for j in range(n_copy):
            _page_copy(pt_ref, cache_ref, buf_ref, sem_ref, i + 1, 1 - slot, j, n_pages, n_seq).start()

    for j in range(n_copy):
        _page_copy(pt_ref, cache_ref, buf_ref, sem_ref, i, slot, j, n_pages, n_seq).wait()
    return slot


def _cmp_sample_kernel(pt_ref, cache_ref, qbd_ref, pe_ref, w1_ref, w2_ref, oc_ref, imp_ref, rows_ref, buf_ref,
                       sem_ref, *, n_pages, page, n_seq):
    slot = _fetch_pages(pt_ref, cache_ref, buf_ref, sem_ref, n_pages, n_seq)
    pages = [buf_ref.at[slot, j] for j in range(n_seq * n_pages)]
    nchunk = n_pages * page // CMP_STRIDE
    fulls = []
    m = n_seq * nchunk
    cpp = page // CMP_STRIDE
    for kv in range(2):
        for i, page_ref in enumerate(pages):
            half = page_ref[kv * LANES:(kv + 1) * LANES, :].astype(BF16)
            rows = half.T.astype(F32)
            for c in range(cpp):
                dst = (i * cpp + c) * CHUNK_PITCH
                rows_ref[kv, dst:dst + CMP_STRIDE, :] = rows[c * CMP_STRIDE:(c + 1) * CMP_STRIDE, :]
        load = lambda s: rows_ref[kv, pl.ds(s, m, stride=CHUNK_PITCH), :]
        h = _compress_hidden(load, pe_ref, w1_ref, kv, m)
        fulls.append(jnp.dot(_silu(h).astype(BF16), w2_ref[kv], preferred_element_type=F32).astype(BF16))
    k_c, v_c = fulls
    nrow = n_seq * N_HEADS
    qbd = qbd_ref[...].reshape(nrow, LANES).astype(BF16)
    s = lax.dot_general(qbd, k_c, (((1,), (1,)), ((), ())), preferred_element_type=F32)
    col = lax.broadcasted_iota(jnp.int32, s.shape, 1)
    own = col // nchunk == lax.broadcasted_iota(jnp.int32, s.shape, 0) // N_HEADS
    ok = own & (col % nchunk < nchunk - 1)
    s = jnp.where(ok, s, NEG_FILL)
    e = jnp.where(ok, jnp.exp2(s - jnp.max(s, axis=1, keepdims=True)), 0.0)
    pc = (e * (1.0 / jnp.sum(e, axis=1, keepdims=True))).astype(BF16)
    cover = jnp.concatenate([_cover_matrix(nchunk, LANES).astype(BF16)] * n_seq, axis=0)
    oc_ref[...] = jnp.dot(pc, v_c, preferred_element_type=F32).reshape(n_seq, N_HEADS, LANES)
    imp_ref[...] = jnp.dot(pc, cover, preferred_element_type=F32).reshape(n_seq, N_HEADS, LANES)


def _cmp_sample(page_table, cache_t, qbd, pe_t, w1bd, w2bd):
    nb, n_pages = page_table.shape
    page = cache_t.shape[-1]
    n_seq = SAMPLE_GROUP
    const = lambda a: pl.BlockSpec(a.shape, lambda b, pt: (0,) * a.ndim)
    per_b = pl.BlockSpec((n_seq, N_HEADS, LANES), lambda b, pt: (b, 0, 0))
    grid_spec = pltpu.PrefetchScalarGridSpec(
        num_scalar_prefetch=1,
        grid=(nb // n_seq,),
        in_specs=[pl.BlockSpec(memory_space=pl.ANY), per_b, const(pe_t), const(w1bd), const(w2bd)],
        out_specs=(per_b, per_b),
        scratch_shapes=[pltpu.VMEM((2, n_seq * n_pages * page // CMP_STRIDE * CHUNK_PITCH, LANES), F32),
                        pltpu.VMEM((2, n_seq * n_pages, KV_W, page), F32),
                        pltpu.SemaphoreType.DMA((2,))],
    )
    return pl.pallas_call(
        functools.partial(_cmp_sample_kernel, n_pages=n_pages, page=page, n_seq=n_seq),
        grid_spec=grid_spec,
        out_shape=(jax.ShapeDtypeStruct((nb, N_HEADS, LANES), F32),
                   jax.ShapeDtypeStruct((nb, N_HEADS, LANES), F32)),
        compiler_params=_cparams(("arbitrary",)),
        name="cmp_sample",
    )(page_table, cache_t, qbd, pe_t, w1bd, w2bd)


def _topk_sample_kernel(imp_ref, out_ref, *, t_pos, n_blocks):
    imp = imp_ref[0] + imp_ref[1] + imp_ref[2] + imp_ref[3]
    t = jnp.full((imp.shape[0], 1), t_pos, jnp.int32)
    out_ref[...] = _not_selected(_masked_importance(imp, t, n_blocks, 1), min(N_SEL, n_blocks), 1)


def _topk_sample(imp_g, *, t_pos, n_blocks):
    rows = imp_g.shape[1]
    return pl.pallas_call(
        functools.partial(_topk_sample_kernel, t_pos=t_pos, n_blocks=n_blocks),
        grid=(1,),
        in_specs=[pl.BlockSpec(imp_g.shape, lambda i: (0, 0, 0))],
        out_specs=pl.BlockSpec((rows, LANES), lambda i: (0, 0)),
        out_shape=jax.ShapeDtypeStruct((rows, LANES), F32),
        compiler_params=_cparams(("arbitrary",)),
        name="topk_sample",
    )(imp_g)


def _attn_sample_kernel(pt_ref, *refs, n_pages, page, n_seq):
    (cache_ref, win_ref, qbd_ref, ns_ref, snew_ref, wnew_ref, wcol_ref, oc_ref, gate_ref, o_ref, nwin_ref,
     buf_ref, sem_ref) = refs
    slot = _fetch_pages(pt_ref, cache_ref, buf_ref, sem_ref, n_pages, n_seq)
    pages = [buf_ref.at[slot, j] for j in range(n_seq * n_pages)]
    nrow = n_seq * N_HEADS
    qf = qbd_ref[...].reshape(nrow, LANES)
    qbd = qf.astype(BF16)
    ns = ns_ref[...].reshape(nrow, LANES)
    row_seq = lax.broadcasted_iota(jnp.int32, (nrow, 1), 0) // N_HEADS
    nt_ = (((1,), (1,)), ((), ()))

    def new_token(rows_ref):
        rows = jnp.concatenate([jnp.broadcast_to(rows_ref[g], (N_HEADS, KV_W)) for g in range(n_seq)], axis=0)
        k_new = rows[:, 0:LANES].astype(BF16).astype(F32)
        v_new = rows[:, LANES:2 * LANES].astype(BF16).astype(F32)
        return jnp.sum(qf * k_new, axis=1, keepdims=True), v_new

    lane = lax.broadcasted_iota(jnp.int32, (nrow, page), 1)
    per_page = page // SEL_BLOCK
    scores = []
    for i, page_ref in enumerate(pages):
        g, p = divmod(i, n_pages)
        s = jnp.dot(qbd, page_ref[0:LANES, :].astype(BF16), preferred_element_type=F32)
        flag = ns[:, p * per_page:p * per_page + 1]
        for j in range(1, per_page):
            flag = jnp.where(lane >= j * SEL_BLOCK, ns[:, p * per_page + j:p * per_page + j + 1], flag)
        scores.append(jnp.where((row_seq == g) & (flag < 0.5), s, NEG_FILL))
    nb_cache = n_pages * per_page
    s_new, v_new = new_token(snew_ref)
    s_new = jnp.where(ns[:, nb_cache:nb_cache + 1] > 0.5, NEG_FILL, s_new)
    s_max = scores[0]
    for s in scores[1:]:
        s_max = jnp.maximum(s_max, s)
    m = jnp.maximum(s_new, jnp.max(s_max, axis=1, keepdims=True))
    e_new = jnp.exp2(s_new - m)
    e_sum = jnp.zeros((nrow, page), F32)
    acc = e_new * v_new
    for s, page_ref in zip(scores, pages):
        e = jnp.exp2(s - m).astype(BF16)
        e_sum = e_sum + e.astype(F32)
        acc = acc + lax.dot_general(e, page_ref[LANES:2 * LANES, :].astype(BF16), nt_, preferred_element_type=F32)
    o_s = acc * (1.0 / (e_new + jnp.sum(e_sum, axis=1, keepdims=True)))

    wlen = win_ref.shape[2]
    first = max(wlen - WINDOW + 1, 0)
    in_win = lax.broadcasted_iota(jnp.int32, (nrow, wlen), 1) >= first
    scores_w = []
    for g in range(n_seq):
        s_w = jnp.dot(qbd, win_ref[g, 0:LANES, :].astype(BF16), preferred_element_type=F32)
        scores_w.append(jnp.where((row_seq == g) & in_win, s_w, NEG_FILL))
    sw_new, vw_new = new_token(wnew_ref)
    sw_max = scores_w[0]
    for s_w in scores_w[1:]:
        sw_max = jnp.maximum(sw_max, s_w)
    m_w = jnp.maximum(sw_new, jnp.max(sw_max, axis=1, keepdims=True))
    ew_new = jnp.exp2(sw_new - m_w)
    ew_sum = jnp.zeros((nrow, wlen), F32)
    acc_w = ew_new * vw_new
    for g in range(n_seq):
        e_w = jnp.exp2(scores_w[g] - m_w).astype(BF16)
        ew_sum = ew_sum + e_w.astype(F32)
        acc_w = acc_w + lax.dot_general(e_w, win_ref[g, LANES:2 * LANES, :].astype(BF16), nt_,
                                        preferred_element_type=F32)
    o_w = acc_w * (1.0 / (ew_new + jnp.sum(ew_sum, axis=1, keepdims=True)))

    gates = gate_ref[...].reshape(nrow, LANES)
    o = gates[:, 0:1] * oc_ref[...].reshape(nrow, LANES) + gates[:, 1:2] * o_s + gates[:, 2:3] * o_w
    o_ref[...] = o.reshape(n_seq, N_HEADS, LANES)

    last = lax.broadcasted_iota(jnp.int32, (KV_W, wlen), 1) == wlen - 1
    for g in range(n_seq):
        nwin_ref[g] = jnp.where(last, wcol_ref[g], pltpu.roll(win_ref[g], wlen - 1, 1))


def _attn_sample(page_table, cache_t, win_t, qbd, ns8, s_new, w_new, w_col, o_c, gates8):
    nb, n_pages = page_table.shape
    page = cache_t.shape[-1]
    n_seq = SAMPLE_GROUP
    per_b = lambda a: pl.BlockSpec((n_seq,) + a.shape[1:], lambda b, pt: (b,) + (0,) * (a.ndim - 1))
    grid_spec = pltpu.PrefetchScalarGridSpec(
        num_scalar_prefetch=1,
        grid=(nb // n_seq,),
        in_specs=[pl.BlockSpec(memory_space=pl.ANY)]
        + [per_b(a) for a in (win_t, qbd, ns8, s_new, w_new, w_col, o_c, gates8)],
        out_specs=(per_b(o_c), per_b(win_t)),
        scratch_shapes=[pltpu.VMEM((2, n_seq * n_pages, KV_W, page), F32),
                        pltpu.SemaphoreType.DMA((2,))],
    )
    return pl.pallas_call(
        functools.partial(_attn_sample_kernel, n_pages=n_pages, page=page, n_seq=n_seq),
        grid_spec=grid_spec,
        out_shape=(jax.ShapeDtypeStruct(o_c.shape, F32), jax.ShapeDtypeStruct(win_t.shape, F32)),
        compiler_params=_cparams(("arbitrary",)),
        name="attn_sample",
    )(page_table, cache_t, win_t, qbd, ns8, s_new, w_new, w_col, o_c, gates8)


def _prep_w_in(w_in):
    o = D_ATTN + 3 * KV_W
    ng = GQA * N_BRANCH
    gl = w_in[:, o:o + N_KV_HEADS * ng]
    rest = w_in[:, o + N_KV_HEADS * ng:]
    pad = jnp.zeros((w_in.shape[0], LANES - ng), w_in.dtype)
    return jnp.concatenate([w_in[:, :o], rest, gl[:, :ng], pad, gl[:, ng:], pad], axis=1).astype(BF16)


def _prep_compress(cmp_pe, cmp_w1, cmp_w2):
    r = CMP_BLOCK // CMP_STRIDE
    pe_t = jnp.tile(cmp_pe.reshape(2, r, CMP_STRIDE, 1, HEAD_DIM), (1, 1, 1, 1, N_KV_HEADS))
    w1 = cmp_w1.reshape(2, r, CMP_STRIDE, HEAD_DIM, CMP_HIDDEN)
    z1 = jnp.zeros_like(w1)
    top = jnp.concatenate([w1, z1], axis=-1)
    bot = jnp.concatenate([z1, w1], axis=-1)
    w1bd = jnp.concatenate([top, bot], axis=3)
    w1bd = w1bd.reshape(2, r, CMP_STRIDE * LANES, N_KV_HEADS * CMP_HIDDEN).astype(BF16)
    z2 = jnp.zeros_like(cmp_w2)
    w2bd = jnp.concatenate([jnp.concatenate([cmp_w2, z2], axis=-1),
                            jnp.concatenate([z2, cmp_w2], axis=-1)], axis=1).astype(BF16)
    return pe_t, w1bd, w2bd


def _kv_out(kv_t):
    n, _, rows = kv_t.shape
    return jnp.transpose(kv_t.reshape(n, 2, N_KV_HEADS, HEAD_DIM, rows), (0, 4, 1, 2, 3))[None]


def kernel(x_prompt, x_sample, cache_cmp_kv, cache_slc_kv, cache_win_kv, state_pool, page_table, norm_g, w_in,
           cmp_pe, cmp_w1, cmp_w2, pool_w, pool_scale, w_out, final_g):
    n, seq_len, d_model = x_prompt.shape
    nb = x_sample.shape[0]
    n_phys, page = cache_cmp_kv.shape[1], cache_cmp_kv.shape[2]
    n_pages = page_table.shape[1]
    past_len = n_pages * page
    wlen = cache_win_kv.shape[2]

    w_r = _prep_w_in(w_in[0])
    pe_t, w1bd, w2bd = _prep_compress(cmp_pe[0], cmp_w1[0], cmp_w2[0])
    pool_w_b = pool_w[0].astype(BF16)
    w_out_b = w_out[0].astype(BF16)
    fg = final_g.reshape(1, d_model)

    (qt, crm, ckv_t, skv_t, wkv_t, ks, vst, kw, vwt, gates_t, sza, u, szp) = _project(
        x_prompt.reshape(n * seq_len, d_model), norm_g, w_r, tm=512, seq_len=seq_len)
    r3 = lambda a: a.reshape(n, seq_len, a.shape[-1])
    kc, rhs_c = _compress_prompt(crm, n, pe_t, w1bd, w2bd)
    a3 = _attn_prompt(qt, gates_t, r3(sza), kc, rhs_c, ks, vst, kw, vwt)
    y_prompt = _out_prompt(a3, r3(u), r3(szp), x_prompt, pool_w_b, pool_scale, w_out_b, fg, tm=512)

    new_cmp_p = _kv_out(ckv_t)
    new_slc_p = _kv_out(skv_t)
    new_win_p = _kv_out(wkv_t[:, :, seq_len - min(WINDOW, seq_len):])
    new_pool_p = r3(u)[:, seq_len - POOL_STATE:][None]

    (qt_s, _, ckv_ts, skv_ts, wkv_ts, _, _, _, _, gates_ts, sza_s, u_s, szp_s) = _project(
        x_sample.reshape(nb, d_model), norm_g, w_r, tm=nb, seq_len=nb)
    q_s = qt_s.T
    q5 = q_s.reshape(nb, N_KV_HEADS, GQA, 1, HEAD_DIM)
    eye = jnp.eye(N_KV_HEADS, dtype=q_s.dtype).reshape(1, N_KV_HEADS, 1, N_KV_HEADS, 1)
    qbd = (q5 * eye).reshape(nb, N_HEADS, LANES).astype(F32)

    to_pages = lambda c: jnp.transpose(c[0], (0, 2, 3, 4, 1)).reshape(n_phys, KV_W, page)
    o_c, imp8 = _cmp_sample(page_table, to_pages(cache_cmp_kv), qbd, pe_t, w1bd, w2bd)
    imp_g = jnp.transpose(imp8.reshape(nb * N_KV_HEADS, GQA, LANES), (1, 0, 2))
    n_blocks = -(-(past_len + 1) // SEL_BLOCK)
    notsel = _topk_sample(imp_g, t_pos=past_len, n_blocks=n_blocks)
    ns8 = jnp.repeat(notsel.reshape(nb, N_KV_HEADS, 1, LANES), GQA, axis=2).reshape(nb, N_HEADS, LANES)

    gates8 = jnp.transpose(gates_ts[:, :GQA * N_BRANCH, :], (2, 0, 1)).reshape(nb, N_HEADS, N_BRANCH)
    gates8 = jnp.pad(gates8, ((0, 0), (0, 0), (0, LANES - N_BRANCH)))
    win_t = jnp.transpose(cache_win_kv[0], (0, 2, 3, 4, 1)).reshape(nb, KV_W, wlen)
    s_new = skv_ts[0].T.reshape(nb, 1, KV_W)
    w_new = wkv_ts[0].T.reshape(nb, 1, KV_W)
    w_col = w_new.reshape(nb, KV_W, 1)
    o8, nwin_t = _attn_sample(page_table, to_pages(cache_slc_kv), win_t, qbd, ns8, s_new, w_new, w_col, o_c, gates8)
    o5 = o8.reshape(nb, N_KV_HEADS, GQA, N_KV_HEADS, HEAD_DIM)
    o2 = jnp.stack([o5[:, k, :, k, :] for k in range(N_KV_HEADS)], axis=1).reshape(nb, D_ATTN)

    state_t = jnp.transpose(state_pool[0], (1, 0, 2))
    y_sample = _out_sample(o2, sza_s, u_s, state_t, szp_s, x_sample.reshape(nb, d_model),
                           pool_w_b, pool_scale, w_out_b, fg).reshape(nb, 1, d_model)

    kv_out_s = lambda t: jnp.transpose(t.reshape(2, N_KV_HEADS, HEAD_DIM, nb, 1), (3, 4, 0, 1, 2))[None]
    new_cmp_s = kv_out_s(ckv_ts)
    new_slc_s = kv_out_s(skv_ts)
    new_win_s = jnp.transpose(nwin_t.reshape(nb, 2, N_KV_HEADS, HEAD_DIM, wlen), (0, 4, 1, 2, 3))[None]
    new_pool_s = jnp.transpose(jnp.concatenate([state_t[1:], u_s[None]], axis=0), (1, 0, 2))[None]

    return (y_prompt, y_sample, new_cmp_p, new_slc_p, new_win_p, new_pool_p,
            new_cmp_s, new_slc_s, new_win_s, new_pool_s)
```

```python
import functools

import jax
import jax.numpy as jnp
from jax import lax
from jax.experimental import pallas as pl
from jax.experimental.pallas import tpu as pltpu

F32 = jnp.float32
BF16 = jnp.bfloat16

HEAD_DIM = 64
N_KV_HEADS = 2
GQA = 4
N_HEADS = N_KV_HEADS * GQA
D_ATTN = N_HEADS * HEAD_DIM
KV_W = 2 * N_KV_HEADS * HEAD_DIM
N_BRANCH = 3
D_POOL = 512
CMP_BLOCK = 32
CMP_STRIDE = 16
CMP_HIDDEN = 128
SEL_BLOCK = 64
N_SEL = 16
WINDOW = 512
Q_BLOCK = 256
POOL_WINDOWS = (2, 4, 8, 16)
POOL_GROUP_DIM = 128
POOL_STATE = 15
RMS_EPS = 1e-6
LANES = 128
NEG_BIAS = -(2.0 ** 30)
NEG_FILL = -1e30
GATE_COLS = 2 * LANES
GATE_ROWS = 16
BF16_SUBLANES = 16
V_ROWS = HEAD_DIM + BF16_SUBLANES
KEY_TILE = 512
FORCED_VALUE = 1e9
N_FORCED = 3
CHUNK_PITCH = 20
SAMPLE_GROUP = 4
SEL_CHAINS = 2
LOG2E = 1.4426950408889634
P_PAD = D_ATTN + 3 * KV_W + 3 * 512 + GATE_COLS
VMEM_LIMIT = 48 * 1024 * 1024


def _cparams(sem):
    return pltpu.CompilerParams(dimension_semantics=sem, vmem_limit_bytes=VMEM_LIMIT)


def _silu(z):
    return z * jax.nn.sigmoid(z)


def _proj_kernel(x_ref, g_ref, w_ref, qt_ref, crm_ref, ct_ref, st_ref, wt_ref, ks_ref, vst_ref, kw_ref, vwt_ref,
                 gatet_ref, sza_ref, u_ref, szp_ref, *, tm, seq_len):
    i = pl.program_id(0)
    x = x_ref[...]
    ms = jnp.mean(x * x, axis=-1, keepdims=True)
    xn = (x * lax.rsqrt(ms + RMS_EPS) * g_ref[...]).astype(BF16)

    def mm(c0, c1):
        return jnp.dot(xn, w_ref[:, c0:c1], preferred_element_type=F32)

    qt_ref[...] = (mm(0, D_ATTN) * (HEAD_DIM ** -0.5 * LOG2E)).T.astype(BF16)
    kv = mm(D_ATTN, D_ATTN + 3 * KV_W)
    crm_ref[0] = kv[:, 0:LANES]
    crm_ref[1] = kv[:, LANES:2 * LANES]
    kvt = kv.T
    ct_ref[...] = kvt[0:KV_W, :]
    st_ref[...] = kvt[KV_W:2 * KV_W, :]
    wt_ref[...] = kvt[2 * KV_W:3 * KV_W, :]

    lane = lax.broadcasted_iota(jnp.int32, (tm, LANES), 1)
    pos = (i * tm) % seq_len + lax.broadcasted_iota(jnp.int32, (tm, LANES), 0)
    onehot = jnp.where(lane - HEAD_DIM == (pos // SEL_BLOCK) % HEAD_DIM, NEG_BIAS, 0.0).astype(F32)
    ones_rows = jnp.where(lax.broadcasted_iota(jnp.int32, (V_ROWS - HEAD_DIM, tm), 0) == 0, 1.0, 0.0).astype(BF16)
    for kvh in range(N_KV_HEADS):
        for (base, k_dst, v_dst, k_pad) in ((KV_W, ks_ref, vst_ref, onehot), (2 * KV_W, kw_ref, vwt_ref, 0.0)):
            slab = kv[:, base:base + LANES]
            if kvh == 1:
                slab = pltpu.roll(slab, HEAD_DIM, 1)
            k_dst[kvh, :, :] = jnp.where(lane < HEAD_DIM, slab, k_pad).astype(BF16)
            v0 = base + LANES + kvh * HEAD_DIM
            v_dst[kvh, 0:HEAD_DIM, :] = kvt[v0:v0 + HEAD_DIM, :].astype(BF16)
            v_dst[kvh, HEAD_DIM:V_ROWS, :] = ones_rows

    c = D_ATTN + 3 * KV_W
    sza_ref[...] = _silu(mm(c, c + 512)).astype(BF16)
    u_ref[...] = mm(c + 512, c + 1024)
    szp_ref[...] = _silu(mm(c + 1024, c + 1536)).astype(BF16)
    gate_t = jax.nn.sigmoid(mm(c + 1536, c + 1536 + GATE_COLS)).T
    for kvh in range(N_KV_HEADS):
        gatet_ref[kvh, :, :] = gate_t[kvh * LANES:kvh * LANES + GATE_ROWS, :]


def _project(x2d, norm_g, w_r, *, tm, seq_len):
    rows = x2d.shape[0]
    nt = rows // tm
    tps = seq_len // tm
    row_blk = lambda w: pl.BlockSpec((tm, w), lambda i: (i, 0))
    kv_t = jax.ShapeDtypeStruct((rows // seq_len, KV_W, seq_len), F32)
    kv_t_blk = pl.BlockSpec((None, KV_W, tm), lambda i: (i // tps, 0, i % tps))
    k_rm = jax.ShapeDtypeStruct((N_KV_HEADS, rows, LANES), BF16)
    k_rm_blk = pl.BlockSpec((N_KV_HEADS, tm, LANES), lambda i: (0, i, 0))
    v_t = jax.ShapeDtypeStruct((N_KV_HEADS, V_ROWS, rows), BF16)
    v_t_blk = pl.BlockSpec((N_KV_HEADS, V_ROWS, tm), lambda i: (0, 0, i))
    out_shape = (
        jax.ShapeDtypeStruct((D_ATTN, rows), BF16),
        jax.ShapeDtypeStruct((2, rows, LANES), F32),
        kv_t, kv_t, kv_t,
        k_rm,
        v_t,
        k_rm,
        v_t,
        jax.ShapeDtypeStruct((N_KV_HEADS, GATE_ROWS, rows), F32),
        jax.ShapeDtypeStruct((rows, 512), BF16),
        jax.ShapeDtypeStruct((rows, 512), F32),
        jax.ShapeDtypeStruct((rows, 512), BF16),
    )
    out_specs = (
        pl.BlockSpec((D_ATTN, tm), lambda i: (0, i)),
        pl.BlockSpec((2, tm, LANES), lambda i: (0, i, 0)),
        kv_t_blk, kv_t_blk, kv_t_blk,
        k_rm_blk, v_t_blk, k_rm_blk, v_t_blk,
        pl.BlockSpec((N_KV_HEADS, GATE_ROWS, tm), lambda i: (0, 0, i)),
        row_blk(512), row_blk(512), row_blk(512),
    )
    return pl.pallas_call(
        functools.partial(_proj_kernel, tm=tm, seq_len=seq_len),
        grid=(nt,),
        in_specs=[row_blk(x2d.shape[1]),
                  pl.BlockSpec((1, x2d.shape[1]), lambda i: (0, 0)),
                  pl.BlockSpec(w_r.shape, lambda i: (0, 0))],
        out_specs=out_specs,
        out_shape=out_shape,
        compiler_params=_cparams(("arbitrary",)),
        name="proj",
    )(x2d, norm_g, w_r)


def _compress_hidden(load_rows, pe_ref, w1_ref, kv, m):
    xs = [load_rows(s) for s in range(CMP_STRIDE)]
    hs = []
    for sub in range(CMP_BLOCK // CMP_STRIDE):
        lhs = jnp.concatenate([(xs[s] + pe_ref[kv, sub, s]).astype(BF16) for s in range(CMP_STRIDE)], axis=1)
        hs.append(jnp.dot(lhs, w1_ref[kv, sub], preferred_element_type=F32))
    return hs[0] + pltpu.roll(hs[1], m - 1, 0)


def _cover_matrix(nc_pad, ns_pad):
    c0 = lax.broadcasted_iota(jnp.int32, (nc_pad, ns_pad), 0) * CMP_STRIDE
    s0 = lax.broadcasted_iota(jnp.int32, (nc_pad, ns_pad), 1) * SEL_BLOCK
    return jnp.where((c0 < s0 + SEL_BLOCK) & (c0 + CMP_BLOCK > s0), 1.0, 0.0).astype(F32)


def _compress_prompt_kernel(c_ref, pe_ref, w1_ref, w2_ref, kc_ref, rhs_ref, *, nchunk):
    ones_rows = jnp.where(lax.broadcasted_iota(jnp.int32, (HEAD_DIM, nchunk), 0) == 0, 1.0, 0.0).astype(BF16)
    s0 = lax.broadcasted_iota(jnp.int32, (LANES, nchunk), 0) * SEL_BLOCK
    c0 = lax.broadcasted_iota(jnp.int32, (LANES, nchunk), 1) * CMP_STRIDE
    cover_t = jnp.where((c0 < s0 + SEL_BLOCK) & (c0 + CMP_BLOCK > s0), 1.0, 0.0).astype(BF16)
    for kv in range(2):
        load = lambda s: c_ref[kv, pl.ds(s, nchunk, stride=CMP_STRIDE), :]
        h = _compress_hidden(load, pe_ref, w1_ref, kv, nchunk)
        full = jnp.dot(_silu(h).astype(BF16), w2_ref[kv], preferred_element_type=F32)
        if kv == 0:
            kc_ref[0, :, :] = full[:, 0:HEAD_DIM].astype(BF16)
            kc_ref[1, :, :] = full[:, HEAD_DIM:2 * HEAD_DIM].astype(BF16)
        else:
            full_t = full.T
            for kvh in range(N_KV_HEADS):
                rhs_ref[kvh, 0:HEAD_DIM, :] = full_t[kvh * HEAD_DIM:(kvh + 1) * HEAD_DIM, :].astype(BF16)
                rhs_ref[kvh, HEAD_DIM:LANES, :] = ones_rows
                rhs_ref[kvh, LANES:2 * LANES, :] = cover_t


def _compress_prompt(crm, n, pe_t, w1bd, w2bd):
    seq_len = crm.shape[1] // n
    nchunk = seq_len // CMP_STRIDE
    return pl.pallas_call(
        functools.partial(_compress_prompt_kernel, nchunk=nchunk),
        grid=(n,),
        in_specs=[pl.BlockSpec((2, seq_len, LANES), lambda b: (0, b, 0)),
                  pl.BlockSpec(pe_t.shape, lambda b: (0, 0, 0, 0, 0)),
                  pl.BlockSpec(w1bd.shape, lambda b: (0, 0, 0, 0)),
                  pl.BlockSpec(w2bd.shape, lambda b: (0, 0, 0))],
        out_specs=(pl.BlockSpec((None, N_KV_HEADS, nchunk, HEAD_DIM), lambda b: (b, 0, 0, 0)),
                   pl.BlockSpec((None, N_KV_HEADS, 2 * LANES, nchunk), lambda b: (b, 0, 0, 0))),
        out_shape=(jax.ShapeDtypeStruct((n, N_KV_HEADS, nchunk, HEAD_DIM), BF16),
                   jax.ShapeDtypeStruct((n, N_KV_HEADS, 2 * LANES, nchunk), BF16)),
        compiler_params=_cparams(("arbitrary",)),
        name="compress_prompt",
    )(crm, pe_t, w1bd, w2bd)


def _not_selected(val, n_top, axis):
    blk = lax.broadcasted_iota(jnp.int32, val.shape, axis).astype(F32)
    forced = val >= FORCED_VALUE
    notsel = jnp.where(forced, 0.0, 1.0).astype(F32)
    val = jnp.where(forced, -3e38, val)
    for _ in range(max(n_top - N_FORCED, 0)):
        m = jnp.max(val, axis=axis, keepdims=True)
        idx = jnp.min(jnp.where(val == m, blk, float(LANES)), axis=axis, keepdims=True)
        pick = blk == idx
        notsel = jnp.where(pick, 0.0, notsel)
        val = jnp.where(pick, -3e38, val)
    return notsel


def _masked_importance(imp, t, n_blocks, axis):
    blk = lax.broadcasted_iota(jnp.int32, imp.shape, axis)
    cur = t // SEL_BLOCK
    forced = (blk == 0) | (blk == cur) | (blk == cur - 1)
    val = jnp.where(forced, FORCED_VALUE, jnp.where(blk * SEL_BLOCK <= t, imp, -1e9))
    return jnp.where(blk < n_blocks, val, -2e9)


def _attn_prompt_kernel(qt_ref, gatet_ref, sza_ref, kc_ref, rhs_ref, ks_ref, vst_ref, kw_ref, vwt_ref, out_ref,
                        qsel_ref, sbuf_ref, m_ref, acc_ref, mw_ref, accw_ref, oc_ref,
                        *, n_sel_blocks, seq_len):
    qb = pl.program_id(2)
    qs = qb * Q_BLOCK
    hq = GQA * Q_BLOCK
    t_row = qs + lax.broadcasted_iota(jnp.int32, (1, Q_BLOCK), 1)
    t_all = jnp.concatenate([t_row] * GQA, axis=1)
    qt = qt_ref[...]
    qt_all = jnp.concatenate([qt[g * HEAD_DIM:(g + 1) * HEAD_DIM, :] for g in range(GQA)], axis=1)
    zeros_lo = jnp.zeros((HEAD_DIM, hq), BF16)
    qw = jnp.concatenate([qt_all, zeros_lo], axis=0)

    tile = min(KEY_TILE, seq_len)
    n_full = qs // tile
    lo_tiles = HEAD_DIM * SEL_BLOCK // tile
    wwidth = min(WINDOW, seq_len)
    causal = lambda kpos: kpos <= t_all
    band = lambda kpos: (kpos > t_all - WINDOW) & (kpos <= t_all)
    diag_w = lambda kpos: (kpos <= t_all) & (qs >= wwidth)

    def flash_steps(steps):
        staged = []
        for (q, k_ref, vt_ref, start, width, mask_fn, mref, aref, c) in steps:
            start = pl.multiple_of(start, LANES)
            s = jnp.dot(k_ref[pl.ds(start, width), :], q, preferred_element_type=F32)
            if mask_fn is not None:
                s = jnp.where(mask_fn(start + lax.broadcasted_iota(jnp.int32, (width, hq), 0)), s, NEG_FILL)
            staged.append((s, vt_ref[:, pl.ds(start, width)]))
        softmaxed = []
        for (s, vt), step in zip(staged, steps):
            mref, c = step[6], step[8]
            m_old = mref[c]
            m_new = jnp.maximum(m_old, jnp.max(s, axis=0, keepdims=True))
            softmaxed.append((jnp.exp2(s - m_new).astype(BF16), jnp.exp2(m_old - m_new), m_new, vt))
        for (p, alpha, m_new, vt), step in zip(softmaxed, steps):
            mref, aref, c = step[6], step[7], step[8]
            aref[c] = alpha * aref[c] + jnp.dot(vt, p, preferred_element_type=F32)
            mref[c] = m_new

    m_ref[...] = jnp.full(m_ref.shape, NEG_FILL, F32)
    acc_ref[...] = jnp.zeros(acc_ref.shape, F32)
    mw_ref[...] = jnp.full(mw_ref.shape, NEG_FILL, F32)
    accw_ref[...] = jnp.zeros(accw_ref.shape, F32)

    nc = kc_ref.shape[0]
    c_last = lax.broadcasted_iota(jnp.int32, (nc, hq), 0) * CMP_STRIDE + (CMP_BLOCK - 1)
    ok_c = c_last <= t_all
    s = jnp.dot(kc_ref[...], qt_all, preferred_element_type=F32)
    flash_steps([(qw, kw_ref, vwt_ref, qs, Q_BLOCK, diag_w, mw_ref, accw_ref, 1)])
    s = jnp.where(ok_c, s, NEG_FILL)
    e = jnp.exp2(s - jnp.max(s, axis=0, keepdims=True))
    r = jnp.dot(rhs_ref[...], e.astype(BF16), preferred_element_type=F32)
    inv = jnp.where(t_all >= CMP_BLOCK - 1, 1.0 / r[HEAD_DIM:HEAD_DIM + 1, :], 0.0)
    qw_late = jnp.where(inv > -1.0, qw, jnp.zeros_like(qw))
    flash_steps([(qw_late, kw_ref, vwt_ref, jnp.maximum(qs - wwidth, 0), wwidth, band, mw_ref, accw_ref, 0)])
    oc_ref[...] = r[0:HEAD_DIM, :] * inv
    imp_all = r[LANES:2 * LANES, :] * inv
    imp = imp_all[:, 0:Q_BLOCK]
    for g in range(1, GQA):
        imp = imp + imp_all[:, g * Q_BLOCK:(g + 1) * Q_BLOCK]

    notsel = _not_selected(_masked_importance(imp, t_row, n_sel_blocks, 0), min(N_SEL, n_sel_blocks), 0)
    notsel = notsel.astype(BF16)
    q_lo = jnp.concatenate([qt_all, jnp.concatenate([notsel[0:HEAD_DIM, :]] * GQA, axis=1)], axis=0)
    q_hi = jnp.concatenate([qt_all, jnp.concatenate([notsel[HEAD_DIM:2 * HEAD_DIM, :]] * GQA, axis=1)], axis=0)
    qsel_ref[0] = q_lo
    qsel_ref[1] = q_hi

    def scores(b, kt, masked):
        start = pl.multiple_of(kt * tile, LANES)
        q = qsel_ref[jnp.where(kt >= lo_tiles, 1, 0)]
        s = jnp.dot(ks_ref[pl.ds(start, tile), :], q, preferred_element_type=F32)
        if masked:
            s = jnp.where(causal(start + lax.broadcasted_iota(jnp.int32, (tile, hq), 0)), s, NEG_FILL)
        sbuf_ref[b] = s

    def consume(b, kt):
        start = pl.multiple_of(kt * tile, LANES)
        s = sbuf_ref[b]
        m_old = m_ref[b]
        m_new = jnp.maximum(m_old, jnp.max(s, axis=0, keepdims=True))
        p = jnp.exp2(s - m_new).astype(BF16)
        acc_ref[b] = jnp.exp2(m_old - m_new) * acc_ref[b] + jnp.dot(vst_ref[:, pl.ds(start, tile)], p,
                                                                  preferred_element_type=F32)
        m_ref[b] = m_new

    @pl.when(n_full == 0)
    def _():
        scores(0, 0, True)

    @pl.when(n_full > 0)
    def _():
        scores(0, 0, False)

    def pair_step(j):
        scores(1, 2 * j + 1, False)
        consume(0, 2 * j)
        scores(0, 2 * j + 2, False)
        consume(1, 2 * j + 1)

    def pair_body(j, carry):
        pair_step(j)
        return carry

    def quad_body(j, carry):
        pair_step(2 * j)
        pair_step(2 * j + 1)
        return carry

    n_pairs = jnp.maximum(n_full - 1, 0) // 2
    lax.fori_loop(0, n_pairs // 2, quad_body, 0)
    lax.fori_loop(n_pairs // 2 * 2, n_pairs, pair_body, 0)

    @pl.when((n_full > 0) & (n_full % 2 == 0))
    def _():
        scores(1, n_full - 1, False)
        consume(0, n_full - 2)
        scores(0, n_full, True)
        consume(1, n_full - 1)
        consume(0, n_full)

    @pl.when(n_full % 2 == 1)
    def _():
        scores(1, n_full, True)
        consume(0, n_full - 1)
        consume(1, n_full)

    @pl.when(n_full == 0)
    def _():
        consume(0, 0)

    def merged(mref, aref, n_chain):
        m = mref[0]
        for c in range(1, n_chain):
            m = jnp.maximum(m, mref[c])
        a = jnp.exp2(mref[0] - m) * aref[0]
        for c in range(1, n_chain):
            a = a + jnp.exp2(mref[c] - m) * aref[c]
        return a[0:HEAD_DIM, :] * (1.0 / a[HEAD_DIM:HEAD_DIM + 1, :])

    gates = gatet_ref[...]
    sza = sza_ref[...]
    o_s = merged(m_ref, acc_ref, SEL_CHAINS)
    o_w = merged(mw_ref, accw_ref, 2)
    o_c = oc_ref[...]
    outs = []
    for g in range(GQA):
        cols = slice(g * Q_BLOCK, (g + 1) * Q_BLOCK)
        gc = gates[g * N_BRANCH + 0:g * N_BRANCH + 1, :]
        gs = gates[g * N_BRANCH + 1:g * N_BRANCH + 2, :]
        gw = gates[g * N_BRANCH + 2:g * N_BRANCH + 3, :]
        outs.append((gc * o_c[:, cols] + gs * o_s[:, cols] + gw * o_w[:, cols]).T)
    out_ref[...] = (jnp.concatenate(outs, axis=1) * sza).astype(BF16)


def _attn_prompt(qt, gates_t, sza3, kc, rhs_c, ks, vst, kw, vwt):
    n, seq_len, _ = sza3.shape
    nqb = seq_len // Q_BLOCK
    nchunk = kc.shape[2]
    hw = GQA * HEAD_DIM
    k_blk = pl.BlockSpec((None, seq_len, LANES), lambda b, k, i: (k, b, 0))
    vt_blk = pl.BlockSpec((None, V_ROWS, seq_len), lambda b, k, i: (k, 0, b))
    return pl.pallas_call(
        functools.partial(_attn_prompt_kernel, n_sel_blocks=-(-seq_len // SEL_BLOCK), seq_len=seq_len),
        grid=(n, N_KV_HEADS, nqb),
        in_specs=[
            pl.BlockSpec((hw, Q_BLOCK), lambda b, k, i: (k, b * nqb + i)),
            pl.BlockSpec((None, GATE_ROWS, Q_BLOCK), lambda b, k, i: (k, 0, b * nqb + i)),
            pl.BlockSpec((None, Q_BLOCK, hw), lambda b, k, i: (b, i, k)),
            pl.BlockSpec((None, None, nchunk, HEAD_DIM), lambda b, k, i: (b, k, 0, 0)),
            pl.BlockSpec((None, None, 2 * LANES, nchunk), lambda b, k, i: (b, k, 0, 0)),
            k_blk, vt_blk, k_blk, vt_blk,
        ],
        out_specs=pl.BlockSpec((None, Q_BLOCK, hw), lambda b, k, i: (b, i, k)),
        out_shape=jax.ShapeDtypeStruct((n, seq_len, D_ATTN), BF16),
        scratch_shapes=[
            pltpu.VMEM((2, 2 * HEAD_DIM, GQA * Q_BLOCK), BF16),
            pltpu.VMEM((SEL_CHAINS, min(KEY_TILE, seq_len), GQA * Q_BLOCK), F32),
            pltpu.VMEM((SEL_CHAINS, 1, GQA * Q_BLOCK), F32),
            pltpu.VMEM((SEL_CHAINS, V_ROWS, GQA * Q_BLOCK), F32),
            pltpu.VMEM((2, 1, GQA * Q_BLOCK), F32),
            pltpu.VMEM((2, V_ROWS, GQA * Q_BLOCK), F32),
            pltpu.VMEM((HEAD_DIM, GQA * Q_BLOCK), F32),
        ],
        compiler_params=_cparams(("arbitrary", "arbitrary", "arbitrary")),
        name="attn_prompt",
    )(qt, gates_t, sza3, kc, rhs_c, ks, vst, kw, vwt)


def _pool_out(d, pw_ref, ps_ref, szp):
    ys = [jnp.dot(d[:, g * POOL_GROUP_DIM:(g + 1) * POOL_GROUP_DIM].astype(BF16), pw_ref[g],
                  preferred_element_type=F32) for g in range(len(POOL_WINDOWS))]
    return jnp.concatenate(ys, axis=1) * ps_ref[...] * szp


def _finish(x, a_bf16, b, wo_ref, fg_ref):
    mix = jnp.concatenate([a_bf16, b.astype(BF16)], axis=1)
    y = x + jnp.dot(mix, wo_ref[...], preferred_element_type=F32)
    ms = jnp.mean(y * y, axis=-1, keepdims=True)
    return y * lax.rsqrt(ms + RMS_EPS) * fg_ref[...]


def _out_prompt_kernel(a_ref, u_ref, halo_ref, szp_ref, x_ref, pw_ref, ps_ref, wo_ref, fg_ref, y_ref, ext_ref,
                       *, tm, halo):
    i = pl.program_id(1)
    u = u_ref[...]
    ext_ref[0:halo, :] = jnp.where(i > 0, halo_ref[...], 0.0)
    ext_ref[halo:halo + tm, :] = u
    pos = i * tm + lax.broadcasted_iota(jnp.int32, (tm, POOL_GROUP_DIM), 0)
    ds = []
    for g, w in enumerate(POOL_WINDOWS):
        c0 = g * POOL_GROUP_DIM
        acc = u[:, c0:c0 + POOL_GROUP_DIM]
        for k in range(1, w):
            acc = acc + ext_ref[halo - k:halo - k + tm, c0:c0 + POOL_GROUP_DIM]
        cnt = jnp.minimum(pos + 1, w).astype(F32)
        ds.append(acc / cnt - u[:, c0:c0 + POOL_GROUP_DIM])
    b = _pool_out(jnp.concatenate(ds, axis=1), pw_ref, ps_ref, szp_ref[...])
    y_ref[...] = _finish(x_ref[...], a_ref[...], b, wo_ref, fg_ref)


def _out_prompt(a3, u3, szp3, x3, pool_w, pool_scale, w_out, final_g, *, tm):
    n, seq_len, d_model = x3.shape
    halo = 16
    nt = seq_len // tm
    blk = lambda w: pl.BlockSpec((None, tm, w), lambda b, i: (b, i, 0))
    const = lambda a: pl.BlockSpec(a.shape, lambda b, i: (0,) * a.ndim)
    return pl.pallas_call(
        functools.partial(_out_prompt_kernel, tm=tm, halo=halo),
        grid=(n, nt),
        in_specs=[blk(D_ATTN), blk(D_POOL),
                  pl.BlockSpec((None, halo, D_POOL), lambda b, i: (b, jnp.maximum(i * (tm // halo) - 1, 0), 0)),
                  blk(D_POOL), blk(d_model),
                  const(pool_w), const(pool_scale), const(w_out), const(final_g)],
        out_specs=blk(d_model),
        out_shape=jax.ShapeDtypeStruct((n, seq_len, d_model), F32),
        scratch_shapes=[pltpu.VMEM((tm + halo, D_POOL), F32)],
        compiler_params=_cparams(("arbitrary", "arbitrary")),
        name="out_prompt",
    )(a3, u3, u3, szp3, x3, pool_w, pool_scale, w_out, final_g)


def _out_sample_kernel(o_ref, sza_ref, u_ref, st_ref, szp_ref, x_ref, pw_ref, ps_ref, wo_ref, fg_ref, y_ref):
    u = u_ref[...]
    ds = []
    for g, w in enumerate(POOL_WINDOWS):
        c0 = g * POOL_GROUP_DIM
        acc = u[:, c0:c0 + POOL_GROUP_DIM]
        for k in range(1, w):
            acc = acc + st_ref[POOL_STATE - k, :, c0:c0 + POOL_GROUP_DIM]
        ds.append(acc / float(w) - u[:, c0:c0 + POOL_GROUP_DIM])
    b = _pool_out(jnp.concatenate(ds, axis=1), pw_ref, ps_ref, szp_ref[...])
    a = (o_ref[...] * sza_ref[...]).astype(BF16)
    y_ref[...] = _finish(x_ref[...], a, b, wo_ref, fg_ref)


def _out_sample(o2, sza, u, state_t, szp, x2, pool_w, pool_scale, w_out, final_g):
    args = (o2, sza, u, state_t, szp, x2, pool_w, pool_scale, w_out, final_g)
    full = lambda a: pl.BlockSpec(a.shape, lambda i: (0,) * a.ndim)
    return pl.pallas_call(
        _out_sample_kernel,
        grid=(1,),
        in_specs=[full(a) for a in args],
        out_specs=full(x2),
        out_shape=jax.ShapeDtypeStruct(x2.shape, F32),
        compiler_params=_cparams(("arbitrary",)),
        name="out_sample",
    )(*args)


def _page_copy(pt_ref, cache_ref, buf_ref, sem_ref, step, slot, j, n_pages, n_seq):
    g, p = divmod(j, n_pages)
    return pltpu.make_async_copy(cache_ref.at[pt_ref[step * n_seq + g, p]], buf_ref.at[slot, j], sem_ref.at[slot])


def _fetch_pages(pt_ref, cache_ref, buf_ref, sem_ref, n_pages, n_seq):
    i = pl.program_id(0)
    slot = i % 2
    n_copy = n_seq * n_pages

    @pl.when(i == 0)
    def _():
        for j in range(n_copy):
            _page_copy(pt_ref, cache_ref, buf_ref, sem_ref, 0, 0, j, n_pages, n_seq).start()

    @pl.when(i + 1 < pl.num_programs(0))
    def _():
        for j in range(n_copy):
            _page_copy(pt_ref, cache_ref, buf_ref, sem_ref, i + 1, 1 - slot, j, n_pages, n_seq).start()

    for j in range(n_copy):
        _page_copy(pt_ref, cache_ref, buf_ref, sem_ref, i, slot, j, n_pages, n_seq).wait()
    return slot


def _cmp_sample_kernel(pt_ref, cache_ref, qbd_ref, pe_ref, w1_ref, w2_ref, oc_ref, imp_ref, rows_ref, buf_ref,
                       sem_ref, *, n_pages, page, n_seq):
    slot = _fetch_pages(pt_ref, cache_ref, buf_ref, sem_ref, n_pages, n_seq)
    pages = [buf_ref.at[slot, j] for j in range(n_seq * n_pages)]
    nchunk = n_pages * page // CMP_STRIDE
    fulls = []
    m = n_seq * nchunk
    cpp = page // CMP_STRIDE
    for kv in range(2):
        for i, page_ref in enumerate(pages):
            half = page_ref[kv * LANES:(kv + 1) * LANES, :].astype(BF16)
            rows = half.T.astype(F32)
            for c in range(cpp):
                dst = (i * cpp + c) * CHUNK_PITCH
                rows_ref[kv, dst:dst + CMP_STRIDE, :] = rows[c * CMP_STRIDE:(c + 1) * CMP_STRIDE, :]
        load = lambda s: rows_ref[kv, pl.ds(s, m, stride=CHUNK_PITCH), :]
        h = _compress_hidden(load, pe_ref, w1_ref, kv, m)
        fulls.append(jnp.dot(_silu(h).astype(BF16), w2_ref[kv], preferred_element_type=F32).astype(BF16))
    k_c, v_c = fulls
    nrow = n_seq * N_HEADS
    qbd = qbd_ref[...].reshape(nrow, LANES).astype(BF16)
    s = lax.dot_general(qbd, k_c, (((1,), (1,)), ((), ())), preferred_element_type=F32)
    col = lax.broadcasted_iota(jnp.int32, s.shape, 1)
    own = col // nchunk == lax.broadcasted_iota(jnp.int32, s.shape, 0) // N_HEADS
    ok = own & (col % nchunk < nchunk - 1)
    s = jnp.where(ok, s, NEG_FILL)
    e = jnp.where(ok, jnp.exp2(s - jnp.max(s, axis=1, keepdims=True)), 0.0)
    pc = (e * (1.0 / jnp.sum(e, axis=1, keepdims=True))).astype(BF16)
    cover = jnp.concatenate([_cover_matrix(nchunk, LANES).astype(BF16)] * n_seq, axis=0)
    oc_ref[...] = jnp.dot(pc, v_c, preferred_element_type=F32).reshape(n_seq, N_HEADS, LANES)
    imp_ref[...] = jnp.dot(pc, cover, preferred_element_type=F32).reshape(n_seq, N_HEADS, LANES)


def _cmp_sample(page_table, cache_t, qbd, pe_t, w1bd, w2bd):
    nb, n_pages = page_table.shape
    page = cache_t.shape[-1]
    n_seq = SAMPLE_GROUP
    const = lambda a: pl.BlockSpec(a.shape, lambda b, pt: (0,) * a.ndim)
    per_b = pl.BlockSpec((n_seq, N_HEADS, LANES), lambda b, pt: (b, 0, 0))
    grid_spec = pltpu.PrefetchScalarGridSpec(
        num_scalar_prefetch=1,
        grid=(nb // n_seq,),
        in_specs=[pl.BlockSpec(memory_space=pl.ANY), per_b, const(pe_t), const(w1bd), const(w2bd)],
        out_specs=(per_b, per_b),
        scratch_shapes=[pltpu.VMEM((2, n_seq * n_pages * page // CMP_STRIDE * CHUNK_PITCH, LANES), F32),
                        pltpu.VMEM((2, n_seq * n_pages, KV_W, page), F32),
                        pltpu.SemaphoreType.DMA((2,))],
    )
    return pl.pallas_call(
        functools.partial(_cmp_sample_kernel, n_pages=n_pages, page=page, n_seq=n_seq),
        grid_spec=grid_spec,
        out_shape=(jax.ShapeDtypeStruct((nb, N_HEADS, LANES), F32),
                   jax.ShapeDtypeStruct((nb, N_HEADS, LANES), F32)),
        compiler_params=_cparams(("arbitrary",)),
        name="cmp_sample",
    )(page_table, cache_t, qbd, pe_t, w1bd, w2bd)


def _topk_sample_kernel(imp_ref, out_ref, *, t_pos, n_blocks):
    imp = imp_ref[0] + imp_ref[1] + imp_ref[2] + imp_ref[3]
    t = jnp.full((imp.shape[0], 1), t_pos, jnp.int32)
    out_ref[...] = _not_selected(_masked_importance(imp, t, n_blocks, 1), min(N_SEL, n_blocks), 1)


def _topk_sample(imp_g, *, t_pos, n_blocks):
    rows = imp_g.shape[1]
    return pl.pallas_call(
        functools.partial(_topk_sample_kernel, t_pos=t_pos, n_blocks=n_blocks),
        grid=(1,),
        in_specs=[pl.BlockSpec(imp_g.shape, lambda i: (0, 0, 0))],
        out_specs=pl.BlockSpec((rows, LANES), lambda i: (0, 0)),
        out_shape=jax.ShapeDtypeStruct((rows, LANES), F32),
        compiler_params=_cparams(("arbitrary",)),
        name="topk_sample",
    )(imp_g)


def _attn_sample_kernel(pt_ref, *refs, n_pages, page, n_seq):
    (cache_ref, win_ref, qbd_ref, ns_ref, snew_ref, wnew_ref, oc_ref, gate_ref, o_ref, nwin_ref,
     buf_ref, sem_ref) = refs
    slot = _fetch_pages(pt_ref, cache_ref, buf_ref, sem_ref, n_pages, n_seq)
    pages = [buf_ref.at[slot, j] for j in range(n_seq * n_pages)]
    nrow = n_seq * N_HEADS
    qf = qbd_ref[...].reshape(nrow, LANES)
    qbd = qf.astype(BF16)
    ns = ns_ref[...].reshape(nrow, LANES)
    row_seq = lax.broadcasted_iota(jnp.int32, (nrow, 1), 0) // N_HEADS
    nt_ = (((1,), (1,)), ((), ()))

    def new_token(rows_ref):
        rows = jnp.concatenate([jnp.broadcast_to(rows_ref[g], (N_HEADS, KV_W)) for g in range(n_seq)], axis=0)
        k_new = rows[:, 0:LANES].astype(BF16).astype(F32)
        v_new = rows[:, LANES:2 * LANES].astype(BF16).astype(F32)
        return jnp.sum(qf * k_new, axis=1, keepdims=True), v_new

    lane = lax.broadcasted_iota(jnp.int32, (nrow, page), 1)
    per_page = page // SEL_BLOCK
    scores = []
    for i, page_ref in enumerate(pages):
        g, p = divmod(i, n_pages)
        s = jnp.dot(qbd, page_ref[0:LANES, :].astype(BF16), preferred_element_type=F32)
        flag = ns[:, p * per_page:p * per_page + 1]
        for j in range(1, per_page):
            flag = jnp.where(lane >= j * SEL_BLOCK, ns[:, p * per_page + j:p * per_page + j + 1], flag)
        scores.append(jnp.where((row_seq == g) & (flag < 0.5), s, NEG_FILL))
    nb_cache = n_pages * per_page
    s_new, v_new = new_token(snew_ref)
    s_new = jnp.where(ns[:, nb_cache:nb_cache + 1] > 0.5, NEG_FILL, s_new)
    s_max = scores[0]
    for s in scores[1:]:
        s_max = jnp.maximum(s_max, s)
    m = jnp.maximum(s_new, jnp.max(s_max, axis=1, keepdims=True))
    e_new = jnp.exp2(s_new - m)
    e_sum = jnp.zeros((nrow, page), F32)
    acc = e_new * v_new
    for s, page_ref in zip(scores, pages):
        e = jnp.exp2(s - m).astype(BF16)
        e_sum = e_sum + e.astype(F32)
        acc = acc + lax.dot_general(e, page_ref[LANES:2 * LANES, :].astype(BF16), nt_, preferred_element_type=F32)
    o_s = acc * (1.0 / (e_new + jnp.sum(e_sum, axis=1, keepdims=True)))

    wlen = win_ref.shape[2]
    first = max(wlen - WINDOW + 1, 0)
    in_win = lax.broadcasted_iota(jnp.int32, (nrow, wlen), 1) >= first
    scores_w = []
    for g in range(n_seq):
        s_w = jnp.dot(qbd, win_ref[g, 0:LANES, :].astype(BF16), preferred_element_type=F32)
        scores_w.append(jnp.where((row_seq == g) & in_win, s_w, NEG_FILL))
    sw_new, vw_new = new_token(wnew_ref)
    sw_max = scores_w[0]
    for s_w in scores_w[1:]:
        sw_max = jnp.maximum(sw_max, s_w)
    m_w = jnp.maximum(sw_new, jnp.max(sw_max, axis=1, keepdims=True))
    ew_new = jnp.exp2(sw_new - m_w)
    ew_sum = jnp.zeros((nrow, wlen), F32)
    acc_w = ew_new * vw_new
    for g in range(n_seq):
        e_w = jnp.exp2(scores_w[g] - m_w).astype(BF16)
        ew_sum = ew_sum + e_w.astype(F32)
        acc_w = acc_w + lax.dot_general(e_w, win_ref[g, LANES:2 * LANES, :].astype(BF16), nt_,
                                        preferred_element_type=F32)
    o_w = acc_w * (1.0 / (ew_new + jnp.sum(ew_sum, axis=1, keepdims=True)))

    gates = gate_ref[...].reshape(nrow, LANES)
    o = gates[:, 0:1] * oc_ref[...].reshape(nrow, LANES) + gates[:, 1:2] * o_s + gates[:, 2:3] * o_w
    o_ref[...] = o.reshape(n_seq, N_HEADS, LANES)

    last = lax.broadcasted_iota(jnp.int32, (KV_W, LANES), 1) == LANES - 1
    for g in range(n_seq):
        shifted = pltpu.roll(win_ref[g], wlen - 1, 1)
        new_col = jnp.broadcast_to(wnew_ref[g], (LANES, KV_W)).T
        nwin_ref[g, :, 0:wlen - LANES] = shifted[:, 0:wlen - LANES]
        nwin_ref[g, :, wlen - LANES:wlen] = jnp.where(last, new_col, shifted[:, wlen - LANES:wlen])


def _attn_sample(page_table, cache_t, win_t, qbd, ns8, s_new, w_new, o_c, gates8):
    nb, n_pages = page_table.shape
    page = cache_t.shape[-1]
    n_seq = SAMPLE_GROUP
    per_b = lambda a: pl.BlockSpec((n_seq,) + a.shape[1:], lambda b, pt: (b,) + (0,) * (a.ndim - 1))
    grid_spec = pltpu.PrefetchScalarGridSpec(
        num_scalar_prefetch=1,
        grid=(nb // n_seq,),
        in_specs=[pl.BlockSpec(memory_space=pl.ANY)]
        + [per_b(a) for a in (win_t, qbd, ns8, s_new, w_new, o_c, gates8)],
        out_specs=(per_b(o_c), per_b(win_t)),
        scratch_shapes=[pltpu.VMEM((2, n_seq * n_pages, KV_W, page), F32),
                        pltpu.SemaphoreType.DMA((2,))],
    )
    return pl.pallas_call(
        functools.partial(_attn_sample_kernel, n_pages=n_pages, page=page, n_seq=n_seq),
        grid_spec=grid_spec,
        out_shape=(jax.ShapeDtypeStruct(o_c.shape, F32), jax.ShapeDtypeStruct(win_t.shape, F32)),
        compiler_params=_cparams(("arbitrary",)),
        name="attn_sample",
    )(page_table, cache_t, win_t, qbd, ns8, s_new, w_new, o_c, gates8)


def _prep_w_in(w_in):
    o = D_ATTN + 3 * KV_W
    ng = GQA * N_BRANCH
    gl = w_in[:, o:o + N_KV_HEADS * ng]
    rest = w_in[:, o + N_KV_HEADS * ng:]
    pad = jnp.zeros((w_in.shape[0], LANES - ng), w_in.dtype)
    return jnp.concatenate([w_in[:, :o], rest, gl[:, :ng], pad, gl[:, ng:], pad], axis=1).astype(BF16)


def _prep_compress(cmp_pe, cmp_w1, cmp_w2):
    r = CMP_BLOCK // CMP_STRIDE
    pe_t = jnp.tile(cmp_pe.reshape(2, r, CMP_STRIDE, 1, HEAD_DIM), (1, 1, 1, 1, N_KV_HEADS))
    w1 = cmp_w1.reshape(2, r, CMP_STRIDE, HEAD_DIM, CMP_HIDDEN)
    z1 = jnp.zeros_like(w1)
    top = jnp.concatenate([w1, z1], axis=-1)
    bot = jnp.concatenate([z1, w1], axis=-1)
    w1bd = jnp.concatenate([top, bot], axis=3)
    w1bd = w1bd.reshape(2, r, CMP_STRIDE * LANES, N_KV_HEADS * CMP_HIDDEN).astype(BF16)
    z2 = jnp.zeros_like(cmp_w2)
    w2bd = jnp.concatenate([jnp.concatenate([cmp_w2, z2], axis=-1),
                            jnp.concatenate([z2, cmp_w2], axis=-1)], axis=1).astype(BF16)
    return pe_t, w1bd, w2bd


def _kv_out(kv_t):
    n, _, rows = kv_t.shape
    return jnp.transpose(kv_t.reshape(n, 2, N_KV_HEADS, HEAD_DIM, rows), (0, 4, 1, 2, 3))[None]


def kernel(x_prompt, x_sample, cache_cmp_kv, cache_slc_kv, cache_win_kv, state_pool, page_table, norm_g, w_in,
           cmp_pe, cmp_w1, cmp_w2, pool_w, pool_scale, w_out, final_g):
    n, seq_len, d_model = x_prompt.shape
    nb = x_sample.shape[0]
    n_phys, page = cache_cmp_kv.shape[1], cache_cmp_kv.shape[2]
    n_pages = page_table.shape[1]
    past_len = n_pages * page
    wlen = cache_win_kv.shape[2]

    w_r = _prep_w_in(w_in[0])
    pe_t, w1bd, w2bd = _prep_compress(cmp_pe[0], cmp_w1[0], cmp_w2[0])
    pool_w_b = pool_w[0].astype(BF16)
    w_out_b = w_out[0].astype(BF16)
    fg = final_g.reshape(1, d_model)

    (qt, crm, ckv_t, skv_t, wkv_t, ks, vst, kw, vwt, gates_t, sza, u, szp) = _project(
        x_prompt.reshape(n * seq_len, d_model), norm_g, w_r, tm=512, seq_len=seq_len)
    r3 = lambda a: a.reshape(n, seq_len, a.shape[-1])
    kc, rhs_c = _compress_prompt(crm, n, pe_t, w1bd, w2bd)
    a3 = _attn_prompt(qt, gates_t, r3(sza), kc, rhs_c, ks, vst, kw, vwt)
    y_prompt = _out_prompt(a3, r3(u), r3(szp), x_prompt, pool_w_b, pool_scale, w_out_b, fg, tm=512)

    new_cmp_p = _kv_out(ckv_t)
    new_slc_p = _kv_out(skv_t)
    new_win_p = _kv_out(wkv_t[:, :, seq_len - min(WINDOW, seq_len):])
    new_pool_p = r3(u)[:, seq_len - POOL_STATE:][None]

    (qt_s, _, ckv_ts, skv_ts, wkv_ts, _, _, _, _, gates_ts, sza_s, u_s, szp_s) = _project(
        x_sample.reshape(nb, d_model), norm_g, w_r, tm=nb, seq_len=nb)
    q_s = qt_s.T
    q5 = q_s.reshape(nb, N_KV_HEADS, GQA, 1, HEAD_DIM)
    eye = jnp.eye(N_KV_HEADS, dtype=q_s.dtype).reshape(1, N_KV_HEADS, 1, N_KV_HEADS, 1)
    qbd = (q5 * eye).reshape(nb, N_HEADS, LANES).astype(F32)

    to_pages = lambda c: jnp.transpose(c[0], (0, 2, 3, 4, 1)).reshape(n_phys, KV_W, page)
    o_c, imp8 = _cmp_sample(page_table, to_pages(cache_cmp_kv), qbd, pe_t, w1bd, w2bd)
    imp_g = jnp.transpose(imp8.reshape(nb * N_KV_HEADS, GQA, LANES), (1, 0, 2))
    n_blocks = -(-(past_len + 1) // SEL_BLOCK)
    notsel = _topk_sample(imp_g, t_pos=past_len, n_blocks=n_blocks)
    ns8 = jnp.repeat(notsel.reshape(nb, N_KV_HEADS, 1, LANES), GQA, axis=2).reshape(nb, N_HEADS, LANES)

    gates8 = jnp.transpose(gates_ts[:, :GQA * N_BRANCH, :], (2, 0, 1)).reshape(nb, N_HEADS, N_BRANCH)
    gates8 = jnp.pad(gates8, ((0, 0), (0, 0), (0, LANES - N_BRANCH)))
    win_t = jnp.transpose(cache_win_kv[0], (0, 2, 3, 4, 1)).reshape(nb, KV_W, wlen)
    s_new = skv_ts[0].T.reshape(nb, 1, KV_W)
    w_new = wkv_ts[0].T.reshape(nb, 1, KV_W)
    o8, nwin_t = _attn_sample(page_table, to_pages(cache_slc_kv), win_t, qbd, ns8, s_new, w_new, o_c, gates8)
    o5 = o8.reshape(nb, N_KV_HEADS, GQA, N_KV_HEADS, HEAD_DIM)
    o2 = jnp.stack([o5[:, k, :, k, :] for k in range(N_KV_HEADS)], axis=1).reshape(nb, D_ATTN)

    state_t = jnp.transpose(state_pool[0], (1, 0, 2))
    y_sample = _out_sample(o2, sza_s, u_s, state_t, szp_s, x_sample.reshape(nb, d_model),
                           pool_w_b, pool_scale, w_out_b, fg).reshape(nb, 1, d_model)

    kv_out_s = lambda t: jnp.transpose(t.reshape(2, N_KV_HEADS, HEAD_DIM, nb, 1), (3, 4, 0, 1, 2))[None]
    new_cmp_s = kv_out_s(ckv_ts)
    new_slc_s = kv_out_s(skv_ts)
    new_win_s = jnp.transpose(nwin_t.reshape(nb, 2, N_KV_HEADS, HEAD_DIM, wlen), (0, 4, 1, 2, 3))[None]
    new_pool_s = jnp.transpose(jnp.concatenate([state_t[1:], u_s[None]], axis=0), (1, 0, 2))[None]

    return (y_prompt, y_sample, new_cmp_p, new_slc_p, new_win_p, new_pool_p,
            new_cmp_s, new_slc_s, new_win_s, new_pool_s)
```

```python
import functools

import jax
import jax.numpy as jnp
from jax import lax
from jax.experimental import pallas as pl
from jax.experimental.pallas import tpu as pltpu

F32 = jnp.float32
BF16 = jnp.bfloat16

HEAD_DIM = 64
N_KV_HEADS = 2
GQA = 4
N_HEADS = N_KV_HEADS * GQA
D_ATTN = N_HEADS * HEAD_DIM
KV_W = 2 * N_KV_HEADS * HEAD_DIM
N_BRANCH = 3
D_POOL = 512
CMP_BLOCK = 32
CMP_STRIDE = 16
CMP_HIDDEN = 128
SEL_BLOCK = 64
N_SEL = 16
WINDOW = 512
Q_BLOCK = 256
POOL_WINDOWS = (2, 4, 8, 16)
POOL_GROUP_DIM = 128
POOL_STATE = 15
RMS_EPS = 1e-6
LANES = 128
NEG_BIAS = -(2.0 ** 30)
NEG_FILL = -1e30
GATE_COLS = 2 * LANES
GATE_ROWS = 16
BF16_SUBLANES = 16
V_ROWS = HEAD_DIM + BF16_SUBLANES
KEY_TILE = 512
FORCED_VALUE = 1e9
N_FORCED = 3
CHUNK_PITCH = 20
SAMPLE_GROUP = 4
SEL_CHAINS = 2
LOG2E = 1.4426950408889634
P_PAD = D_ATTN + 3 * KV_W + 3 * 512 + GATE_COLS
VMEM_LIMIT = 48 * 1024 * 1024


def _cparams(sem):
    return pltpu.CompilerParams(dimension_semantics=sem, vmem_limit_bytes=VMEM_LIMIT)


def _silu(z):
    return z * jax.nn.sigmoid(z)


def _proj_kernel(x_ref, g_ref, w_ref, qt_ref, crm_ref, ct_ref, st_ref, wt_ref, ks_ref, vst_ref, kw_ref, vwt_ref,
                 gatet_ref, sza_ref, u_ref, szp_ref, *, tm, seq_len):
    i = pl.program_id(0)
    x = x_ref[...]
    ms = jnp.mean(x * x, axis=-1, keepdims=True)
    xn = (x * lax.rsqrt(ms + RMS_EPS) * g_ref[...]).astype(BF16)

    def mm(c0, c1):
        return lax.dot_general(xn, w_ref[c0:c1, :], (((1,), (1,)), ((), ())), preferred_element_type=F32)

    qt_ref[...] = (mm(0, D_ATTN) * (HEAD_DIM ** -0.5 * LOG2E)).T.astype(BF16)
    kv = mm(D_ATTN, D_ATTN + 3 * KV_W)
    crm_ref[0] = kv[:, 0:LANES]
    crm_ref[1] = kv[:, LANES:2 * LANES]
    kvt = kv.T
    ct_ref[...] = kvt[0:KV_W, :]
    st_ref[...] = kvt[KV_W:2 * KV_W, :]
    wt_ref[...] = kvt[2 * KV_W:3 * KV_W, :]

    lane = lax.broadcasted_iota(jnp.int32, (tm, LANES), 1)
    pos = (i * tm) % seq_len + lax.broadcasted_iota(jnp.int32, (tm, LANES), 0)
    onehot = jnp.where(lane - HEAD_DIM == (pos // SEL_BLOCK) % HEAD_DIM, NEG_BIAS, 0.0).astype(F32)
    ones_rows = jnp.where(lax.broadcasted_iota(jnp.int32, (V_ROWS - HEAD_DIM, tm), 0) == 0, 1.0, 0.0).astype(BF16)
    for kvh in range(N_KV_HEADS):
        for (base, k_dst, v_dst, k_pad) in ((KV_W, ks_ref, vst_ref, onehot), (2 * KV_W, kw_ref, vwt_ref, 0.0)):
            slab = kv[:, base:base + LANES]
            if kvh == 1:
                slab = pltpu.roll(slab, HEAD_DIM, 1)
            k_dst[kvh, :, :] = jnp.where(lane < HEAD_DIM, slab, k_pad).astype(BF16)
            v0 = base + LANES + kvh * HEAD_DIM
            v_dst[kvh, 0:HEAD_DIM, :] = kvt[v0:v0 + HEAD_DIM, :].astype(BF16)
            v_dst[kvh, HEAD_DIM:V_ROWS, :] = ones_rows

    c = D_ATTN + 3 * KV_W
    sza_ref[...] = _silu(mm(c, c + 512)).astype(BF16)
    u_ref[...] = mm(c + 512, c + 1024)
    szp_ref[...] = _silu(mm(c + 1024, c + 1536)).astype(BF16)
    gate_t = jax.nn.sigmoid(mm(c + 1536, c + 1536 + GATE_COLS)).T
    for kvh in range(N_KV_HEADS):
        gatet_ref[kvh, :, :] = gate_t[kvh * LANES:kvh * LANES + GATE_ROWS, :]


def _project(x2d, norm_g, w_r, *, tm, seq_len):
    rows = x2d.shape[0]
    nt = rows // tm
    tps = seq_len // tm
    row_blk = lambda w: pl.BlockSpec((tm, w), lambda i: (i, 0))
    kv_t = jax.ShapeDtypeStruct((rows // seq_len, KV_W, seq_len), F32)
    kv_t_blk = pl.BlockSpec((None, KV_W, tm), lambda i: (i // tps, 0, i % tps))
    k_rm = jax.ShapeDtypeStruct((N_KV_HEADS, rows, LANES), BF16)
    k_rm_blk = pl.BlockSpec((N_KV_HEADS, tm, LANES), lambda i: (0, i, 0))
    v_t = jax.ShapeDtypeStruct((N_KV_HEADS, V_ROWS, rows), BF16)
    v_t_blk = pl.BlockSpec((N_KV_HEADS, V_ROWS, tm), lambda i: (0, 0, i))
    out_shape = (
        jax.ShapeDtypeStruct((D_ATTN, rows), BF16),
        jax.ShapeDtypeStruct((2, rows, LANES), F32),
        kv_t, kv_t, kv_t,
        k_rm,
        v_t,
        k_rm,
        v_t,
        jax.ShapeDtypeStruct((N_KV_HEADS, GATE_ROWS, rows), F32),
        jax.ShapeDtypeStruct((rows, 512), BF16),
        jax.ShapeDtypeStruct((rows, 512), F32),
        jax.ShapeDtypeStruct((rows, 512), BF16),
    )
    out_specs = (
        pl.BlockSpec((D_ATTN, tm), lambda i: (0, i)),
        pl.BlockSpec((2, tm, LANES), lambda i: (0, i, 0)),
        kv_t_blk, kv_t_blk, kv_t_blk,
        k_rm_blk, v_t_blk, k_rm_blk, v_t_blk,
        pl.BlockSpec((N_KV_HEADS, GATE_ROWS, tm), lambda i: (0, 0, i)),
        row_blk(512), row_blk(512), row_blk(512),
    )
    return pl.pallas_call(
        functools.partial(_proj_kernel, tm=tm, seq_len=seq_len),
        grid=(nt,),
        in_specs=[row_blk(x2d.shape[1]),
                  pl.BlockSpec((1, x2d.shape[1]), lambda i: (0, 0)),
                  pl.BlockSpec(w_r.shape, lambda i: (0, 0))],
        out_specs=out_specs,
        out_shape=out_shape,
        compiler_params=_cparams(("arbitrary",)),
        name="proj",
    )(x2d, norm_g, w_r)


def _compress_hidden(load_rows, pe_ref, w1_ref, kv, m):
    xs = [load_rows(s) for s in range(CMP_STRIDE)]
    hs = []
    for sub in range(CMP_BLOCK // CMP_STRIDE):
        lhs = jnp.concatenate([(xs[s] + pe_ref[kv, sub, s]).astype(BF16) for s in range(CMP_STRIDE)], axis=1)
        hs.append(jnp.dot(lhs, w1_ref[kv, sub], preferred_element_type=F32))
    return hs[0] + pltpu.roll(hs[1], m - 1, 0)


def _cover_matrix(nc_pad, ns_pad):
    c0 = lax.broadcasted_iota(jnp.int32, (nc_pad, ns_pad), 0) * CMP_STRIDE
    s0 = lax.broadcasted_iota(jnp.int32, (nc_pad, ns_pad), 1) * SEL_BLOCK
    return jnp.where((c0 < s0 + SEL_BLOCK) & (c0 + CMP_BLOCK > s0), 1.0, 0.0).astype(F32)


def _compress_prompt_kernel(c_ref, pe_ref, w1_ref, w2_ref, kc_ref, rhs_ref, *, nchunk):
    ones_rows = jnp.where(lax.broadcasted_iota(jnp.int32, (HEAD_DIM, nchunk), 0) == 0, 1.0, 0.0).astype(BF16)
    s0 = lax.broadcasted_iota(jnp.int32, (LANES, nchunk), 0) * SEL_BLOCK
    c0 = lax.broadcasted_iota(jnp.int32, (LANES, nchunk), 1) * CMP_STRIDE
    cover_t = jnp.where((c0 < s0 + SEL_BLOCK) & (c0 + CMP_BLOCK > s0), 1.0, 0.0).astype(BF16)
    for kv in range(2):
        load = lambda s: c_ref[kv, pl.ds(s, nchunk, stride=CMP_STRIDE), :]
        h = _compress_hidden(load, pe_ref, w1_ref, kv, nchunk)
        full = jnp.dot(_silu(h).astype(BF16), w2_ref[kv], preferred_element_type=F32)
        if kv == 0:
            kc_ref[0, :, :] = full[:, 0:HEAD_DIM].astype(BF16)
            kc_ref[1, :, :] = full[:, HEAD_DIM:2 * HEAD_DIM].astype(BF16)
        else:
            full_t = full.T
            for kvh in range(N_KV_HEADS):
                rhs_ref[kvh, 0:HEAD_DIM, :] = full_t[kvh * HEAD_DIM:(kvh + 1) * HEAD_DIM, :].astype(BF16)
                rhs_ref[kvh, HEAD_DIM:LANES, :] = ones_rows
                rhs_ref[kvh, LANES:2 * LANES, :] = cover_t


def _compress_prompt(crm, n, pe_t, w1bd, w2bd):
    seq_len = crm.shape[1] // n
    nchunk = seq_len // CMP_STRIDE
    return pl.pallas_call(
        functools.partial(_compress_prompt_kernel, nchunk=nchunk),
        grid=(n,),
        in_specs=[pl.BlockSpec((2, seq_len, LANES), lambda b: (0, b, 0)),
                  pl.BlockSpec(pe_t.shape, lambda b: (0, 0, 0, 0, 0)),
                  pl.BlockSpec(w1bd.shape, lambda b: (0, 0, 0, 0)),
                  pl.BlockSpec(w2bd.shape, lambda b: (0, 0, 0))],
        out_specs=(pl.BlockSpec((None, N_KV_HEADS, nchunk, HEAD_DIM), lambda b: (b, 0, 0, 0)),
                   pl.BlockSpec((None, N_KV_HEADS, 2 * LANES, nchunk), lambda b: (b, 0, 0, 0))),
        out_shape=(jax.ShapeDtypeStruct((n, N_KV_HEADS, nchunk, HEAD_DIM), BF16),
                   jax.ShapeDtypeStruct((n, N_KV_HEADS, 2 * LANES, nchunk), BF16)),
        compiler_params=_cparams(("arbitrary",)),
        name="compress_prompt",
    )(crm, pe_t, w1bd, w2bd)


def _not_selected(val, n_top, axis):
    blk = lax.broadcasted_iota(jnp.int32, val.shape, axis).astype(F32)
    forced = val >= FORCED_VALUE
    notsel = jnp.where(forced, 0.0, 1.0).astype(F32)
    val = jnp.where(forced, -3e38, val)
    for _ in range(max(n_top - N_FORCED, 0)):
        m = jnp.max(val, axis=axis, keepdims=True)
        idx = jnp.min(jnp.where(val == m, blk, float(LANES)), axis=axis, keepdims=True)
        pick = blk == idx
        notsel = jnp.where(pick, 0.0, notsel)
        val = jnp.where(pick, -3e38, val)
    return notsel


def _masked_importance(imp, t, n_blocks, axis):
    blk = lax.broadcasted_iota(jnp.int32, imp.shape, axis)
    cur = t // SEL_BLOCK
    forced = (blk == 0) | (blk == cur) | (blk == cur - 1)
    val = jnp.where(forced, FORCED_VALUE, jnp.where(blk * SEL_BLOCK <= t, imp, -1e9))
    return jnp.where(blk < n_blocks, val, -2e9)


def _attn_prompt_kernel(qt_ref, gatet_ref, sza_ref, kc_ref, rhs_ref, ks_ref, vst_ref, kw_ref, vwt_ref, out_ref,
                        qsel_ref, sbuf_ref, smax_ref, m_ref, acc_ref, mw_ref, accw_ref, oc_ref,
                        *, n_sel_blocks, seq_len):
    qb = pl.program_id(2)
    qs = qb * Q_BLOCK
    hq = GQA * Q_BLOCK
    t_row = qs + lax.broadcasted_iota(jnp.int32, (1, Q_BLOCK), 1)
    t_all = jnp.concatenate([t_row] * GQA, axis=1)
    qt = qt_ref[...]
    qt_all = jnp.concatenate([qt[g * HEAD_DIM:(g + 1) * HEAD_DIM, :] for g in range(GQA)], axis=1)
    zeros_lo = jnp.zeros((HEAD_DIM, hq), BF16)
    qw = jnp.concatenate([qt_all, zeros_lo], axis=0)

    tile = min(KEY_TILE, seq_len)
    n_full = qs // tile
    lo_tiles = HEAD_DIM * SEL_BLOCK // tile
    wwidth = min(WINDOW, seq_len)
    causal = lambda kpos: kpos <= t_all
    band = lambda kpos: (kpos > t_all - WINDOW) & (kpos <= t_all)
    diag_w = lambda kpos: (kpos <= t_all) & (qs >= wwidth)

    def flash_steps(steps):
        staged = []
        for (q, k_ref, vt_ref, start, width, mask_fn, mref, aref, c) in steps:
            start = pl.multiple_of(start, LANES)
            s = jnp.dot(k_ref[pl.ds(start, width), :], q, preferred_element_type=F32)
            if mask_fn is not None:
                s = jnp.where(mask_fn(start + lax.broadcasted_iota(jnp.int32, (width, hq), 0)), s, NEG_FILL)
            staged.append((s, vt_ref[:, pl.ds(start, width)]))
        softmaxed = []
        for (s, vt), step in zip(staged, steps):
            mref, c = step[6], step[8]
            m_old = mref[c]
            m_new = jnp.maximum(m_old, jnp.max(s, axis=0, keepdims=True))
            softmaxed.append((jnp.exp2(s - m_new).astype(BF16), jnp.exp2(m_old - m_new), m_new, vt))
        for (p, alpha, m_new, vt), step in zip(softmaxed, steps):
            mref, aref, c = step[6], step[7], step[8]
            aref[c] = alpha * aref[c] + jnp.dot(vt, p, preferred_element_type=F32)
            mref[c] = m_new

    m_ref[...] = jnp.full(m_ref.shape, NEG_FILL, F32)
    acc_ref[...] = jnp.zeros(acc_ref.shape, F32)
    mw_ref[...] = jnp.full(mw_ref.shape, NEG_FILL, F32)
    accw_ref[...] = jnp.zeros(accw_ref.shape, F32)

    nc = kc_ref.shape[0]
    c_last = lax.broadcasted_iota(jnp.int32, (nc, hq), 0) * CMP_STRIDE + (CMP_BLOCK - 1)
    ok_c = c_last <= t_all
    s = jnp.dot(kc_ref[...], qt_all, preferred_element_type=F32)
    flash_steps([(qw, kw_ref, vwt_ref, qs, Q_BLOCK, diag_w, mw_ref, accw_ref, 1)])
    s = jnp.where(ok_c, s, NEG_FILL)
    e = jnp.exp2(s - jnp.max(s, axis=0, keepdims=True))
    r = jnp.dot(rhs_ref[...], e.astype(BF16), preferred_element_type=F32)
    inv = jnp.where(t_all >= CMP_BLOCK - 1, 1.0 / r[HEAD_DIM:HEAD_DIM + 1, :], 0.0)
    qw_late = jnp.where(inv > -1.0, qw, jnp.zeros_like(qw))
    flash_steps([(qw_late, kw_ref, vwt_ref, jnp.maximum(qs - wwidth, 0), wwidth, band, mw_ref, accw_ref, 0)])
    oc_ref[...] = r[0:HEAD_DIM, :] * inv
    imp_all = r[LANES:2 * LANES, :] * inv
    imp = imp_all[:, 0:Q_BLOCK]
    for g in range(1, GQA):
        imp = imp + imp_all[:, g * Q_BLOCK:(g + 1) * Q_BLOCK]

    notsel = _not_selected(_masked_importance(imp, t_row, n_sel_blocks, 0), min(N_SEL, n_sel_blocks), 0)
    notsel = notsel.astype(BF16)
    q_lo = jnp.concatenate([qt_all, jnp.concatenate([notsel[0:HEAD_DIM, :]] * GQA, axis=1)], axis=0)
    q_hi = jnp.concatenate([qt_all, jnp.concatenate([notsel[HEAD_DIM:2 * HEAD_DIM, :]] * GQA, axis=1)], axis=0)
    qsel_ref[0] = q_lo
    qsel_ref[1] = q_hi

    def scores(b, kt, masked):
        start = pl.multiple_of(kt * tile, LANES)
        q = qsel_ref[jnp.where(kt >= lo_tiles, 1, 0)]
        s = jnp.dot(ks_ref[pl.ds(start, tile), :], q, preferred_element_type=F32)
        if masked:
            s = jnp.where(causal(start + lax.broadcasted_iota(jnp.int32, (tile, hq), 0)), s, NEG_FILL)
        sbuf_ref[b] = s
        smax_ref[b] = jnp.max(s, axis=0, keepdims=True)

    def consume(b, kt):
        start = pl.multiple_of(kt * tile, LANES)
        m_old = m_ref[b]
        m_new = jnp.maximum(m_old, smax_ref[b])
        p = jnp.exp2(sbuf_ref[b] - m_new).astype(BF16)
        acc_ref[b] = jnp.exp2(m_old - m_new) * acc_ref[b] + jnp.dot(vst_ref[:, pl.ds(start, tile)], p,
                                                                  preferred_element_type=F32)
        m_ref[b] = m_new

    @pl.when(n_full == 0)
    def _():
        scores(0, 0, True)

    @pl.when(n_full > 0)
    def _():
        scores(0, 0, False)

    def pair_step(j):
        scores(1, 2 * j + 1, False)
        consume(0, 2 * j)
        scores(0, 2 * j + 2, False)
        consume(1, 2 * j + 1)

    def pair_body(j, carry):
        pair_step(j)
        return carry

    def quad_body(j, carry):
        pair_step(2 * j)
        pair_step(2 * j + 1)
        return carry

    n_pairs = jnp.maximum(n_full - 1, 0) // 2
    lax.fori_loop(0, n_pairs // 2, quad_body, 0)
    lax.fori_loop(n_pairs // 2 * 2, n_pairs, pair_body, 0)

    @pl.when((n_full > 0) & (n_full % 2 == 0))
    def _():
        scores(1, n_full - 1, False)
        consume(0, n_full - 2)
        scores(0, n_full, True)
        consume(1, n_full - 1)
        consume(0, n_full)

    @pl.when(n_full % 2 == 1)
    def _():
        scores(1, n_full, True)
        consume(0, n_full - 1)
        consume(1, n_full)

    @pl.when(n_full == 0)
    def _():
        consume(0, 0)

    def merged(mref, aref, n_chain):
        m = mref[0]
        for c in range(1, n_chain):
            m = jnp.maximum(m, mref[c])
        a = jnp.exp2(mref[0] - m) * aref[0]
        for c in range(1, n_chain):
            a = a + jnp.exp2(mref[c] - m) * aref[c]
        return a[0:HEAD_DIM, :] * (1.0 / a[HEAD_DIM:HEAD_DIM + 1, :])

    gates = gatet_ref[...]
    sza = sza_ref[...]
    o_s = merged(m_ref, acc_ref, SEL_CHAINS)
    o_w = merged(mw_ref, accw_ref, 2)
    o_c = oc_ref[...]
    outs = []
    for g in range(GQA):
        cols = slice(g * Q_BLOCK, (g + 1) * Q_BLOCK)
        gc = gates[g * N_BRANCH + 0:g * N_BRANCH + 1, :]
        gs = gates[g * N_BRANCH + 1:g * N_BRANCH + 2, :]
        gw = gates[g * N_BRANCH + 2:g * N_BRANCH + 3, :]
        outs.append((gc * o_c[:, cols] + gs * o_s[:, cols] + gw * o_w[:, cols]).T)
    out_ref[...] = (jnp.concatenate(outs, axis=1) * sza).astype(BF16)


def _attn_prompt(qt, gates_t, sza3, kc, rhs_c, ks, vst, kw, vwt):
    n, seq_len, _ = sza3.shape
    nqb = seq_len // Q_BLOCK
    nchunk = kc.shape[2]
    hw = GQA * HEAD_DIM
    k_blk = pl.BlockSpec((None, seq_len, LANES), lambda b, k, i: (k, b, 0))
    vt_blk = pl.BlockSpec((None, V_ROWS, seq_len), lambda b, k, i: (k, 0, b))
    return pl.pallas_call(
        functools.partial(_attn_prompt_kernel, n_sel_blocks=-(-seq_len // SEL_BLOCK), seq_len=seq_len),
        grid=(n, N_KV_HEADS, nqb),
        in_specs=[
            pl.BlockSpec((hw, Q_BLOCK), lambda b, k, i: (k, b * nqb + i)),
            pl.BlockSpec((None, GATE_ROWS, Q_BLOCK), lambda b, k, i: (k, 0, b * nqb + i)),
            pl.BlockSpec((None, Q_BLOCK, hw), lambda b, k, i: (b, i, k)),
            pl.BlockSpec((None, None, nchunk, HEAD_DIM), lambda b, k, i: (b, k, 0, 0)),
            pl.BlockSpec((None, None, 2 * LANES, nchunk), lambda b, k, i: (b, k, 0, 0)),
            k_blk, vt_blk, k_blk, vt_blk,
        ],
        out_specs=pl.BlockSpec((None, Q_BLOCK, hw), lambda b, k, i: (b, i, k)),
        out_shape=jax.ShapeDtypeStruct((n, seq_len, D_ATTN), BF16),
        scratch_shapes=[
            pltpu.VMEM((2, 2 * HEAD_DIM, GQA * Q_BLOCK), BF16),
            pltpu.VMEM((SEL_CHAINS, min(KEY_TILE, seq_len), GQA * Q_BLOCK), F32),
            pltpu.VMEM((SEL_CHAINS, 1, GQA * Q_BLOCK), F32),
            pltpu.VMEM((SEL_CHAINS, 1, GQA * Q_BLOCK), F32),
            pltpu.VMEM((SEL_CHAINS, V_ROWS, GQA * Q_BLOCK), F32),
            pltpu.VMEM((2, 1, GQA * Q_BLOCK), F32),
            pltpu.VMEM((2, V_ROWS, GQA * Q_BLOCK), F32),
            pltpu.VMEM((HEAD_DIM, GQA * Q_BLOCK), F32),
        ],
        compiler_params=_cparams(("arbitrary", "arbitrary", "arbitrary")),
        name="attn_prompt",
    )(qt, gates_t, sza3, kc, rhs_c, ks, vst, kw, vwt)


def _pool_out(d, pw_ref, ps_ref, szp):
    ys = [jnp.dot(d[:, g * POOL_GROUP_DIM:(g + 1) * POOL_GROUP_DIM].astype(BF16), pw_ref[g],
                  preferred_element_type=F32) for g in range(len(POOL_WINDOWS))]
    return jnp.concatenate(ys, axis=1) * ps_ref[...] * szp


def _finish(x, a_bf16, b, wo_ref, fg_ref):
    mix = jnp.concatenate([a_bf16, b.astype(BF16)], axis=1)
    y = x + jnp.dot(mix, wo_ref[...], preferred_element_type=F32)
    ms = jnp.mean(y * y, axis=-1, keepdims=True)
    return y * lax.rsqrt(ms + RMS_EPS) * fg_ref[...]


def _out_prompt_kernel(a_ref, u_ref, halo_ref, szp_ref, x_ref, pw_ref, ps_ref, wo_ref, fg_ref, y_ref, ext_ref,
                       *, tm, halo):
    i = pl.program_id(1)
    u = u_ref[...]
    ext_ref[0:halo, :] = jnp.where(i > 0, halo_ref[...], 0.0)
    ext_ref[halo:halo + tm, :] = u
    pos = i * tm + lax.broadcasted_iota(jnp.int32, (tm, POOL_GROUP_DIM), 0)
    ds = []
    for g, w in enumerate(POOL_WINDOWS):
        c0 = g * POOL_GROUP_DIM
        acc = u[:, c0:c0 + POOL_GROUP_DIM]
        for k in range(1, w):
            acc = acc + ext_ref[halo - k:halo - k + tm, c0:c0 + POOL_GROUP_DIM]
        cnt = jnp.minimum(pos + 1, w).astype(F32)
        ds.append(acc / cnt - u[:, c0:c0 + POOL_GROUP_DIM])
    b = _pool_out(jnp.concatenate(ds, axis=1), pw_ref, ps_ref, szp_ref[...])
    y_ref[...] = _finish(x_ref[...], a_ref[...], b, wo_ref, fg_ref)


def _out_prompt(a3, u3, szp3, x3, pool_w, pool_scale, w_out, final_g, *, tm):
    n, seq_len, d_model = x3.shape
    halo = 16
    nt = seq_len // tm
    blk = lambda w: pl.BlockSpec((None, tm, w), lambda b, i: (b, i, 0))
    const = lambda a: pl.BlockSpec(a.shape, lambda b, i: (0,) * a.ndim)
    return pl.pallas_call(
        functools.partial(_out_prompt_kernel, tm=tm, halo=halo),
        grid=(n, nt),
        in_specs=[blk(D_ATTN), blk(D_POOL),
                  pl.BlockSpec((None, halo, D_POOL), lambda b, i: (b, jnp.maximum(i * (tm // halo) - 1, 0), 0)),
                  blk(D_POOL), blk(d_model),
                  const(pool_w), const(pool_scale), const(w_out), const(final_g)],
        out_specs=blk(d_model),
        out_shape=jax.ShapeDtypeStruct((n, seq_len, d_model), F32),
        scratch_shapes=[pltpu.VMEM((tm + halo, D_POOL), F32)],
        compiler_params=_cparams(("arbitrary", "arbitrary")),
        name="out_prompt",
    )(a3, u3, u3, szp3, x3, pool_w, pool_scale, w_out, final_g)


def _out_sample_kernel(o_ref, sza_ref, u_ref, st_ref, szp_ref, x_ref, pw_ref, ps_ref, wo_ref, fg_ref, y_ref):
    u = u_ref[...]
    ds = []
    for g, w in enumerate(POOL_WINDOWS):
        c0 = g * POOL_GROUP_DIM
        acc = u[:, c0:c0 + POOL_GROUP_DIM]
        for k in range(1, w):
            acc = acc + st_ref[POOL_STATE - k, :, c0:c0 + POOL_GROUP_DIM]
        ds.append(acc / float(w) - u[:, c0:c0 + POOL_GROUP_DIM])
    b = _pool_out(jnp.concatenate(ds, axis=1), pw_ref, ps_ref, szp_ref[...])
    a = (o_ref[...] * sza_ref[...]).astype(BF16)
    y_ref[...] = _finish(x_ref[...], a, b, wo_ref, fg_ref)


def _out_sample(o2, sza, u, state_t, szp, x2, pool_w, pool_scale, w_out, final_g):
    args = (o2, sza, u, state_t, szp, x2, pool_w, pool_scale, w_out, final_g)
    full = lambda a: pl.BlockSpec(a.shape, lambda i: (0,) * a.ndim)
    return pl.pallas_call(
        _out_sample_kernel,
        grid=(1,),
        in_specs=[full(a) for a in args],
        out_specs=full(x2),
        out_shape=jax.ShapeDtypeStruct(x2.shape, F32),
        compiler_params=_cparams(("arbitrary",)),
        name="out_sample",
    )(*args)


def _page_copy(pt_ref, cache_ref, buf_ref, sem_ref, step, slot, j, n_pages, n_seq):
    g, p = divmod(j, n_pages)
    return pltpu.make_async_copy(cache_ref.at[pt_ref[step * n_seq + g, p]], buf_ref.at[slot, j], sem_ref.at[slot])


def _fetch_pages(pt_ref, cache_ref, buf_ref, sem_ref, n_pages, n_seq):
    i = pl.program_id(0)
    slot = i % 2
    n_copy = n_seq * n_pages

    @pl.when(i == 0)
    def _():
        for j in range(n_copy):
            _page_copy(pt_ref, cache_ref, buf_ref, sem_ref, 0, 0, j, n_pages, n_seq).start()

    @pl.when(i + 1 < pl.num_programs(0))
    def _():
        for j in range(n_copy):
            _page_copy(pt_ref, cache_ref, buf_ref, sem_ref, i + 1, 1 - slot, j, n_pages, n_seq).start()

    for j in range(n_copy):
        _page_copy(pt_ref, cache_ref, buf_ref, sem_ref, i, slot, j, n_pages, n_seq).wait()
    return slot


def _cmp_sample_kernel(pt_ref, cache_ref, qbd_ref, pe_ref, w1_ref, w2_ref, oc_ref, imp_ref, rows_ref, buf_ref,
                       sem_ref, *, n_pages, page, n_seq):
    slot = _fetch_pages(pt_ref, cache_ref, buf_ref, sem_ref, n_pages, n_seq)
    pages = [buf_ref.at[slot, j] for j in range(n_seq * n_pages)]
    nchunk = n_pages * page // CMP_STRIDE
    fulls = []
    m = n_seq * nchunk
    cpp = page // CMP_STRIDE
    for kv in range(2):
        for i, page_ref in enumerate(pages):
            half = page_ref[kv * LANES:(kv + 1) * LANES, :].astype(BF16)
            rows = half.T.astype(F32)
            for c in range(cpp):
                dst = (i * cpp + c) * CHUNK_PITCH
                rows_ref[kv, dst:dst + CMP_STRIDE, :] = rows[c * CMP_STRIDE:(c + 1) * CMP_STRIDE, :]
        load = lambda s: rows_ref[kv, pl.ds(s, m, stride=CHUNK_PITCH), :]
        h = _compress_hidden(load, pe_ref, w1_ref, kv, m)
        fulls.append(jnp.dot(_silu(h).astype(BF16), w2_ref[kv], preferred_element_type=F32).astype(BF16))
    k_c, v_c = fulls
    nrow = n_seq * N_HEADS
    qbd = qbd_ref[...].reshape(nrow, LANES).astype(BF16)
    s = lax.dot_general(qbd, k_c, (((1,), (1,)), ((), ())), preferred_element_type=F32)
    col = lax.broadcasted_iota(jnp.int32, s.shape, 1)
    own = col // nchunk == lax.broadcasted_iota(jnp.int32, s.shape, 0) // N_HEADS
    ok = own & (col % nchunk < nchunk - 1)
    s = jnp.where(ok, s, NEG_FILL)
    e = jnp.where(ok, jnp.exp2(s - jnp.max(s, axis=1, keepdims=True)), 0.0)
    pc = (e * (1.0 / jnp.sum(e, axis=1, keepdims=True))).astype(BF16)
    cover = jnp.concatenate([_cover_matrix(nchunk, LANES).astype(BF16)] * n_seq, axis=0)
    oc_ref[...] = jnp.dot(pc, v_c, preferred_element_type=F32).reshape(n_seq, N_HEADS, LANES)
    imp_ref[...] = jnp.dot(pc, cover, preferred_element_type=F32).reshape(n_seq, N_HEADS, LANES)


def _cmp_sample(page_table, cache_t, qbd, pe_t, w1bd, w2bd):
    nb, n_pages = page_table.shape
    page = cache_t.shape[-1]
    n_seq = SAMPLE_GROUP
    const = lambda a: pl.BlockSpec(a.shape, lambda b, pt: (0,) * a.ndim)
    per_b = pl.BlockSpec((n_seq, N_HEADS, LANES), lambda b, pt: (b, 0, 0))
    grid_spec = pltpu.PrefetchScalarGridSpec(
        num_scalar_prefetch=1,
        grid=(nb // n_seq,),
        in_specs=[pl.BlockSpec(memory_space=pl.ANY), per_b, const(pe_t), const(w1bd), const(w2bd)],
        out_specs=(per_b, per_b),
        scratch_shapes=[pltpu.VMEM((2, n_seq * n_pages * page // CMP_STRIDE * CHUNK_PITCH, LANES), F32),
                        pltpu.VMEM((2, n_seq * n_pages, KV_W, page), F32),
                        pltpu.SemaphoreType.DMA((2,))],
    )
    return pl.pallas_call(
        functools.partial(_cmp_sample_kernel, n_pages=n_pages, page=page, n_seq=n_seq),
        grid_spec=grid_spec,
        out_shape=(jax.ShapeDtypeStruct((nb, N_HEADS, LANES), F32),
                   jax.ShapeDtypeStruct((nb, N_HEADS, LANES), F32)),
        compiler_params=_cparams(("arbitrary",)),
        name="cmp_sample",
    )(page_table, cache_t, qbd, pe_t, w1bd, w2bd)


def _topk_sample_kernel(imp_ref, out_ref, *, t_pos, n_blocks):
    imp = imp_ref[0] + imp_ref[1] + imp_ref[2] + imp_ref[3]
    t = jnp.full((imp.shape[0], 1), t_pos, jnp.int32)
    out_ref[...] = _not_selected(_masked_importance(imp, t, n_blocks, 1), min(N_SEL, n_blocks), 1)


def _topk_sample(imp_g, *, t_pos, n_blocks):
    rows = imp_g.shape[1]
    return pl.pallas_call(
        functools.partial(_topk_sample_kernel, t_pos=t_pos, n_blocks=n_blocks),
        grid=(1,),
        in_specs=[pl.BlockSpec(imp_g.shape, lambda i: (0, 0, 0))],
        out_specs=pl.BlockSpec((rows, LANES), lambda i: (0, 0)),
        out_shape=jax.ShapeDtypeStruct((rows, LANES), F32),
        compiler_params=_cparams(("arbitrary",)),
        name="topk_sample",
    )(imp_g)


def _attn_sample_kernel(pt_ref, *refs, n_pages, page, n_seq):
    (cache_ref, win_ref, qbd_ref, ns_ref, snew_ref, wnew_ref, oc_ref, gate_ref, o_ref, nwin_ref,
     buf_ref, sem_ref) = refs
    slot = _fetch_pages(pt_ref, cache_ref, buf_ref, sem_ref, n_pages, n_seq)
    pages = [buf_ref.at[slot, j] for j in range(n_seq * n_pages)]
    nrow = n_seq * N_HEADS
    qf = qbd_ref[...].reshape(nrow, LANES)
    qbd = qf.astype(BF16)
    ns = ns_ref[...].reshape(nrow, LANES)
    row_seq = lax.broadcasted_iota(jnp.int32, (nrow, 1), 0) // N_HEADS
    nt_ = (((1,), (1,)), ((), ()))

    def new_token(rows_ref):
        rows = jnp.concatenate([jnp.broadcast_to(rows_ref[g], (N_HEADS, KV_W)) for g in range(n_seq)], axis=0)
        k_new = rows[:, 0:LANES].astype(BF16).astype(F32)
        v_new = rows[:, LANES:2 * LANES].astype(BF16).astype(F32)
        return jnp.sum(qf * k_new, axis=1, keepdims=True), v_new

    lane = lax.broadcasted_iota(jnp.int32, (nrow, page), 1)
    per_page = page // SEL_BLOCK
    scores = []
    for i, page_ref in enumerate(pages):
        g, p = divmod(i, n_pages)
        s = jnp.dot(qbd, page_ref[0:LANES, :].astype(BF16), preferred_element_type=F32)
        flag = ns[:, p * per_page:p * per_page + 1]
        for j in range(1, per_page):
            flag = jnp.where(lane >= j * SEL_BLOCK, ns[:, p * per_page + j:p * per_page + j + 1], flag)
        scores.append(jnp.where((row_seq == g) & (flag < 0.5), s, NEG_FILL))
    nb_cache = n_pages * per_page
    s_new, v_new = new_token(snew_ref)
    s_new = jnp.where(ns[:, nb_cache:nb_cache + 1] > 0.5, NEG_FILL, s_new)
    s_max = scores[0]
    for s in scores[1:]:
        s_max = jnp.maximum(s_max, s)
    m = jnp.maximum(s_new, jnp.max(s_max, axis=1, keepdims=True))
    e_new = jnp.exp2(s_new - m)
    e_sum = jnp.zeros((nrow, page), F32)
    acc = e_new * v_new
    for s, page_ref in zip(scores, pages):
        e = jnp.exp2(s - m).astype(BF16)
        e_sum = e_sum + e.astype(F32)
        acc = acc + lax.dot_general(e, page_ref[LANES:2 * LANES, :].astype(BF16), nt_, preferred_element_type=F32)
    o_s = acc * (1.0 / (e_new + jnp.sum(e_sum, axis=1, keepdims=True)))

    wlen = win_ref.shape[2]
    first = max(wlen - WINDOW + 1, 0)
    in_win = lax.broadcasted_iota(jnp.int32, (nrow, wlen), 1) >= first
    scores_w = []
    for g in range(n_seq):
        s_w = jnp.dot(qbd, win_ref[g, 0:LANES, :].astype(BF16), preferred_element_type=F32)
        scores_w.append(jnp.where((row_seq == g) & in_win, s_w, NEG_FILL))
    sw_new, vw_new = new_token(wnew_ref)
    sw_max = scores_w[0]
    for s_w in scores_w[1:]:
        sw_max = jnp.maximum(sw_max, s_w)
    m_w = jnp.maximum(sw_new, jnp.max(sw_max, axis=1, keepdims=True))
    ew_new = jnp.exp2(sw_new - m_w)
    ew_sum = jnp.zeros((nrow, wlen), F32)
    acc_w = ew_new * vw_new
    for g in range(n_seq):
        e_w = jnp.exp2(scores_w[g] - m_w).astype(BF16)
        ew_sum = ew_sum + e_w.astype(F32)
        acc_w = acc_w + lax.dot_general(e_w, win_ref[g, LANES:2 * LANES, :].astype(BF16), nt_,
                                        preferred_element_type=F32)
    o_w = acc_w * (1.0 / (ew_new + jnp.sum(ew_sum, axis=1, keepdims=True)))

    gates = gate_ref[...].reshape(nrow, LANES)
    o = gates[:, 0:1] * oc_ref[...].reshape(nrow, LANES) + gates[:, 1:2] * o_s + gates[:, 2:3] * o_w
    o_ref[...] = o.reshape(n_seq, N_HEADS, LANES)

    last = lax.broadcasted_iota(jnp.int32, (KV_W, LANES), 1) == LANES - 1
    for g in range(n_seq):
        shifted = pltpu.roll(win_ref[g], wlen - 1, 1)
        new_col = jnp.broadcast_to(wnew_ref[g], (LANES, KV_W)).T
        nwin_ref[g, :, 0:wlen - LANES] = shifted[:, 0:wlen - LANES]
        nwin_ref[g, :, wlen - LANES:wlen] = jnp.where(last, new_col, shifted[:, wlen - LANES:wlen])


def _attn_sample(page_table, cache_t, win_t, qbd, ns8, s_new, w_new, o_c, gates8):
    nb, n_pages = page_table.shape
    page = cache_t.shape[-1]
    n_seq = SAMPLE_GROUP
    per_b = lambda a: pl.BlockSpec((n_seq,) + a.shape[1:], lambda b, pt: (b,) + (0,) * (a.ndim - 1))
    grid_spec = pltpu.PrefetchScalarGridSpec(
        num_scalar_prefetch=1,
        grid=(nb // n_seq,),
        in_specs=[pl.BlockSpec(memory_space=pl.ANY)]
        + [per_b(a) for a in (win_t, qbd, ns8, s_new, w_new, o_c, gates8)],
        out_specs=(per_b(o_c), per_b(win_t)),
        scratch_shapes=[pltpu.VMEM((2, n_seq * n_pages, KV_W, page), F32),
                        pltpu.SemaphoreType.DMA((2,))],
    )
    return pl.pallas_call(
        functools.partial(_attn_sample_kernel, n_pages=n_pages, page=page, n_seq=n_seq),
        grid_spec=grid_spec,
        out_shape=(jax.ShapeDtypeStruct(o_c.shape, F32), jax.ShapeDtypeStruct(win_t.shape, F32)),
        compiler_params=_cparams(("arbitrary",)),
        name="attn_sample",
    )(page_table, cache_t, win_t, qbd, ns8, s_new, w_new, o_c, gates8)


def _prep_w_in(w_in):
    w_t = w_in.T
    o = D_ATTN + 3 * KV_W
    ng = GQA * N_BRANCH
    gl = w_t[o:o + N_KV_HEADS * ng]
    rest = w_t[o + N_KV_HEADS * ng:]
    pad = jnp.zeros((LANES - ng, w_t.shape[1]), w_t.dtype)
    return jnp.concatenate([w_t[:o], rest, gl[:ng], pad, gl[ng:], pad], axis=0).astype(BF16)


def _prep_compress(cmp_pe, cmp_w1, cmp_w2):
    r = CMP_BLOCK // CMP_STRIDE
    pe_t = jnp.tile(cmp_pe.reshape(2, r, CMP_STRIDE, 1, HEAD_DIM), (1, 1, 1, 1, N_KV_HEADS))
    w1 = cmp_w1.reshape(2, r, CMP_STRIDE, HEAD_DIM, CMP_HIDDEN)
    z1 = jnp.zeros_like(w1)
    top = jnp.concatenate([w1, z1], axis=-1)
    bot = jnp.concatenate([z1, w1], axis=-1)
    w1bd = jnp.concatenate([top, bot], axis=3)
    w1bd = w1bd.reshape(2, r, CMP_STRIDE * LANES, N_KV_HEADS * CMP_HIDDEN).astype(BF16)
    z2 = jnp.zeros_like(cmp_w2)
    w2bd = jnp.concatenate([jnp.concatenate([cmp_w2, z2], axis=-1),
                            jnp.concatenate([z2, cmp_w2], axis=-1)], axis=1).astype(BF16)
    return pe_t, w1bd, w2bd


def _kv_out(kv_t):
    n, _, rows = kv_t.shape
    return jnp.transpose(kv_t.reshape(n, 2, N_KV_HEADS, HEAD_DIM, rows), (0, 4, 1, 2, 3))[None]


def kernel(x_prompt, x_sample, cache_cmp_kv, cache_slc_kv, cache_win_kv, state_pool, page_table, norm_g, w_in,
           cmp_pe, cmp_w1, cmp_w2, pool_w, pool_scale, w_out, final_g):
    n, seq_len, d_model = x_prompt.shape
    nb = x_sample.shape[0]
    n_phys, page = cache_cmp_kv.shape[1], cache_cmp_kv.shape[2]
    n_pages = page_table.shape[1]
    past_len = n_pages * page
    wlen = cache_win_kv.shape[2]

    w_r = _prep_w_in(w_in[0])
    pe_t, w1bd, w2bd = _prep_compress(cmp_pe[0], cmp_w1[0], cmp_w2[0])
    pool_w_b = pool_w[0].astype(BF16)
    w_out_b = w_out[0].astype(BF16)
    fg = final_g.reshape(1, d_model)

    (qt, crm, ckv_t, skv_t, wkv_t, ks, vst, kw, vwt, gates_t, sza, u, szp) = _project(
        x_prompt.reshape(n * seq_len, d_model), norm_g, w_r, tm=512, seq_len=seq_len)
    r3 = lambda a: a.reshape(n, seq_len, a.shape[-1])
    kc, rhs_c = _compress_prompt(crm, n, pe_t, w1bd, w2bd)
    a3 = _attn_prompt(qt, gates_t, r3(sza), kc, rhs_c, ks, vst, kw, vwt)
    y_prompt = _out_prompt(a3, r3(u), r3(szp), x_prompt, pool_w_b, pool_scale, w_out_b, fg, tm=512)

    new_cmp_p = _kv_out(ckv_t)
    new_slc_p = _kv_out(skv_t)
    new_win_p = _kv_out(wkv_t[:, :, seq_len - min(WINDOW, seq_len):])
    new_pool_p = r3(u)[:, seq_len - POOL_STATE:][None]

    (qt_s, _, ckv_ts, skv_ts, wkv_ts, _, _, _, _, gates_ts, sza_s, u_s, szp_s) = _project(
        x_sample.reshape(nb, d_model), norm_g, w_r, tm=nb, seq_len=nb)
    q_s = qt_s.T
    q5 = q_s.reshape(nb, N_KV_HEADS, GQA, 1, HEAD_DIM)
    eye = jnp.eye(N_KV_HEADS, dtype=q_s.dtype).reshape(1, N_KV_HEADS, 1, N_KV_HEADS, 1)
    qbd = (q5 * eye).reshape(nb, N_HEADS, LANES).astype(F32)

    to_pages = lambda c: jnp.transpose(c[0], (0, 2, 3, 4, 1)).reshape(n_phys, KV_W, page)
    o_c, imp8 = _cmp_sample(page_table, to_pages(cache_cmp_kv), qbd, pe_t, w1bd, w2bd)
    imp_g = jnp.transpose(imp8.reshape(nb * N_KV_HEADS, GQA, LANES), (1, 0, 2))
    n_blocks = -(-(past_len + 1) // SEL_BLOCK)
    notsel = _topk_sample(imp_g, t_pos=past_len, n_blocks=n_blocks)
    ns8 = jnp.repeat(notsel.reshape(nb, N_KV_HEADS, 1, LANES), GQA, axis=2).reshape(nb, N_HEADS, LANES)

    gates8 = jnp.transpose(gates_ts[:, :GQA * N_BRANCH, :], (2, 0, 1)).reshape(nb, N_HEADS, N_BRANCH)
    gates8 = jnp.pad(gates8, ((0, 0), (0, 0), (0, LANES - N_BRANCH)))
    win_t = jnp.transpose(cache_win_kv[0], (0, 2, 3, 4, 1)).reshape(nb, KV_W, wlen)
    s_new = skv_ts[0].T.reshape(nb, 1, KV_W)
    w_new = wkv_ts[0].T.reshape(nb, 1, KV_W)
    o8, nwin_t = _attn_sample(page_table, to_pages(cache_slc_kv), win_t, qbd, ns8, s_new, w_new, o_c, gates8)
    o5 = o8.reshape(nb, N_KV_HEADS, GQA, N_KV_HEADS, HEAD_DIM)
    o2 = jnp.stack([o5[:, k, :, k, :] for k in range(N_KV_HEADS)], axis=1).reshape(nb, D_ATTN)

    state_t = jnp.transpose(state_pool[0], (1, 0, 2))
    y_sample = _out_sample(o2, sza_s, u_s, state_t, szp_s, x_sample.reshape(nb, d_model),
                           pool_w_b, pool_scale, w_out_b, fg).reshape(nb, 1, d_model)

    kv_out_s = lambda t: jnp.transpose(t.reshape(2, N_KV_HEADS, HEAD_DIM, nb, 1), (3, 4, 0, 1, 2))[None]
    new_cmp_s = kv_out_s(ckv_ts)
    new_slc_s = kv_out_s(skv_ts)
    new_win_s = jnp.transpose(nwin_t.reshape(nb, 2, N_KV_HEADS, HEAD_DIM, wlen), (0, 4, 1, 2, 3))[None]
    new_pool_s = jnp.transpose(jnp.concatenate([state_t[1:], u_s[None]], axis=0), (1, 0, 2))[None]

    return (y_prompt, y_sample, new_cmp_p, new_slc_p, new_win_p, new_pool_p,
            new_cmp_s, new_slc_s, new_win_s, new_pool_s)
```

```python
import functools

import jax
import jax.numpy as jnp
from jax import lax
from jax.experimental import pallas as pl
from jax.experimental.pallas import tpu as pltpu

F32 = jnp.float32
BF16 = jnp.bfloat16

HEAD_DIM = 64
N_KV_HEADS = 2
GQA = 4
N_HEADS = N_KV_HEADS * GQA
D_ATTN = N_HEADS * HEAD_DIM
KV_W = 2 * N_KV_HEADS * HEAD_DIM
N_BRANCH = 3
D_POOL = 512
CMP_BLOCK = 32
CMP_STRIDE = 16
CMP_HIDDEN = 128
SEL_BLOCK = 64
N_SEL = 16
WINDOW = 512
Q_BLOCK = 256
POOL_WINDOWS = (2, 4, 8, 16)
POOL_GROUP_DIM = 128
POOL_STATE = 15
RMS_EPS = 1e-6
LANES = 128
NEG_BIAS = -(2.0 ** 30)
NEG_FILL = -1e30
GATE_ROWS = 16
BF16_SUBLANES = 16
V_ROWS = HEAD_DIM + BF16_SUBLANES
KEY_TILE = 512
FORCED_VALUE = 1e9
N_FORCED = 3
CHUNK_PITCH = 20
SAMPLE_GROUP = 4
SEL_CHAINS = 2
LOG2E = 1.4426950408889634
VMEM_LIMIT = 48 * 1024 * 1024


def _cparams(sem):
    return pltpu.CompilerParams(dimension_semantics=sem, vmem_limit_bytes=VMEM_LIMIT)


def _silu(z):
    return z * jax.nn.sigmoid(z)


def _proj_kernel(x_ref, g_ref, w_ref, wg_ref, qt_ref, crm_ref, ct_ref, st_ref, wt_ref, ks_ref, vst_ref, kw_ref,
                 vwt_ref, gatet_ref, sza_ref, u_ref, szp_ref, *, tm, seq_len):
    i = pl.program_id(0)
    x = x_ref[...]
    ms = jnp.mean(x * x, axis=-1, keepdims=True)
    xn = (x * lax.rsqrt(ms + RMS_EPS) * g_ref[...]).astype(BF16)
    nt = (((1,), (1,)), ((), ()))

    def mm(c0, c1):
        return lax.dot_general(xn, w_ref[c0:c1, :].astype(BF16), nt, preferred_element_type=F32)

    qt_ref[...] = (mm(0, D_ATTN) * (HEAD_DIM ** -0.5 * LOG2E)).T.astype(BF16)
    kv = mm(D_ATTN, D_ATTN + 3 * KV_W)
    crm_ref[0] = kv[:, 0:LANES]
    crm_ref[1] = kv[:, LANES:2 * LANES]
    kvt = kv.T
    ct_ref[...] = kvt[0:KV_W, :]
    st_ref[...] = kvt[KV_W:2 * KV_W, :]
    wt_ref[...] = kvt[2 * KV_W:3 * KV_W, :]

    lane = lax.broadcasted_iota(jnp.int32, (tm, LANES), 1)
    pos = (i * tm) % seq_len + lax.broadcasted_iota(jnp.int32, (tm, LANES), 0)
    onehot = jnp.where(lane - HEAD_DIM == (pos // SEL_BLOCK) % HEAD_DIM, NEG_BIAS, 0.0).astype(F32)
    ones_rows = jnp.where(lax.broadcasted_iota(jnp.int32, (V_ROWS - HEAD_DIM, tm), 0) == 0, 1.0, 0.0).astype(BF16)
    for kvh in range(N_KV_HEADS):
        for (base, k_dst, v_dst, k_pad) in ((KV_W, ks_ref, vst_ref, onehot), (2 * KV_W, kw_ref, vwt_ref, 0.0)):
            slab = kv[:, base:base + LANES]
            if kvh == 1:
                slab = pltpu.roll(slab, HEAD_DIM, 1)
            k_dst[kvh, :, :] = jnp.where(lane < HEAD_DIM, slab, k_pad).astype(BF16)
            v0 = base + LANES + kvh * HEAD_DIM
            v_dst[kvh, 0:HEAD_DIM, :] = kvt[v0:v0 + HEAD_DIM, :].astype(BF16)
            v_dst[kvh, HEAD_DIM:V_ROWS, :] = ones_rows

    c = D_ATTN + 3 * KV_W + N_HEADS * N_BRANCH
    sza_ref[...] = _silu(mm(c, c + 512)).astype(BF16)
    u_ref[...] = mm(c + 512, c + 1024)
    szp_ref[...] = _silu(mm(c + 1024, c + 1536)).astype(BF16)
    gate_t = jax.nn.sigmoid(lax.dot_general(wg_ref[...].astype(BF16), xn, nt, preferred_element_type=F32))
    for kvh in range(N_KV_HEADS):
        gatet_ref[kvh, :, :] = gate_t[kvh * GATE_ROWS:(kvh + 1) * GATE_ROWS, :]


def _project(x2d, norm_g, w_t, w_gates, *, tm, seq_len):
    rows = x2d.shape[0]
    nt = rows // tm
    tps = seq_len // tm
    row_blk = lambda w: pl.BlockSpec((tm, w), lambda i: (i, 0))
    kv_t = jax.ShapeDtypeStruct((rows // seq_len, KV_W, seq_len), F32)
    kv_t_blk = pl.BlockSpec((None, KV_W, tm), lambda i: (i // tps, 0, i % tps))
    k_rm = jax.ShapeDtypeStruct((N_KV_HEADS, rows, LANES), BF16)
    k_rm_blk = pl.BlockSpec((N_KV_HEADS, tm, LANES), lambda i: (0, i, 0))
    v_t = jax.ShapeDtypeStruct((N_KV_HEADS, V_ROWS, rows), BF16)
    v_t_blk = pl.BlockSpec((N_KV_HEADS, V_ROWS, tm), lambda i: (0, 0, i))
    out_shape = (
        jax.ShapeDtypeStruct((D_ATTN, rows), BF16),
        jax.ShapeDtypeStruct((2, rows, LANES), F32),
        kv_t, kv_t, kv_t,
        k_rm,
        v_t,
        k_rm,
        v_t,
        jax.ShapeDtypeStruct((N_KV_HEADS, GATE_ROWS, rows), F32),
        jax.ShapeDtypeStruct((rows, 512), BF16),
        jax.ShapeDtypeStruct((rows, 512), F32),
        jax.ShapeDtypeStruct((rows, 512), BF16),
    )
    out_specs = (
        pl.BlockSpec((D_ATTN, tm), lambda i: (0, i)),
        pl.BlockSpec((2, tm, LANES), lambda i: (0, i, 0)),
        kv_t_blk, kv_t_blk, kv_t_blk,
        k_rm_blk, v_t_blk, k_rm_blk, v_t_blk,
        pl.BlockSpec((N_KV_HEADS, GATE_ROWS, tm), lambda i: (0, 0, i)),
        row_blk(512), row_blk(512), row_blk(512),
    )
    return pl.pallas_call(
        functools.partial(_proj_kernel, tm=tm, seq_len=seq_len),
        grid=(nt,),
        in_specs=[row_blk(x2d.shape[1]),
                  pl.BlockSpec((1, x2d.shape[1]), lambda i: (0, 0)),
                  pl.BlockSpec(w_t.shape, lambda i: (0, 0), pipeline_mode=pl.Buffered(1)),
                  pl.BlockSpec(w_gates.shape, lambda i: (0, 0))],
        out_specs=out_specs,
        out_shape=out_shape,
        compiler_params=_cparams(("arbitrary",)),
        name="proj",
    )(x2d, norm_g, w_t, w_gates)


def _compress_hidden(load_rows, pe_ref, w1_ref, kv, m):
    xs = [load_rows(s) for s in range(CMP_STRIDE)]
    hs = []
    for sub in range(CMP_BLOCK // CMP_STRIDE):
        lhs = jnp.concatenate([(xs[s] + pe_ref[kv, sub, s]).astype(BF16) for s in range(CMP_STRIDE)], axis=1)
        hs.append(jnp.dot(lhs, w1_ref[kv, sub], preferred_element_type=F32))
    return hs[0] + pltpu.roll(hs[1], m - 1, 0)


def _cover_matrix(nc_pad, ns_pad):
    c0 = lax.broadcasted_iota(jnp.int32, (nc_pad, ns_pad), 0) * CMP_STRIDE
    s0 = lax.broadcasted_iota(jnp.int32, (nc_pad, ns_pad), 1) * SEL_BLOCK
    return jnp.where((c0 < s0 + SEL_BLOCK) & (c0 + CMP_BLOCK > s0), 1.0, 0.0).astype(F32)


def _compress_prompt_kernel(c_ref, pe_ref, w1_ref, w2_ref, kc_ref, rhs_ref, *, nchunk):
    ones_rows = jnp.where(lax.broadcasted_iota(jnp.int32, (HEAD_DIM, nchunk), 0) == 0, 1.0, 0.0).astype(BF16)
    s0 = lax.broadcasted_iota(jnp.int32, (LANES, nchunk), 0) * SEL_BLOCK
    c0 = lax.broadcasted_iota(jnp.int32, (LANES, nchunk), 1) * CMP_STRIDE
    cover_t = jnp.where((c0 < s0 + SEL_BLOCK) & (c0 + CMP_BLOCK > s0), 1.0, 0.0).astype(BF16)
    for kv in range(2):
        load = lambda s: c_ref[kv, pl.ds(s, nchunk, stride=CMP_STRIDE), :]
        h = _compress_hidden(load, pe_ref, w1_ref, kv, nchunk)
        full = jnp.dot(_silu(h).astype(BF16), w2_ref[kv], preferred_element_type=F32)
        if kv == 0:
            kc_ref[0, :, :] = full[:, 0:HEAD_DIM].astype(BF16)
            kc_ref[1, :, :] = full[:, HEAD_DIM:2 * HEAD_DIM].astype(BF16)
        else:
            full_t = full.T
            for kvh in range(N_KV_HEADS):
                rhs_ref[kvh, 0:HEAD_DIM, :] = full_t[kvh * HEAD_DIM:(kvh + 1) * HEAD_DIM, :].astype(BF16)
                rhs_ref[kvh, HEAD_DIM:LANES, :] = ones_rows
                rhs_ref[kvh, LANES:2 * LANES, :] = cover_t


def _compress_prompt(crm, n, pe_t, w1bd, w2bd):
    seq_len = crm.shape[1] // n
    nchunk = seq_len // CMP_STRIDE
    return pl.pallas_call(
        functools.partial(_compress_prompt_kernel, nchunk=nchunk),
        grid=(n,),
        in_specs=[pl.BlockSpec((2, seq_len, LANES), lambda b: (0, b, 0)),
                  pl.BlockSpec(pe_t.shape, lambda b: (0, 0, 0, 0, 0)),
                  pl.BlockSpec(w1bd.shape, lambda b: (0, 0, 0, 0)),
                  pl.BlockSpec(w2bd.shape, lambda b: (0, 0, 0))],
        out_specs=(pl.BlockSpec((None, N_KV_HEADS, nchunk, HEAD_DIM), lambda b: (b, 0, 0, 0)),
                   pl.BlockSpec((None, N_KV_HEADS, 2 * LANES, nchunk), lambda b: (b, 0, 0, 0))),
        out_shape=(jax.ShapeDtypeStruct((n, N_KV_HEADS, nchunk, HEAD_DIM), BF16),
                   jax.ShapeDtypeStruct((n, N_KV_HEADS, 2 * LANES, nchunk), BF16)),
        compiler_params=_cparams(("arbitrary",)),
        name="compress_prompt",
    )(crm, pe_t, w1bd, w2bd)


def _not_selected(val, n_top, axis):
    blk = lax.broadcasted_iota(jnp.int32, val.shape, axis).astype(F32)
    forced = val >= FORCED_VALUE
    notsel = jnp.where(forced, 0.0, 1.0).astype(F32)
    val = jnp.where(forced, -3e38, val)
    for _ in range(max(n_top - N_FORCED, 0)):
        m = jnp.max(val, axis=axis, keepdims=True)
        idx = jnp.min(jnp.where(val == m, blk, float(LANES)), axis=axis, keepdims=True)
        pick = blk == idx
        notsel = jnp.where(pick, 0.0, notsel)
        val = jnp.where(pick, -3e38, val)
    return notsel


def _masked_importance(imp, t, n_blocks, axis):
    blk = lax.broadcasted_iota(jnp.int32, imp.shape, axis)
    cur = t // SEL_BLOCK
    forced = (blk == 0) | (blk == cur) | (blk == cur - 1)
    val = jnp.where(forced, FORCED_VALUE, jnp.where(blk * SEL_BLOCK <= t, imp, -1e9))
    return jnp.where(blk < n_blocks, val, -2e9)


def _attn_prompt_kernel(qt_ref, gatet_ref, sza_ref, kc_ref, rhs_ref, ks_ref, vst_ref, kw_ref, vwt_ref, out_ref,
                        qsel_ref, sbuf_ref, smax_ref, m_ref, acc_ref, mw_ref, accw_ref, oc_ref,
                        *, n_sel_blocks, seq_len):
    qb = pl.program_id(2)
    qs = qb * Q_BLOCK
    hq = GQA * Q_BLOCK
    t_row = qs + lax.broadcasted_iota(jnp.int32, (1, Q_BLOCK), 1)
    t_all = jnp.concatenate([t_row] * GQA, axis=1)
    qt = qt_ref[...]
    qt_all = jnp.concatenate([qt[g * HEAD_DIM:(g + 1) * HEAD_DIM, :] for g in range(GQA)], axis=1)
    zeros_lo = jnp.zeros((HEAD_DIM, hq), BF16)
    qw = jnp.concatenate([qt_all, zeros_lo], axis=0)

    tile = min(KEY_TILE, seq_len)
    n_full = qs // tile
    lo_tiles = HEAD_DIM * SEL_BLOCK // tile
    wwidth = min(WINDOW, seq_len)
    causal = lambda kpos: kpos <= t_all
    band = lambda kpos: (kpos > t_all - WINDOW) & (kpos <= t_all)
    diag_w = lambda kpos: (kpos <= t_all) & (qs >= wwidth)

    def flash_steps(steps):
        staged = []
        for (q, k_ref, vt_ref, start, width, mask_fn, mref, aref, c) in steps:
            start = pl.multiple_of(start, LANES)
            s = jnp.dot(k_ref[pl.ds(start, width), :], q, preferred_element_type=F32)
            if mask_fn is not None:
                s = jnp.where(mask_fn(start + lax.broadcasted_iota(jnp.int32, (width, hq), 0)), s, NEG_FILL)
            staged.append((s, vt_ref[:, pl.ds(start, width)]))
        softmaxed = []
        for (s, vt), step in zip(staged, steps):
            mref, c = step[6], step[8]
            m_old = mref[c]
            m_new = jnp.maximum(m_old, jnp.max(s, axis=0, keepdims=True))
            softmaxed.append((jnp.exp2(s - m_new).astype(BF16), jnp.exp2(m_old - m_new), m_new, vt))
        for (p, alpha, m_new, vt), step in zip(softmaxed, steps):
            mref, aref, c = step[6], step[7], step[8]
            aref[c] = alpha * aref[c] + jnp.dot(vt, p, preferred_element_type=F32)
            mref[c] = m_new

    m_ref[...] = jnp.full(m_ref.shape, NEG_FILL, F32)
    acc_ref[...] = jnp.zeros(acc_ref.shape, F32)
    mw_ref[...] = jnp.full(mw_ref.shape, NEG_FILL, F32)
    accw_ref[...] = jnp.zeros(accw_ref.shape, F32)

    nc = kc_ref.shape[0]
    c_last = lax.broadcasted_iota(jnp.int32, (nc, hq), 0) * CMP_STRIDE + (CMP_BLOCK - 1)
    ok_c = c_last <= t_all
    s = jnp.dot(kc_ref[...], qt_all, preferred_element_type=F32)
    flash_steps([(qw, kw_ref, vwt_ref, qs, Q_BLOCK, diag_w, mw_ref, accw_ref, 1)])
    s = jnp.where(ok_c, s, NEG_FILL)
    e = jnp.exp2(s - jnp.max(s, axis=0, keepdims=True))
    r = jnp.dot(rhs_ref[...], e.astype(BF16), preferred_element_type=F32)
    inv = jnp.where(t_all >= CMP_BLOCK - 1, 1.0 / r[HEAD_DIM:HEAD_DIM + 1, :], 0.0)
    qw_late = jnp.where(inv > -1.0, qw, jnp.zeros_like(qw))
    flash_steps([(qw_late, kw_ref, vwt_ref, jnp.maximum(qs - wwidth, 0), wwidth, band, mw_ref, accw_ref, 0)])
    oc_ref[...] = r[0:HEAD_DIM, :] * inv
    imp_all = r[LANES:2 * LANES, :] * inv
    imp = imp_all[:, 0:Q_BLOCK]
    for g in range(1, GQA):
        imp = imp + imp_all[:, g * Q_BLOCK:(g + 1) * Q_BLOCK]

    notsel = _not_selected(_masked_importance(imp, t_row, n_sel_blocks, 0), min(N_SEL, n_sel_blocks), 0)
    notsel = notsel.astype(BF16)
    q_lo = jnp.concatenate([qt_all, jnp.concatenate([notsel[0:HEAD_DIM, :]] * GQA, axis=1)], axis=0)
    q_hi = jnp.concatenate([qt_all, jnp.concatenate([notsel[HEAD_DIM:2 * HEAD_DIM, :]] * GQA, axis=1)], axis=0)
    qsel_ref[0] = q_lo
    qsel_ref[1] = q_hi

    def scores(b, kt, masked):
        start = pl.multiple_of(kt * tile, LANES)
        q = qsel_ref[jnp.where(kt >= lo_tiles, 1, 0)]
        s = jnp.dot(ks_ref[pl.ds(start, tile), :], q, preferred_element_type=F32)
        if masked:
            s = jnp.where(causal(start + lax.broadcasted_iota(jnp.int32, (tile, hq), 0)), s, NEG_FILL)
        sbuf_ref[b] = s
        smax_ref[b] = jnp.max(s, axis=0, keepdims=True)

    def consume(b, kt):
        start = pl.multiple_of(kt * tile, LANES)
        m_old = m_ref[b]
        m_new = jnp.maximum(m_old, smax_ref[b])
        p = jnp.exp2(sbuf_ref[b] - m_new).astype(BF16)
        acc_ref[b] = jnp.exp2(m_old - m_new) * acc_ref[b] + jnp.dot(vst_ref[:, pl.ds(start, tile)], p,
                                                                  preferred_element_type=F32)
        m_ref[b] = m_new

    @pl.when(n_full == 0)
    def _():
        scores(0, 0, True)

    @pl.when(n_full > 0)
    def _():
        scores(0, 0, False)

    def pair_step(j):
        scores(1, 2 * j + 1, False)
        consume(0, 2 * j)
        scores(0, 2 * j + 2, False)
        consume(1, 2 * j + 1)

    def pair_body(j, carry):
        pair_step(j)
        return carry

    def quad_body(j, carry):
        pair_step(2 * j)
        pair_step(2 * j + 1)
        return carry

    n_pairs = jnp.maximum(n_full - 1, 0) // 2
    lax.fori_loop(0, n_pairs // 2, quad_body, 0)
    lax.fori_loop(n_pairs // 2 * 2, n_pairs, pair_body, 0)

    @pl.when((n_full > 0) & (n_full % 2 == 0))
    def _():
        scores(1, n_full - 1, False)
        consume(0, n_full - 2)
        scores(0, n_full, True)
        consume(1, n_full - 1)
        consume(0, n_full)

    @pl.when(n_full % 2 == 1)
    def _():
        scores(1, n_full, True)
        consume(0, n_full - 1)
        consume(1, n_full)

    @pl.when(n_full == 0)
    def _():
        consume(0, 0)

    def merged(mref, aref, n_chain):
        m = mref[0]
        for c in range(1, n_chain):
            m = jnp.maximum(m, mref[c])
        a = jnp.exp2(mref[0] - m) * aref[0]
        for c in range(1, n_chain):
            a = a + jnp.exp2(mref[c] - m) * aref[c]
        return a[0:HEAD_DIM, :] * (1.0 / a[HEAD_DIM:HEAD_DIM + 1, :])

    gates = gatet_ref[...]
    sza = sza_ref[...]
    o_s = merged(m_ref, acc_ref, SEL_CHAINS)
    o_w = merged(mw_ref, accw_ref, 2)
    o_c = oc_ref[...]
    outs = []
    for g in range(GQA):
        cols = slice(g * Q_BLOCK, (g + 1) * Q_BLOCK)
        gc = gates[g * N_BRANCH + 0:g * N_BRANCH + 1, :]
        gs = gates[g * N_BRANCH + 1:g * N_BRANCH + 2, :]
        gw = gates[g * N_BRANCH + 2:g * N_BRANCH + 3, :]
        outs.append((gc * o_c[:, cols] + gs * o_s[:, cols] + gw * o_w[:, cols]).T)
    out_ref[...] = (jnp.concatenate(outs, axis=1) * sza).astype(BF16)


def _attn_prompt(qt, gates_t, sza3, kc, rhs_c, ks, vst, kw, vwt):
    n, seq_len, _ = sza3.shape
    nqb = seq_len // Q_BLOCK
    nchunk = kc.shape[2]
    hw = GQA * HEAD_DIM
    k_blk = pl.BlockSpec((None, seq_len, LANES), lambda b, k, i: (k, b, 0))
    vt_blk = pl.BlockSpec((None, V_ROWS, seq_len), lambda b, k, i: (k, 0, b))
    return pl.pallas_call(
        functools.partial(_attn_prompt_kernel, n_sel_blocks=-(-seq_len // SEL_BLOCK), seq_len=seq_len),
        grid=(n, N_KV_HEADS, nqb),
        in_specs=[
            pl.BlockSpec((hw, Q_BLOCK), lambda b, k, i: (k, b * nqb + i)),
            pl.BlockSpec((None, GATE_ROWS, Q_BLOCK), lambda b, k, i: (k, 0, b * nqb + i)),
            pl.BlockSpec((None, Q_BLOCK, hw), lambda b, k, i: (b, i, k)),
            pl.BlockSpec((None, None, nchunk, HEAD_DIM), lambda b, k, i: (b, k, 0, 0)),
            pl.BlockSpec((None, None, 2 * LANES, nchunk), lambda b, k, i: (b, k, 0, 0)),
            k_blk, vt_blk, k_blk, vt_blk,
        ],
        out_specs=pl.BlockSpec((None, Q_BLOCK, hw), lambda b, k, i: (b, i, k)),
        out_shape=jax.ShapeDtypeStruct((n, seq_len, D_ATTN), BF16),
        scratch_shapes=[
            pltpu.VMEM((2, 2 * HEAD_DIM, GQA * Q_BLOCK), BF16),
            pltpu.VMEM((SEL_CHAINS, min(KEY_TILE, seq_len), GQA * Q_BLOCK), F32),
            pltpu.VMEM((SEL_CHAINS, 1, GQA * Q_BLOCK), F32),
            pltpu.VMEM((SEL_CHAINS, 1, GQA * Q_BLOCK), F32),
            pltpu.VMEM((SEL_CHAINS, V_ROWS, GQA * Q_BLOCK), F32),
            pltpu.VMEM((2, 1, GQA * Q_BLOCK), F32),
            pltpu.VMEM((2, V_ROWS, GQA * Q_BLOCK), F32),
            pltpu.VMEM((HEAD_DIM, GQA * Q_BLOCK), F32),
        ],
        compiler_params=_cparams(("arbitrary", "arbitrary", "arbitrary")),
        name="attn_prompt",
    )(qt, gates_t, sza3, kc, rhs_c, ks, vst, kw, vwt)


def _pool_out(d, pw_ref, ps_ref, szp):
    ys = [jnp.dot(d[:, g * POOL_GROUP_DIM:(g + 1) * POOL_GROUP_DIM].astype(BF16), pw_ref[g],
                  preferred_element_type=F32) for g in range(len(POOL_WINDOWS))]
    return jnp.concatenate(ys, axis=1) * ps_ref[...] * szp


def _finish(x, a_bf16, b, wo_ref, fg_ref):
    mix = jnp.concatenate([a_bf16, b.astype(BF16)], axis=1)
    y = x + jnp.dot(mix, wo_ref[...], preferred_element_type=F32)
    ms = jnp.mean(y * y, axis=-1, keepdims=True)
    return y * lax.rsqrt(ms + RMS_EPS) * fg_ref[...]


def _out_prompt_kernel(a_ref, u_ref, halo_ref, szp_ref, x_ref, pw_ref, ps_ref, wo_ref, fg_ref, y_ref, ext_ref,
                       *, tm, halo):
    i = pl.program_id(1)
    u = u_ref[...]
    ext_ref[0:halo, :] = jnp.where(i > 0, halo_ref[...], 0.0)
    ext_ref[halo:halo + tm, :] = u
    pos = i * tm + lax.broadcasted_iota(jnp.int32, (tm, POOL_GROUP_DIM), 0)
    ds = []
    for g, w in enumerate(POOL_WINDOWS):
        c0 = g * POOL_GROUP_DIM
        acc = u[:, c0:c0 + POOL_GROUP_DIM]
        for k in range(1, w):
            acc = acc + ext_ref[halo - k:halo - k + tm, c0:c0 + POOL_GROUP_DIM]
        cnt = jnp.minimum(pos + 1, w).astype(F32)
        ds.append(acc / cnt - u[:, c0:c0 + POOL_GROUP_DIM])
    b = _pool_out(jnp.concatenate(ds, axis=1), pw_ref, ps_ref, szp_ref[...])
    y_ref[...] = _finish(x_ref[...], a_ref[...], b, wo_ref, fg_ref)


def _out_prompt(a3, u3, szp3, x3, pool_w, pool_scale, w_out, final_g, *, tm):
    n, seq_len, d_model = x3.shape
    halo = 16
    nt = seq_len // tm
    blk = lambda w: pl.BlockSpec((None, tm, w), lambda b, i: (b, i, 0))
    const = lambda a: pl.BlockSpec(a.shape, lambda b, i: (0,) * a.ndim)
    return pl.pallas_call(
        functools.partial(_out_prompt_kernel, tm=tm, halo=halo),
        grid=(n, nt),
        in_specs=[blk(D_ATTN), blk(D_POOL),
                  pl.BlockSpec((None, halo, D_POOL), lambda b, i: (b, jnp.maximum(i * (tm // halo) - 1, 0), 0)),
                  blk(D_POOL), blk(d_model),
                  const(pool_w), const(pool_scale), const(w_out), const(final_g)],
        out_specs=blk(d_model),
        out_shape=jax.ShapeDtypeStruct((n, seq_len, d_model), F32),
        scratch_shapes=[pltpu.VMEM((tm + halo, D_POOL), F32)],
        compiler_params=_cparams(("arbitrary", "arbitrary")),
        name="out_prompt",
    )(a3, u3, u3, szp3, x3, pool_w, pool_scale, w_out, final_g)


def _out_sample_kernel(o_ref, sza_ref, u_ref, st_ref, szp_ref, x_ref, pw_ref, ps_ref, wo_ref, fg_ref, y_ref):
    u = u_ref[...]
    ds = []
    for g, w in enumerate(POOL_WINDOWS):
        c0 = g * POOL_GROUP_DIM
        acc = u[:, c0:c0 + POOL_GROUP_DIM]
        for k in range(1, w):
            acc = acc + st_ref[POOL_STATE - k, :, c0:c0 + POOL_GROUP_DIM]
        ds.append(acc / float(w) - u[:, c0:c0 + POOL_GROUP_DIM])
    b = _pool_out(jnp.concatenate(ds, axis=1), pw_ref, ps_ref, szp_ref[...])
    a = (o_ref[...] * sza_ref[...]).astype(BF16)
    y_ref[...] = _finish(x_ref[...], a, b, wo_ref, fg_ref)


def _out_sample(o2, sza, u, state_t, szp, x2, pool_w, pool_scale, w_out, final_g):
    args = (o2, sza, u, state_t, szp, x2, pool_w, pool_scale, w_out, final_g)
    full = lambda a: pl.BlockSpec(a.shape, lambda i: (0,) * a.ndim)
    return pl.pallas_call(
        _out_sample_kernel,
        grid=(1,),
        in_specs=[full(a) for a in args],
        out_specs=full(x2),
        out_shape=jax.ShapeDtypeStruct(x2.shape, F32),
        compiler_params=_cparams(("arbitrary",)),
        name="out_sample",
    )(*args)


def _page_copy(pt_ref, cache_ref, buf_ref, sem_ref, step, slot, j, n_pages, n_seq):
    g, p = divmod(j, n_pages)
    return pltpu.make_async_copy(cache_ref.at[pt_ref[step * n_seq + g, p]], buf_ref.at[slot, j], sem_ref.at[slot])


def _fetch_pages(pt_ref, cache_ref, buf_ref, sem_ref, n_pages, n_seq):
    i = pl.program_id(0)
    slot = i % 2
    n_copy = n_seq * n_pages

    @pl.when(i == 0)
    def _():
        for j in range(n_copy):
            _page_copy(pt_ref, cache_ref, buf_ref, sem_ref, 0, 0, j, n_pages, n_seq).start()

    @pl.when(i + 1 < pl.num_programs(0))
    def _():
        for j in range(n_copy):
            _page_copy(pt_ref, cache_ref, buf_ref, sem_ref, i + 1, 1 - slot, j, n_pages, n_seq).start()

    for j in range(n_copy):
        _page_copy(pt_ref, cache_ref, buf_ref, sem_ref, i, slot, j, n_pages, n_seq).wait()
    return slot


def _cmp_sample_kernel(pt_ref, cache_ref, qbd_ref, pe_ref, w1_ref, w2_ref, oc_ref, imp_ref, rows_ref, buf_ref,
                       sem_ref, *, n_pages, page, n_seq):
    slot = _fetch_pages(pt_ref, cache_ref, buf_ref, sem_ref, n_pages, n_seq)
    pages = [buf_ref.at[slot, j] for j in range(n_seq * n_pages)]
    nchunk = n_pages * page // CMP_STRIDE
    fulls = []
    m = n_seq * nchunk
    cpp = page // CMP_STRIDE
    for kv in range(2):
        for i, page_ref in enumerate(pages):
            half = page_ref[kv * LANES:(kv + 1) * LANES, :].astype(BF16)
            rows = half.T.astype(F32)
            for c in range(cpp):
                dst = (i * cpp + c) * CHUNK_PITCH
                rows_ref[kv, dst:dst + CMP_STRIDE, :] = rows[c * CMP_STRIDE:(c + 1) * CMP_STRIDE, :]
        load = lambda s: rows_ref[kv, pl.ds(s, m, stride=CHUNK_PITCH), :]
        h = _compress_hidden(load, pe_ref, w1_ref, kv, m)
        fulls.append(jnp.dot(_silu(h).astype(BF16), w2_ref[kv], preferred_element_type=F32).astype(BF16))
    k_c, v_c = fulls
    nrow = n_seq * N_HEADS
    qbd = qbd_ref[...].reshape(nrow, LANES).astype(BF16)
    s = lax.dot_general(qbd, k_c, (((1,), (1,)), ((), ())), preferred_element_type=F32)
    col = lax.broadcasted_iota(jnp.int32, s.shape, 1)
    own = col // nchunk == lax.broadcasted_iota(jnp.int32, s.shape, 0) // N_HEADS
    ok = own & (col % nchunk < nchunk - 1)
    s = jnp.where(ok, s, NEG_FILL)
    e = jnp.where(ok, jnp.exp2(s - jnp.max(s, axis=1, keepdims=True)), 0.0)
    pc = (e * (1.0 / jnp.sum(e, axis=1, keepdims=True))).astype(BF16)
    cover = jnp.concatenate([_cover_matrix(nchunk, LANES).astype(BF16)] * n_seq, axis=0)
    oc_ref[...] = jnp.dot(pc, v_c, preferred_element_type=F32).reshape(n_seq, N_HEADS, LANES)
    imp_ref[...] = jnp.dot(pc, cover, preferred_element_type=F32).reshape(n_seq, N_HEADS, LANES)


def _cmp_sample(page_table, cache_t, qbd, pe_t, w1bd, w2bd):
    nb, n_pages = page_table.shape
    page = cache_t.shape[-1]
    n_seq = SAMPLE_GROUP
    const = lambda a: pl.BlockSpec(a.shape, lambda b, pt: (0,) * a.ndim)
    per_b = pl.BlockSpec((n_seq, N_HEADS, LANES), lambda b, pt: (b, 0, 0))
    grid_spec = pltpu.PrefetchScalarGridSpec(
        num_scalar_prefetch=1,
        grid=(nb // n_seq,),
        in_specs=[pl.BlockSpec(memory_space=pl.ANY), per_b, const(pe_t), const(w1bd), const(w2bd)],
        out_specs=(per_b, per_b),
        scratch_shapes=[pltpu.VMEM((2, n_seq * n_pages * page // CMP_STRIDE * CHUNK_PITCH, LANES), F32),
                        pltpu.VMEM((2, n_seq * n_pages, KV_W, page), F32),
                        pltpu.SemaphoreType.DMA((2,))],
    )
    return pl.pallas_call(
        functools.partial(_cmp_sample_kernel, n_pages=n_pages, page=page, n_seq=n_seq),
        grid_spec=grid_spec,
        out_shape=(jax.ShapeDtypeStruct((nb, N_HEADS, LANES), F32),
                   jax.ShapeDtypeStruct((nb, N_HEADS, LANES), F32)),
        compiler_params=_cparams(("arbitrary",)),
        name="cmp_sample",
    )(page_table, cache_t, qbd, pe_t, w1bd, w2bd)


def _topk_sample_kernel(imp_ref, out_ref, *, t_pos, n_blocks):
    imp = imp_ref[0] + imp_ref[1] + imp_ref[2] + imp_ref[3]
    t = jnp.full((imp.shape[0], 1), t_pos, jnp.int32)
    out_ref[...] = _not_selected(_masked_importance(imp, t, n_blocks, 1), min(N_SEL, n_blocks), 1)


def _topk_sample(imp_g, *, t_pos, n_blocks):
    rows = imp_g.shape[1]
    return pl.pallas_call(
        functools.partial(_topk_sample_kernel, t_pos=t_pos, n_blocks=n_blocks),
        grid=(1,),
        in_specs=[pl.BlockSpec(imp_g.shape, lambda i: (0, 0, 0))],
        out_specs=pl.BlockSpec((rows, LANES), lambda i: (0, 0)),
        out_shape=jax.ShapeDtypeStruct((rows, LANES), F32),
        compiler_params=_cparams(("arbitrary",)),
        name="topk_sample",
    )(imp_g)


def _attn_sample_kernel(pt_ref, *refs, n_pages, page, n_seq):
    (cache_ref, win_ref, qbd_ref, ns_ref, snew_ref, wnew_ref, oc_ref, gate_ref, o_ref, nwin_ref,
     buf_ref, sem_ref) = refs
    slot = _fetch_pages(pt_ref, cache_ref, buf_ref, sem_ref, n_pages, n_seq)
    pages = [buf_ref.at[slot, j] for j in range(n_seq * n_pages)]
    nrow = n_seq * N_HEADS
    qf = qbd_ref[...].reshape(nrow, LANES)
    qbd = qf.astype(BF16)
    ns = ns_ref[...].reshape(nrow, LANES)
    row_seq = lax.broadcasted_iota(jnp.int32, (nrow, 1), 0) // N_HEADS
    nt_ = (((1,), (1,)), ((), ()))

    def new_token(rows_ref):
        rows = jnp.concatenate([jnp.broadcast_to(rows_ref[g], (N_HEADS, KV_W)) for g in range(n_seq)], axis=0)
        k_new = rows[:, 0:LANES].astype(BF16).astype(F32)
        v_new = rows[:, LANES:2 * LANES].astype(BF16).astype(F32)
        return jnp.sum(qf * k_new, axis=1, keepdims=True), v_new

    lane = lax.broadcasted_iota(jnp.int32, (nrow, page), 1)
    per_page = page // SEL_BLOCK
    scores = []
    for i, page_ref in enumerate(pages):
        g, p = divmod(i, n_pages)
        s = jnp.dot(qbd, page_ref[0:LANES, :].astype(BF16), preferred_element_type=F32)
        flag = ns[:, p * per_page:p * per_page + 1]
        for j in range(1, per_page):
            flag = jnp.where(lane >= j * SEL_BLOCK, ns[:, p * per_page + j:p * per_page + j + 1], flag)
        scores.append(jnp.where((row_seq == g) & (flag < 0.5), s, NEG_FILL))
    nb_cache = n_pages * per_page
    s_new, v_new = new_token(snew_ref)
    s_new = jnp.where(ns[:, nb_cache:nb_cache + 1] > 0.5, NEG_FILL, s_new)
    s_max = scores[0]
    for s in scores[1:]:
        s_max = jnp.maximum(s_max, s)
    m = jnp.maximum(s_new, jnp.max(s_max, axis=1, keepdims=True))
    e_new = jnp.exp2(s_new - m)
    e_sum = jnp.zeros((nrow, page), F32)
    acc = e_new * v_new
    for s, page_ref in zip(scores, pages):
        e = jnp.exp2(s - m).astype(BF16)
        e_sum = e_sum + e.astype(F32)
        acc = acc + lax.dot_general(e, page_ref[LANES:2 * LANES, :].astype(BF16), nt_, preferred_element_type=F32)
    o_s = acc * (1.0 / (e_new + jnp.sum(e_sum, axis=1, keepdims=True)))

    wlen = win_ref.shape[2]
    first = max(wlen - WINDOW + 1, 0)
    in_win = lax.broadcasted_iota(jnp.int32, (nrow, wlen), 1) >= first
    scores_w = []
    for g in range(n_seq):
        s_w = jnp.dot(qbd, win_ref[g, 0:LANES, :].astype(BF16), preferred_element_type=F32)
        scores_w.append(jnp.where((row_seq == g) & in_win, s_w, NEG_FILL))
    sw_new, vw_new = new_token(wnew_ref)
    sw_max = scores_w[0]
    for s_w in scores_w[1:]:
        sw_max = jnp.maximum(sw_max, s_w)
    m_w = jnp.maximum(sw_new, jnp.max(sw_max, axis=1, keepdims=True))
    ew_new = jnp.exp2(sw_new - m_w)
    ew_sum = jnp.zeros((nrow, wlen), F32)
    acc_w = ew_new * vw_new
    for g in range(n_seq):
        e_w = jnp.exp2(scores_w[g] - m_w).astype(BF16)
        ew_sum = ew_sum + e_w.astype(F32)
        acc_w = acc_w + lax.dot_general(e_w, win_ref[g, LANES:2 * LANES, :].astype(BF16), nt_,
                                        preferred_element_type=F32)
    o_w = acc_w * (1.0 / (ew_new + jnp.sum(ew_sum, axis=1, keepdims=True)))

    gates = gate_ref[...].reshape(nrow, LANES)
    o = gates[:, 0:1] * oc_ref[...].reshape(nrow, LANES) + gates[:, 1:2] * o_s + gates[:, 2:3] * o_w
    o_ref[...] = o.reshape(n_seq, N_HEADS, LANES)

    last = lax.broadcasted_iota(jnp.int32, (KV_W, LANES), 1) == LANES - 1
    for g in range(n_seq):
        shifted = pltpu.roll(win_ref[g], wlen - 1, 1)
        new_col = jnp.broadcast_to(wnew_ref[g], (LANES, KV_W)).T
        nwin_ref[g, :, 0:wlen - LANES] = shifted[:, 0:wlen - LANES]
        nwin_ref[g, :, wlen - LANES:wlen] = jnp.where(last, new_col, shifted[:, wlen - LANES:wlen])


def _attn_sample(page_table, cache_t, win_t, qbd, ns8, s_new, w_new, o_c, gates8):
    nb, n_pages = page_table.shape
    page = cache_t.shape[-1]
    n_seq = SAMPLE_GROUP
    per_b = lambda a: pl.BlockSpec((n_seq,) + a.shape[1:], lambda b, pt: (b,) + (0,) * (a.ndim - 1))
    grid_spec = pltpu.PrefetchScalarGridSpec(
        num_scalar_prefetch=1,
        grid=(nb // n_seq,),
        in_specs=[pl.BlockSpec(memory_space=pl.ANY)]
        + [per_b(a) for a in (win_t, qbd, ns8, s_new, w_new, o_c, gates8)],
        out_specs=(per_b(o_c), per_b(win_t)),
        scratch_shapes=[pltpu.VMEM((2, n_seq * n_pages, KV_W, page), F32),
                        pltpu.SemaphoreType.DMA((2,))],
    )
    return pl.pallas_call(
        functools.partial(_attn_sample_kernel, n_pages=n_pages, page=page, n_seq=n_seq),
        grid_spec=grid_spec,
        out_shape=(jax.ShapeDtypeStruct(o_c.shape, F32), jax.ShapeDtypeStruct(win_t.shape, F32)),
        compiler_params=_cparams(("arbitrary",)),
        name="attn_sample",
    )(page_table, cache_t, win_t, qbd, ns8, s_new, w_new, o_c, gates8)


def _prep_w_in(w_in):
    w_t = w_in.T
    o = D_ATTN + 3 * KV_W
    ng = GQA * N_BRANCH
    gl = w_t[o:o + N_KV_HEADS * ng].reshape(N_KV_HEADS, ng, w_t.shape[1])
    w_gates = jnp.pad(gl, ((0, 0), (0, GATE_ROWS - ng), (0, 0))).reshape(N_KV_HEADS * GATE_ROWS, w_t.shape[1])
    return w_t, w_gates


def _prep_compress(cmp_pe, cmp_w1, cmp_w2):
    r = CMP_BLOCK // CMP_STRIDE
    pe_t = jnp.tile(cmp_pe.reshape(2, r, CMP_STRIDE, 1, HEAD_DIM), (1, 1, 1, 1, N_KV_HEADS))
    w1 = cmp_w1.reshape(2, r, CMP_STRIDE, HEAD_DIM, CMP_HIDDEN)
    z1 = jnp.zeros_like(w1)
    top = jnp.concatenate([w1, z1], axis=-1)
    bot = jnp.concatenate([z1, w1], axis=-1)
    w1bd = jnp.concatenate([top, bot], axis=3)
    w1bd = w1bd.reshape(2, r, CMP_STRIDE * LANES, N_KV_HEADS * CMP_HIDDEN).astype(BF16)
    z2 = jnp.zeros_like(cmp_w2)
    w2bd = jnp.concatenate([jnp.concatenate([cmp_w2, z2], axis=-1),
                            jnp.concatenate([z2, cmp_w2], axis=-1)], axis=1).astype(BF16)
    return pe_t, w1bd, w2bd


def _kv_out(kv_t):
    n, _, rows = kv_t.shape
    return jnp.transpose(kv_t.reshape(n, 2, N_KV_HEADS, HEAD_DIM, rows), (0, 4, 1, 2, 3))[None]


def kernel(x_prompt, x_sample, cache_cmp_kv, cache_slc_kv, cache_win_kv, state_pool, page_table, norm_g, w_in,
           cmp_pe, cmp_w1, cmp_w2, pool_w, pool_scale, w_out, final_g):
    n, seq_len, d_model = x_prompt.shape
    nb = x_sample.shape[0]
    n_phys, page = cache_cmp_kv.shape[1], cache_cmp_kv.shape[2]
    n_pages = page_table.shape[1]
    past_len = n_pages * page
    wlen = cache_win_kv.shape[2]

    w_t, w_gates = _prep_w_in(w_in[0])
    pe_t, w1bd, w2bd = _prep_compress(cmp_pe[0], cmp_w1[0], cmp_w2[0])
    pool_w_b = pool_w[0].astype(BF16)
    w_out_b = w_out[0].astype(BF16)
    fg = final_g.reshape(1, d_model)

    (qt, crm, ckv_t, skv_t, wkv_t, ks, vst, kw, vwt, gates_t, sza, u, szp) = _project(
        x_prompt.reshape(n * seq_len, d_model), norm_g, w_t, w_gates, tm=512, seq_len=seq_len)
    r3 = lambda a: a.reshape(n, seq_len, a.shape[-1])
    kc, rhs_c = _compress_prompt(crm, n, pe_t, w1bd, w2bd)
    a3 = _attn_prompt(qt, gates_t, r3(sza), kc, rhs_c, ks, vst, kw, vwt)
    y_prompt = _out_prompt(a3, r3(u), r3(szp), x_prompt, pool_w_b, pool_scale, w_out_b, fg, tm=512)

    new_cmp_p = _kv_out(ckv_t)
    new_slc_p = _kv_out(skv_t)
    new_win_p = _kv_out(wkv_t[:, :, seq_len - min(WINDOW, seq_len):])
    new_pool_p = r3(u)[:, seq_len - POOL_STATE:][None]

    (qt_s, _, ckv_ts, skv_ts, wkv_ts, _, _, _, _, gates_ts, sza_s, u_s, szp_s) = _project(
        x_sample.reshape(nb, d_model), norm_g, w_t, w_gates, tm=nb, seq_len=nb)
    q_s = qt_s.T
    q5 = q_s.reshape(nb, N_KV_HEADS, GQA, 1, HEAD_DIM)
    eye = jnp.eye(N_KV_HEADS, dtype=q_s.dtype).reshape(1, N_KV_HEADS, 1, N_KV_HEADS, 1)
    qbd = (q5 * eye).reshape(nb, N_HEADS, LANES).astype(F32)

    to_pages = lambda c: jnp.transpose(c[0], (0, 2, 3, 4, 1)).reshape(n_phys, KV_W, page)
    o_c, imp8 = _cmp_sample(page_table, to_pages(cache_cmp_kv), qbd, pe_t, w1bd, w2bd)
    imp_g = jnp.transpose(imp8.reshape(nb * N_KV_HEADS, GQA, LANES), (1, 0, 2))
    n_blocks = -(-(past_len + 1) // SEL_BLOCK)
    notsel = _topk_sample(imp_g, t_pos=past_len, n_blocks=n_blocks)
    ns8 = jnp.repeat(notsel.reshape(nb, N_KV_HEADS, 1, LANES), GQA, axis=2).reshape(nb, N_HEADS, LANES)

    gates8 = jnp.transpose(gates_ts[:, :GQA * N_BRANCH, :], (2, 0, 1)).reshape(nb, N_HEADS, N_BRANCH)
    gates8 = jnp.pad(gates8, ((0, 0), (0, 0), (0, LANES - N_BRANCH)))
    win_t = jnp.transpose(cache_win_kv[0], (0, 2, 3, 4, 1)).reshape(nb, KV_W, wlen)
    s_new = skv_ts[0].T.reshape(nb, 1, KV_W)
    w_new = wkv_ts[0].T.reshape(nb, 1, KV_W)
    o8, nwin_t = _attn_sample(page_table, to_pages(cache_slc_kv), win_t, qbd, ns8, s_new, w_new, o_c, gates8)
    o5 = o8.reshape(nb, N_KV_HEADS, GQA, N_KV_HEADS, HEAD_DIM)
    o2 = jnp.stack([o5[:, k, :, k, :] for k in range(N_KV_HEADS)], axis=1).reshape(nb, D_ATTN)

    state_t = jnp.transpose(state_pool[0], (1, 0, 2))
    y_sample = _out_sample(o2, sza_s, u_s, state_t, szp_s, x_sample.reshape(nb, d_model),
                           pool_w_b, pool_scale, w_out_b, fg).reshape(nb, 1, d_model)

    kv_out_s = lambda t: jnp.transpose(t.reshape(2, N_KV_HEADS, HEAD_DIM, nb, 1), (3, 4, 0, 1, 2))[None]
    new_cmp_s = kv_out_s(ckv_ts)
    new_slc_s = kv_out_s(skv_ts)
    new_win_s = jnp.transpose(nwin_t.reshape(nb, 2, N_KV_HEADS, HEAD_DIM, wlen), (0, 4, 1, 2, 3))[None]
    new_pool_s = jnp.transpose(jnp.concatenate([state_t[1:], u_s[None]], axis=0), (1, 0, 2))[None]

    return (y_prompt, y_sample, new_cmp_p, new_slc_p, new_win_p, new_pool_p,
            new_cmp_s, new_slc_s, new_win_s, new_pool_s)
```

```python
import functools

import jax
import jax.numpy as jnp
from jax import lax
from jax.experimental import pallas as pl
from jax.experimental.pallas import tpu as pltpu

F32 = jnp.float32
BF16 = jnp.bfloat16

HEAD_DIM = 64
N_KV_HEADS = 2
GQA = 4
N_HEADS = N_KV_HEADS * GQA
D_ATTN = N_HEADS * HEAD_DIM
KV_W = 2 * N_KV_HEADS * HEAD_DIM
N_BRANCH = 3
D_POOL = 512
CMP_BLOCK = 32
CMP_STRIDE = 16
CMP_HIDDEN = 128
SEL_BLOCK = 64
N_SEL = 16
WINDOW = 512
Q_BLOCK = 256
POOL_WINDOWS = (2, 4, 8, 16)
assert all(w & (w - 1) == 0 for w in POOL_WINDOWS)
POOL_GROUP_DIM = 128
POOL_STATE = 15
RMS_EPS = 1e-6
LANES = 128
NEG_BIAS = -(2.0 ** 30)
NEG_FILL = -1e30
GATE_ROWS = 16
BF16_SUBLANES = 16
V_ROWS = HEAD_DIM + BF16_SUBLANES
KEY_TILE = 512
FORCED_VALUE = 1e9
N_FORCED = 3
CHUNK_PITCH = 20
SAMPLE_GROUP = 4
SEL_CHAINS = 2
LOG2E = 1.4426950408889634
VMEM_LIMIT = 48 * 1024 * 1024


def _cparams(sem):
    return pltpu.CompilerParams(dimension_semantics=sem, vmem_limit_bytes=VMEM_LIMIT)


def _silu(z):
    return z * jax.nn.sigmoid(z)


def _proj_kernel(x_ref, g_ref, w_ref, wg_ref, qt_ref, crm_ref, ct_ref, st_ref, wt_ref, ks_ref, vst_ref, kw_ref,
                 vwt_ref, gatet_ref, sza_ref, u_ref, szp_ref, *, tm, seq_len):
    i = pl.program_id(0)
    x = x_ref[...]
    ms = jnp.mean(x * x, axis=-1, keepdims=True)
    xn = (x * lax.rsqrt(ms + RMS_EPS) * g_ref[...]).astype(BF16)
    nt = (((1,), (1,)), ((), ()))

    def mm(c0, c1):
        return lax.dot_general(xn, w_ref[c0:c1, :].astype(BF16), nt, preferred_element_type=F32)

    qt_ref[...] = (mm(0, D_ATTN) * (HEAD_DIM ** -0.5 * LOG2E)).T.astype(BF16)
    kv = mm(D_ATTN, D_ATTN + 3 * KV_W)
    crm_ref[0] = kv[:, 0:LANES]
    crm_ref[1] = kv[:, LANES:2 * LANES]
    kvt = kv.T
    ct_ref[...] = kvt[0:KV_W, :]
    st_ref[...] = kvt[KV_W:2 * KV_W, :]
    wt_ref[...] = kvt[2 * KV_W:3 * KV_W, :]

    lane = lax.broadcasted_iota(jnp.int32, (tm, LANES), 1)
    pos = (i * tm) % seq_len + lax.broadcasted_iota(jnp.int32, (tm, LANES), 0)
    onehot = jnp.where(lane - HEAD_DIM == (pos // SEL_BLOCK) % HEAD_DIM, NEG_BIAS, 0.0).astype(F32)
    ones_rows = jnp.where(lax.broadcasted_iota(jnp.int32, (V_ROWS - HEAD_DIM, tm), 0) == 0, 1.0, 0.0).astype(BF16)
    for kvh in range(N_KV_HEADS):
        for (base, k_dst, v_dst, k_pad) in ((KV_W, ks_ref, vst_ref, onehot), (2 * KV_W, kw_ref, vwt_ref, 0.0)):
            slab = kv[:, base:base + LANES]
            if kvh == 1:
                slab = pltpu.roll(slab, HEAD_DIM, 1)
            k_dst[kvh, :, :] = jnp.where(lane < HEAD_DIM, slab, k_pad).astype(BF16)
            v0 = base + LANES + kvh * HEAD_DIM
            v_dst[kvh, 0:HEAD_DIM, :] = kvt[v0:v0 + HEAD_DIM, :].astype(BF16)
            v_dst[kvh, HEAD_DIM:V_ROWS, :] = ones_rows

    c = D_ATTN + 3 * KV_W + N_HEADS * N_BRANCH
    sza_ref[...] = _silu(mm(c, c + 512)).astype(BF16)
    u_ref[...] = mm(c + 512, c + 1024)
    szp_ref[...] = _silu(mm(c + 1024, c + 1536)).astype(BF16)
    gate_t = jax.nn.sigmoid(lax.dot_general(wg_ref[...].astype(BF16), xn, nt, preferred_element_type=F32))
    for kvh in range(N_KV_HEADS):
        gatet_ref[kvh, :, :] = gate_t[kvh * GATE_ROWS:(kvh + 1) * GATE_ROWS, :]


def _project(x2d, norm_g, w_t, w_gates, *, tm, seq_len):
    rows = x2d.shape[0]
    nt = rows // tm
    tps = seq_len // tm
    row_blk = lambda w: pl.BlockSpec((tm, w), lambda i: (i, 0))
    kv_t = jax.ShapeDtypeStruct((rows // seq_len, KV_W, seq_len), F32)
    kv_t_blk = pl.BlockSpec((None, KV_W, tm), lambda i: (i // tps, 0, i % tps))
    k_rm = jax.ShapeDtypeStruct((N_KV_HEADS, rows, LANES), BF16)
    k_rm_blk = pl.BlockSpec((N_KV_HEADS, tm, LANES), lambda i: (0, i, 0))
    v_t = jax.ShapeDtypeStruct((N_KV_HEADS, V_ROWS, rows), BF16)
    v_t_blk = pl.BlockSpec((N_KV_HEADS, V_ROWS, tm), lambda i: (0, 0, i))
    out_shape = (
        jax.ShapeDtypeStruct((D_ATTN, rows), BF16),
        jax.ShapeDtypeStruct((2, rows, LANES), F32),
        kv_t, kv_t, kv_t,
        k_rm,
        v_t,
        k_rm,
        v_t,
        jax.ShapeDtypeStruct((N_KV_HEADS, GATE_ROWS, rows), F32),
        jax.ShapeDtypeStruct((rows, 512), BF16),
        jax.ShapeDtypeStruct((rows, 512), F32),
        jax.ShapeDtypeStruct((rows, 512), BF16),
    )
    out_specs = (
        pl.BlockSpec((D_ATTN, tm), lambda i: (0, i)),
        pl.BlockSpec((2, tm, LANES), lambda i: (0, i, 0)),
        kv_t_blk, kv_t_blk, kv_t_blk,
        k_rm_blk, v_t_blk, k_rm_blk, v_t_blk,
        pl.BlockSpec((N_KV_HEADS, GATE_ROWS, tm), lambda i: (0, 0, i)),
        row_blk(512), row_blk(512), row_blk(512),
    )
    return pl.pallas_call(
        functools.partial(_proj_kernel, tm=tm, seq_len=seq_len),
        grid=(nt,),
        in_specs=[row_blk(x2d.shape[1]),
                  pl.BlockSpec((1, x2d.shape[1]), lambda i: (0, 0)),
                  pl.BlockSpec(w_t.shape, lambda i: (0, 0), pipeline_mode=pl.Buffered(1)),
                  pl.BlockSpec(w_gates.shape, lambda i: (0, 0))],
        out_specs=out_specs,
        out_shape=out_shape,
        compiler_params=_cparams(("arbitrary",)),
        name="proj",
    )(x2d, norm_g, w_t, w_gates)


def _compress_hidden(load_rows, pe_ref, w1_ref, kv, m):
    xs = [load_rows(s) for s in range(CMP_STRIDE)]
    hs = []
    for sub in range(CMP_BLOCK // CMP_STRIDE):
        lhs = jnp.concatenate([(xs[s] + pe_ref[kv, sub, s]).astype(BF16) for s in range(CMP_STRIDE)], axis=1)
        hs.append(jnp.dot(lhs, w1_ref[kv, sub], preferred_element_type=F32))
    return hs[0] + pltpu.roll(hs[1], m - 1, 0)


def _cover_matrix(nc_pad, ns_pad):
    c0 = lax.broadcasted_iota(jnp.int32, (nc_pad, ns_pad), 0) * CMP_STRIDE
    s0 = lax.broadcasted_iota(jnp.int32, (nc_pad, ns_pad), 1) * SEL_BLOCK
    return jnp.where((c0 < s0 + SEL_BLOCK) & (c0 + CMP_BLOCK > s0), 1.0, 0.0).astype(F32)


def _compress_prompt_kernel(c_ref, pe_ref, w1_ref, w2_ref, kc_ref, rhs_ref, *, nchunk):
    ones_rows = jnp.where(lax.broadcasted_iota(jnp.int32, (HEAD_DIM, nchunk), 0) == 0, 1.0, 0.0).astype(BF16)
    s0 = lax.broadcasted_iota(jnp.int32, (LANES, nchunk), 0) * SEL_BLOCK
    c0 = lax.broadcasted_iota(jnp.int32, (LANES, nchunk), 1) * CMP_STRIDE
    cover_t = jnp.where((c0 < s0 + SEL_BLOCK) & (c0 + CMP_BLOCK > s0), 1.0, 0.0).astype(BF16)
    for kv in range(2):
        load = lambda s: c_ref[kv, pl.ds(s, nchunk, stride=CMP_STRIDE), :]
        h = _compress_hidden(load, pe_ref, w1_ref, kv, nchunk)
        full = jnp.dot(_silu(h).astype(BF16), w2_ref[kv], preferred_element_type=F32)
        if kv == 0:
            kc_ref[0, :, :] = full[:, 0:HEAD_DIM].astype(BF16)
            kc_ref[1, :, :] = full[:, HEAD_DIM:2 * HEAD_DIM].astype(BF16)
        else:
            full_t = full.T
            for kvh in range(N_KV_HEADS):
                rhs_ref[kvh, 0:HEAD_DIM, :] = full_t[kvh * HEAD_DIM:(kvh + 1) * HEAD_DIM, :].astype(BF16)
                rhs_ref[kvh, HEAD_DIM:LANES, :] = ones_rows
                rhs_ref[kvh, LANES:2 * LANES, :] = cover_t


def _compress_prompt(crm, n, pe_t, w1bd, w2bd):
    seq_len = crm.shape[1] // n
    nchunk = seq_len // CMP_STRIDE
    return pl.pallas_call(
        functools.partial(_compress_prompt_kernel, nchunk=nchunk),
        grid=(n,),
        in_specs=[pl.BlockSpec((2, seq_len, LANES), lambda b: (0, b, 0)),
                  pl.BlockSpec(pe_t.shape, lambda b: (0, 0, 0, 0, 0)),
                  pl.BlockSpec(w1bd.shape, lambda b: (0, 0, 0, 0)),
                  pl.BlockSpec(w2bd.shape, lambda b: (0, 0, 0))],
        out_specs=(pl.BlockSpec((None, N_KV_HEADS, nchunk, HEAD_DIM), lambda b: (b, 0, 0, 0)),
                   pl.BlockSpec((None, N_KV_HEADS, 2 * LANES, nchunk), lambda b: (b, 0, 0, 0))),
        out_shape=(jax.ShapeDtypeStruct((n, N_KV_HEADS, nchunk, HEAD_DIM), BF16),
                   jax.ShapeDtypeStruct((n, N_KV_HEADS, 2 * LANES, nchunk), BF16)),
        compiler_params=_cparams(("arbitrary",)),
        name="compress_prompt",
    )(crm, pe_t, w1bd, w2bd)


def _not_selected(val, n_top, axis):
    blk = lax.broadcasted_iota(jnp.int32, val.shape, axis).astype(F32)
    forced = val >= FORCED_VALUE
    notsel = jnp.where(forced, 0.0, 1.0).astype(F32)
    val = jnp.where(forced, -3e38, val)
    for _ in range(max(n_top - N_FORCED, 0)):
        m = jnp.max(val, axis=axis, keepdims=True)
        idx = jnp.min(jnp.where(val == m, blk, float(LANES)), axis=axis, keepdims=True)
        pick = blk == idx
        notsel = jnp.where(pick, 0.0, notsel)
        val = jnp.where(pick, -3e38, val)
    return notsel


def _masked_importance(imp, t, n_blocks, axis):
    blk = lax.broadcasted_iota(jnp.int32, imp.shape, axis)
    cur = t // SEL_BLOCK
    forced = (blk == 0) | (blk == cur) | (blk == cur - 1)
    val = jnp.where(forced, FORCED_VALUE, jnp.where(blk * SEL_BLOCK <= t, imp, -1e9))
    return jnp.where(blk < n_blocks, val, -2e9)


def _attn_prompt_kernel(qt_ref, gatet_ref, sza_ref, kc_ref, rhs_ref, ks_ref, vst_ref, kw_ref, vwt_ref, out_ref,
                        qsel_ref, sbuf_ref, smax_ref, m_ref, acc_ref, mw_ref, accw_ref, oc_ref,
                        *, n_sel_blocks, seq_len):
    qb = pl.program_id(2)
    qs = qb * Q_BLOCK
    hq = GQA * Q_BLOCK
    t_row = qs + lax.broadcasted_iota(jnp.int32, (1, Q_BLOCK), 1)
    t_all = jnp.concatenate([t_row] * GQA, axis=1)
    qt = qt_ref[...]
    qt_all = jnp.concatenate([qt[g * HEAD_DIM:(g + 1) * HEAD_DIM, :] for g in range(GQA)], axis=1)
    zeros_lo = jnp.zeros((HEAD_DIM, hq), BF16)
    qw = jnp.concatenate([qt_all, zeros_lo], axis=0)

    tile = min(KEY_TILE, seq_len)
    n_full = qs // tile
    lo_tiles = HEAD_DIM * SEL_BLOCK // tile
    wwidth = min(WINDOW, seq_len)
    causal = lambda kpos: kpos <= t_all
    band = lambda kpos: (kpos > t_all - WINDOW) & (kpos <= t_all)
    diag_w = lambda kpos: (kpos <= t_all) & (qs >= wwidth)

    def flash_steps(steps):
        staged = []
        for (q, k_ref, vt_ref, start, width, mask_fn, mref, aref, c) in steps:
            start = pl.multiple_of(start, LANES)
            s = jnp.dot(k_ref[pl.ds(start, width), :], q, preferred_element_type=F32)
            if mask_fn is not None:
                s = jnp.where(mask_fn(start + lax.broadcasted_iota(jnp.int32, (width, hq), 0)), s, NEG_FILL)
            staged.append((s, vt_ref[:, pl.ds(start, width)]))
        softmaxed = []
        for (s, vt), step in zip(staged, steps):
            mref, c = step[6], step[8]
            m_old = mref[c]
            m_new = jnp.maximum(m_old, jnp.max(s, axis=0, keepdims=True))
            softmaxed.append((jnp.exp2(s - m_new).astype(BF16), jnp.exp2(m_old - m_new), m_new, vt))
        for (p, alpha, m_new, vt), step in zip(softmaxed, steps):
            mref, aref, c = step[6], step[7], step[8]
            aref[c] = alpha * aref[c] + jnp.dot(vt, p, preferred_element_type=F32)
            mref[c] = m_new

    m_ref[...] = jnp.full(m_ref.shape, NEG_FILL, F32)
    acc_ref[...] = jnp.zeros(acc_ref.shape, F32)
    mw_ref[...] = jnp.full(mw_ref.shape, NEG_FILL, F32)
    accw_ref[...] = jnp.zeros(accw_ref.shape, F32)

    nc = kc_ref.shape[0]
    c_last = lax.broadcasted_iota(jnp.int32, (nc, hq), 0) * CMP_STRIDE + (CMP_BLOCK - 1)
    ok_c = c_last <= t_all
    s = jnp.dot(kc_ref[...], qt_all, preferred_element_type=F32)
    flash_steps([(qw, kw_ref, vwt_ref, qs, Q_BLOCK, diag_w, mw_ref, accw_ref, 1)])
    s = jnp.where(ok_c, s, NEG_FILL)
    e = jnp.exp2(s - jnp.max(s, axis=0, keepdims=True))
    r = jnp.dot(rhs_ref[...], e.astype(BF16), preferred_element_type=F32)
    inv = jnp.where(t_all >= CMP_BLOCK - 1, 1.0 / r[HEAD_DIM:HEAD_DIM + 1, :], 0.0)
    qw_late = jnp.where(inv > -1.0, qw, jnp.zeros_like(qw))
    flash_steps([(qw_late, kw_ref, vwt_ref, jnp.maximum(qs - wwidth, 0), wwidth, band, mw_ref, accw_ref, 0)])
    oc_ref[...] = r[0:HEAD_DIM, :] * inv
    imp_all = r[LANES:2 * LANES, :] * inv
    imp = imp_all[:, 0:Q_BLOCK]
    for g in range(1, GQA):
        imp = imp + imp_all[:, g * Q_BLOCK:(g + 1) * Q_BLOCK]

    notsel = _not_selected(_masked_importance(imp, t_row, n_sel_blocks, 0), min(N_SEL, n_sel_blocks), 0)
    notsel = notsel.astype(BF16)
    q_lo = jnp.concatenate([qt_all, jnp.concatenate([notsel[0:HEAD_DIM, :]] * GQA, axis=1)], axis=0)
    q_hi = jnp.concatenate([qt_all, jnp.concatenate([notsel[HEAD_DIM:2 * HEAD_DIM, :]] * GQA, axis=1)], axis=0)
    qsel_ref[0] = q_lo
    qsel_ref[1] = q_hi

    def scores(b, kt, masked):
        start = pl.multiple_of(kt * tile, LANES)
        q = qsel_ref[jnp.where(kt >= lo_tiles, 1, 0)]
        s = jnp.dot(ks_ref[pl.ds(start, tile), :], q, preferred_element_type=F32)
        if masked:
            s = jnp.where(causal(start + lax.broadcasted_iota(jnp.int32, (tile, hq), 0)), s, NEG_FILL)
        sbuf_ref[b] = s
        smax_ref[b] = jnp.max(s, axis=0, keepdims=True)

    def consume(b, kt):
        start = pl.multiple_of(kt * tile, LANES)
        m_old = m_ref[b]
        m_new = jnp.maximum(m_old, smax_ref[b])
        p = jnp.exp2(sbuf_ref[b] - m_new).astype(BF16)
        acc_ref[b] = jnp.exp2(m_old - m_new) * acc_ref[b] + jnp.dot(vst_ref[:, pl.ds(start, tile)], p,
                                                                  preferred_element_type=F32)
        m_ref[b] = m_new

    @pl.when(n_full == 0)
    def _():
        scores(0, 0, True)

    @pl.when(n_full > 0)
    def _():
        scores(0, 0, False)

    def pair_step(j):
        scores(1, 2 * j + 1, False)
        consume(0, 2 * j)
        scores(0, 2 * j + 2, False)
        consume(1, 2 * j + 1)

    def pair_body(j, carry):
        pair_step(j)
        return carry

    def quad_body(j, carry):
        pair_step(2 * j)
        pair_step(2 * j + 1)
        return carry

    n_pairs = jnp.maximum(n_full - 1, 0) // 2
    lax.fori_loop(0, n_pairs // 2, quad_body, 0)
    lax.fori_loop(n_pairs // 2 * 2, n_pairs, pair_body, 0)

    @pl.when((n_full > 0) & (n_full % 2 == 0))
    def _():
        scores(1, n_full - 1, False)
        consume(0, n_full - 2)
        scores(0, n_full, True)
        consume(1, n_full - 1)
        consume(0, n_full)

    @pl.when(n_full % 2 == 1)
    def _():
        scores(1, n_full, True)
        consume(0, n_full - 1)
        consume(1, n_full)

    @pl.when(n_full == 0)
    def _():
        consume(0, 0)

    def merged(mref, aref, n_chain):
        m = mref[0]
        for c in range(1, n_chain):
            m = jnp.maximum(m, mref[c])
        a = jnp.exp2(mref[0] - m) * aref[0]
        for c in range(1, n_chain):
            a = a + jnp.exp2(mref[c] - m) * aref[c]
        return a[0:HEAD_DIM, :] * (1.0 / a[HEAD_DIM:HEAD_DIM + 1, :])

    gates = gatet_ref[...]
    sza = sza_ref[...]
    o_s = merged(m_ref, acc_ref, SEL_CHAINS)
    o_w = merged(mw_ref, accw_ref, 2)
    o_c = oc_ref[...]
    outs = []
    for g in range(GQA):
        cols = slice(g * Q_BLOCK, (g + 1) * Q_BLOCK)
        gc = gates[g * N_BRANCH + 0:g * N_BRANCH + 1, :]
        gs = gates[g * N_BRANCH + 1:g * N_BRANCH + 2, :]
        gw = gates[g * N_BRANCH + 2:g * N_BRANCH + 3, :]
        outs.append((gc * o_c[:, cols] + gs * o_s[:, cols] + gw * o_w[:, cols]).T)
    out_ref[...] = (jnp.concatenate(outs, axis=1) * sza).astype(BF16)


def _attn_prompt(qt, gates_t, sza3, kc, rhs_c, ks, vst, kw, vwt):
    n, seq_len, _ = sza3.shape
    nqb = seq_len // Q_BLOCK
    nchunk = kc.shape[2]
    hw = GQA * HEAD_DIM
    k_blk = pl.BlockSpec((None, seq_len, LANES), lambda b, k, i: (k, b, 0))
    vt_blk = pl.BlockSpec((None, V_ROWS, seq_len), lambda b, k, i: (k, 0, b))
    return pl.pallas_call(
        functools.partial(_attn_prompt_kernel, n_sel_blocks=-(-seq_len // SEL_BLOCK), seq_len=seq_len),
        grid=(n, N_KV_HEADS, nqb),
        in_specs=[
            pl.BlockSpec((hw, Q_BLOCK), lambda b, k, i: (k, b * nqb + i)),
            pl.BlockSpec((None, GATE_ROWS, Q_BLOCK), lambda b, k, i: (k, 0, b * nqb + i)),
            pl.BlockSpec((None, Q_BLOCK, hw), lambda b, k, i: (b, i, k)),
            pl.BlockSpec((None, None, nchunk, HEAD_DIM), lambda b, k, i: (b, k, 0, 0)),
            pl.BlockSpec((None, None, 2 * LANES, nchunk), lambda b, k, i: (b, k, 0, 0)),
            k_blk, vt_blk, k_blk, vt_blk,
        ],
        out_specs=pl.BlockSpec((None, Q_BLOCK, hw), lambda b, k, i: (b, i, k)),
        out_shape=jax.ShapeDtypeStruct((n, seq_len, D_ATTN), BF16),
        scratch_shapes=[
            pltpu.VMEM((2, 2 * HEAD_DIM, GQA * Q_BLOCK), BF16),
            pltpu.VMEM((SEL_CHAINS, min(KEY_TILE, seq_len), GQA * Q_BLOCK), F32),
            pltpu.VMEM((SEL_CHAINS, 1, GQA * Q_BLOCK), F32),
            pltpu.VMEM((SEL_CHAINS, 1, GQA * Q_BLOCK), F32),
            pltpu.VMEM((SEL_CHAINS, V_ROWS, GQA * Q_BLOCK), F32),
            pltpu.VMEM((2, 1, GQA * Q_BLOCK), F32),
            pltpu.VMEM((2, V_ROWS, GQA * Q_BLOCK), F32),
            pltpu.VMEM((HEAD_DIM, GQA * Q_BLOCK), F32),
        ],
        compiler_params=_cparams(("arbitrary", "arbitrary", "arbitrary")),
        name="attn_prompt",
    )(qt, gates_t, sza3, kc, rhs_c, ks, vst, kw, vwt)


def _pool_out(d, pw_ref, ps_ref, szp):
    ys = [jnp.dot(d[:, g * POOL_GROUP_DIM:(g + 1) * POOL_GROUP_DIM].astype(BF16), pw_ref[g],
                  preferred_element_type=F32) for g in range(len(POOL_WINDOWS))]
    return jnp.concatenate(ys, axis=1) * ps_ref[...] * szp


def _finish(x, a_bf16, b, wo_ref, fg_ref):
    mix = jnp.concatenate([a_bf16, b.astype(BF16)], axis=1)
    y = x + jnp.dot(mix, wo_ref[...], preferred_element_type=F32)
    ms = jnp.mean(y * y, axis=-1, keepdims=True)
    return y * lax.rsqrt(ms + RMS_EPS) * fg_ref[...]


def _out_prompt_kernel(a_ref, u_ref, halo_ref, szp_ref, x_ref, pw_ref, ps_ref, wo_ref, fg_ref, y_ref, ext_ref,
                       *, tm, halo):
    i = pl.program_id(1)
    u = u_ref[...]
    ext_ref[0:halo, :] = jnp.where(i > 0, halo_ref[...], 0.0)
    ext_ref[halo:halo + tm, :] = u
    pos = i * tm + lax.broadcasted_iota(jnp.int32, (tm, POOL_GROUP_DIM), 0)
    ds = []
    for g, w in enumerate(POOL_WINDOWS):
        c0 = g * POOL_GROUP_DIM
        run = ext_ref[:, c0:c0 + POOL_GROUP_DIM]
        k = 1
        while k < w:
            run = run + pltpu.roll(run, k, 0)
            k *= 2
        cnt = jnp.minimum(pos + 1, w).astype(F32)
        ds.append(run[halo:halo + tm, :] / cnt - u[:, c0:c0 + POOL_GROUP_DIM])
    b = _pool_out(jnp.concatenate(ds, axis=1), pw_ref, ps_ref, szp_ref[...])
    y_ref[...] = _finish(x_ref[...], a_ref[...], b, wo_ref, fg_ref)


def _out_prompt(a3, u3, szp3, x3, pool_w, pool_scale, w_out, final_g, *, tm):
    n, seq_len, d_model = x3.shape
    halo = 16
    nt = seq_len // tm
    blk = lambda w: pl.BlockSpec((None, tm, w), lambda b, i: (b, i, 0))
    const = lambda a: pl.BlockSpec(a.shape, lambda b, i: (0,) * a.ndim)
    return pl.pallas_call(
        functools.partial(_out_prompt_kernel, tm=tm, halo=halo),
        grid=(n, nt),
        in_specs=[blk(D_ATTN), blk(D_POOL),
                  pl.BlockSpec((None, halo, D_POOL), lambda b, i: (b, jnp.maximum(i * (tm // halo) - 1, 0), 0)),
                  blk(D_POOL), blk(d_model),
                  const(pool_w), const(pool_scale), const(w_out), const(final_g)],
        out_specs=blk(d_model),
        out_shape=jax.ShapeDtypeStruct((n, seq_len, d_model), F32),
        scratch_shapes=[pltpu.VMEM((tm + halo, D_POOL), F32)],
        compiler_params=_cparams(("arbitrary", "arbitrary")),
        name="out_prompt",
    )(a3, u3, u3, szp3, x3, pool_w, pool_scale, w_out, final_g)


def _out_sample_kernel(o_ref, sza_ref, u_ref, st_ref, szp_ref, x_ref, pw_ref, ps_ref, wo_ref, fg_ref, y_ref):
    u = u_ref[...]
    ds = []
    for g, w in enumerate(POOL_WINDOWS):
        c0 = g * POOL_GROUP_DIM
        acc = u[:, c0:c0 + POOL_GROUP_DIM]
        for k in range(1, w):
            acc = acc + st_ref[POOL_STATE - k, :, c0:c0 + POOL_GROUP_DIM]
        ds.append(acc / float(w) - u[:, c0:c0 + POOL_GROUP_DIM])
    b = _pool_out(jnp.concatenate(ds, axis=1), pw_ref, ps_ref, szp_ref[...])
    a = (o_ref[...] * sza_ref[...]).astype(BF16)
    y_ref[...] = _finish(x_ref[...], a, b, wo_ref, fg_ref)


def _out_sample(o2, sza, u, state_t, szp, x2, pool_w, pool_scale, w_out, final_g):
    args = (o2, sza, u, state_t, szp, x2, pool_w, pool_scale, w_out, final_g)
    full = lambda a: pl.BlockSpec(a.shape, lambda i: (0,) * a.ndim)
    return pl.pallas_call(
        _out_sample_kernel,
        grid=(1,),
        in_specs=[full(a) for a in args],
        out_specs=full(x2),
        out_shape=jax.ShapeDtypeStruct(x2.shape, F32),
        compiler_params=_cparams(("arbitrary",)),
        name="out_sample",
    )(*args)


def _page_copy(pt_ref, cache_ref, buf_ref, sem_ref, step, slot, j, n_pages, n_seq):
    g, p = divmod(j, n_pages)
    return pltpu.make_async_copy(cache_ref.at[pt_ref[step * n_seq + g, p]], buf_ref.at[slot, j], sem_ref.at[slot])


def _fetch_pages(pt_ref, cache_ref, buf_ref, sem_ref, n_pages, n_seq):
    i = pl.program_id(0)
    slot = i % 2
    n_copy = n_seq * n_pages

    @pl.when(i == 0)
    def _():
        for j in range(n_copy):
            _page_copy(pt_ref, cache_ref, buf_ref, sem_ref, 0, 0, j, n_pages, n_seq).start()

    @pl.when(i + 1 < pl.num_programs(0))
    def _():
        for j in range(n_copy):
            _page_copy(pt_ref, cache_ref, buf_ref, sem_ref, i + 1, 1 - slot, j, n_pages, n_seq).start()

    for j in range(n_copy):
        _page_copy(pt_ref, cache_ref, buf_ref, sem_ref, i, slot, j, n_pages, n_seq).wait()
    return slot


def _cmp_sample_kernel(pt_ref, cache_ref, qbd_ref, pe_ref, w1_ref, w2_ref, oc_ref, imp_ref, rows_ref, buf_ref,
                       sem_ref, *, n_pages, page, n_seq):
    slot = _fetch_pages(pt_ref, cache_ref, buf_ref, sem_ref, n_pages, n_seq)
    pages = [buf_ref.at[slot, j] for j in range(n_seq * n_pages)]
    nchunk = n_pages * page // CMP_STRIDE
    fulls = []
    m = n_seq * nchunk
    cpp = page // CMP_STRIDE
    for kv in range(2):
        for i, page_ref in enumerate(pages):
            half = page_ref[kv * LANES:(kv + 1) * LANES, :].astype(BF16)
            rows = half.T.astype(F32)
            for c in range(cpp):
                dst = (i * cpp + c) * CHUNK_PITCH
                rows_ref[kv, dst:dst + CMP_STRIDE, :] = rows[c * CMP_STRIDE:(c + 1) * CMP_STRIDE, :]
        load = lambda s: rows_ref[kv, pl.ds(s, m, stride=CHUNK_PITCH), :]
        h = _compress_hidden(load, pe_ref, w1_ref, kv, m)
        fulls.append(jnp.dot(_silu(h).astype(BF16), w2_ref[kv], preferred_element_type=F32).astype(BF16))
    k_c, v_c = fulls
    nrow = n_seq * N_HEADS
    qbd = qbd_ref[...].reshape(nrow, LANES).astype(BF16)
    s = lax.dot_general(qbd, k_c, (((1,), (1,)), ((), ())), preferred_element_type=F32)
    col = lax.broadcasted_iota(jnp.int32, s.shape, 1)
    own = col // nchunk == lax.broadcasted_iota(jnp.int32, s.shape, 0) // N_HEADS
    ok = own & (col % nchunk < nchunk - 1)
    s = jnp.where(ok, s, NEG_FILL)
    e = jnp.where(ok, jnp.exp2(s - jnp.max(s, axis=1, keepdims=True)), 0.0)
    pc = (e * (1.0 / jnp.sum(e, axis=1, keepdims=True))).astype(BF16)
    cover = jnp.concatenate([_cover_matrix(nchunk, LANES).astype(BF16)] * n_seq, axis=0)
    oc_ref[...] = jnp.dot(pc, v_c, preferred_element_type=F32).reshape(n_seq, N_HEADS, LANES)
    imp_ref[...] = jnp.dot(pc, cover, preferred_element_type=F32).reshape(n_seq, N_HEADS, LANES)


def _cmp_sample(page_table, cache_t, qbd, pe_t, w1bd, w2bd):
    nb, n_pages = page_table.shape
    page = cache_t.shape[-1]
    n_seq = SAMPLE_GROUP
    const = lambda a: pl.BlockSpec(a.shape, lambda b, pt: (0,) * a.ndim)
    per_b = pl.BlockSpec((n_seq, N_HEADS, LANES), lambda b, pt: (b, 0, 0))
    grid_spec = pltpu.PrefetchScalarGridSpec(
        num_scalar_prefetch=1,
        grid=(nb // n_seq,),
        in_specs=[pl.BlockSpec(memory_space=pl.ANY), per_b, const(pe_t), const(w1bd), const(w2bd)],
        out_specs=(per_b, per_b),
        scratch_shapes=[pltpu.VMEM((2, n_seq * n_pages * page // CMP_STRIDE * CHUNK_PITCH, LANES), F32),
                        pltpu.VMEM((2, n_seq * n_pages, KV_W, page), F32),
                        pltpu.SemaphoreType.DMA((2,))],
    )
    return pl.pallas_call(
        functools.partial(_cmp_sample_kernel, n_pages=n_pages, page=page, n_seq=n_seq),
        grid_spec=grid_spec,
        out_shape=(jax.ShapeDtypeStruct((nb, N_HEADS, LANES), F32),
                   jax.ShapeDtypeStruct((nb, N_HEADS, LANES), F32)),
        compiler_params=_cparams(("arbitrary",)),
        name="cmp_sample",
    )(page_table, cache_t, qbd, pe_t, w1bd, w2bd)


def _topk_sample_kernel(imp_ref, out_ref, *, t_pos, n_blocks):
    imp = imp_ref[0] + imp_ref[1] + imp_ref[2] + imp_ref[3]
    t = jnp.full((imp.shape[0], 1), t_pos, jnp.int32)
    out_ref[...] = _not_selected(_masked_importance(imp, t, n_blocks, 1), min(N_SEL, n_blocks), 1)


def _topk_sample(imp_g, *, t_pos, n_blocks):
    rows = imp_g.shape[1]
    return pl.pallas_call(
        functools.partial(_topk_sample_kernel, t_pos=t_pos, n_blocks=n_blocks),
        grid=(1,),
        in_specs=[pl.BlockSpec(imp_g.shape, lambda i: (0, 0, 0))],
        out_specs=pl.BlockSpec((rows, LANES), lambda i: (0, 0)),
        out_shape=jax.ShapeDtypeStruct((rows, LANES), F32),
        compiler_params=_cparams(("arbitrary",)),
        name="topk_sample",
    )(imp_g)


def _attn_sample_kernel(pt_ref, *refs, n_pages, page, n_seq):
    (cache_ref, win_ref, qbd_ref, ns_ref, snew_ref, wnew_ref, oc_ref, gate_ref, o_ref, nwin_ref,
     buf_ref, sem_ref) = refs
    slot = _fetch_pages(pt_ref, cache_ref, buf_ref, sem_ref, n_pages, n_seq)
    pages = [buf_ref.at[slot, j] for j in range(n_seq * n_pages)]
    nrow = n_seq * N_HEADS
    qf = qbd_ref[...].reshape(nrow, LANES)
    qbd = qf.astype(BF16)
    ns = ns_ref[...].reshape(nrow, LANES)
    row_seq = lax.broadcasted_iota(jnp.int32, (nrow, 1), 0) // N_HEADS
    nt_ = (((1,), (1,)), ((), ()))

    def new_token(rows_ref):
        rows = jnp.concatenate([jnp.broadcast_to(rows_ref[g], (N_HEADS, KV_W)) for g in range(n_seq)], axis=0)
        k_new = rows[:, 0:LANES].astype(BF16).astype(F32)
        v_new = rows[:, LANES:2 * LANES].astype(BF16).astype(F32)
        return jnp.sum(qf * k_new, axis=1, keepdims=True), v_new

    lane = lax.broadcasted_iota(jnp.int32, (nrow, page), 1)
    per_page = page // SEL_BLOCK
    scores = []
    for i, page_ref in enumerate(pages):
        g, p = divmod(i, n_pages)
        s = jnp.dot(qbd, page_ref[0:LANES, :].astype(BF16), preferred_element_type=F32)
        flag = ns[:, p * per_page:p * per_page + 1]
        for j in range(1, per_page):
            flag = jnp.where(lane >= j * SEL_BLOCK, ns[:, p * per_page + j:p * per_page + j + 1], flag)
        scores.append(jnp.where((row_seq == g) & (flag < 0.5), s, NEG_FILL))
    nb_cache = n_pages * per_page
    s_new, v_new = new_token(snew_ref)
    s_new = jnp.where(ns[:, nb_cache:nb_cache + 1] > 0.5, NEG_FILL, s_new)
    s_max = scores[0]
    for s in scores[1:]:
        s_max = jnp.maximum(s_max, s)
    m = jnp.maximum(s_new, jnp.max(s_max, axis=1, keepdims=True))
    e_new = jnp.exp2(s_new - m)
    e_sum = jnp.zeros((nrow, page), F32)
    acc = e_new * v_new
    for s, page_ref in zip(scores, pages):
        e = jnp.exp2(s - m).astype(BF16)
        e_sum = e_sum + e.astype(F32)
        acc = acc + lax.dot_general(e, page_ref[LANES:2 * LANES, :].astype(BF16), nt_, preferred_element_type=F32)
    o_s = acc * (1.0 / (e_new + jnp.sum(e_sum, axis=1, keepdims=True)))

    wlen = win_ref.shape[2]
    first = max(wlen - WINDOW + 1, 0)
    in_win = lax.broadcasted_iota(jnp.int32, (nrow, wlen), 1) >= first
    scores_w = []
    for g in range(n_seq):
        s_w = jnp.dot(qbd, win_ref[g, 0:LANES, :].astype(BF16), preferred_element_type=F32)
        scores_w.append(jnp.where((row_seq == g) & in_win, s_w, NEG_FILL))
    sw_new, vw_new = new_token(wnew_ref)
    sw_max = scores_w[0]
    for s_w in scores_w[1:]:
        sw_max = jnp.maximum(sw_max, s_w)
    m_w = jnp.maximum(sw_new, jnp.max(sw_max, axis=1, keepdims=True))
    ew_new = jnp.exp2(sw_new - m_w)
    ew_sum = jnp.zeros((nrow, wlen), F32)
    acc_w = ew_new * vw_new
    for g in range(n_seq):
        e_w = jnp.exp2(scores_w[g] - m_w).astype(BF16)
        ew_sum = ew_sum + e_w.astype(F32)
        acc_w = acc_w + lax.dot_general(e_w, win_ref[g, LANES:2 * LANES, :].astype(BF16), nt_,
                                        preferred_element_type=F32)
    o_w = acc_w * (1.0 / (ew_new + jnp.sum(ew_sum, axis=1, keepdims=True)))

    gates = gate_ref[...].reshape(nrow, LANES)
    o = gates[:, 0:1] * oc_ref[...].reshape(nrow, LANES) + gates[:, 1:2] * o_s + gates[:, 2:3] * o_w
    o_ref[...] = o.reshape(n_seq, N_HEADS, LANES)

    last = lax.broadcasted_iota(jnp.int32, (KV_W, LANES), 1) == LANES - 1
    for g in range(n_seq):
        shifted = pltpu.roll(win_ref[g], wlen - 1, 1)
        new_col = jnp.broadcast_to(wnew_ref[g], (LANES, KV_W)).T
        nwin_ref[g, :, 0:wlen - LANES] = shifted[:, 0:wlen - LANES]
        nwin_ref[g, :, wlen - LANES:wlen] = jnp.where(last, new_col, shifted[:, wlen - LANES:wlen])


def _attn_sample(page_table, cache_t, win_t, qbd, ns8, s_new, w_new, o_c, gates8):
    nb, n_pages = page_table.shape
    page = cache_t.shape[-1]
    n_seq = SAMPLE_GROUP
    per_b = lambda a: pl.BlockSpec((n_seq,) + a.shape[1:], lambda b, pt: (b,) + (0,) * (a.ndim - 1))
    grid_spec = pltpu.PrefetchScalarGridSpec(
        num_scalar_prefetch=1,
        grid=(nb // n_seq,),
        in_specs=[pl.BlockSpec(memory_space=pl.ANY)]
        + [per_b(a) for a in (win_t, qbd, ns8, s_new, w_new, o_c, gates8)],
        out_specs=(per_b(o_c), per_b(win_t)),
        scratch_shapes=[pltpu.VMEM((2, n_seq * n_pages, KV_W, page), F32),
                        pltpu.SemaphoreType.DMA((2,))],
    )
    return pl.pallas_call(
        functools.partial(_attn_sample_kernel, n_pages=n_pages, page=page, n_seq=n_seq),
        grid_spec=grid_spec,
        out_shape=(jax.ShapeDtypeStruct(o_c.shape, F32), jax.ShapeDtypeStruct(win_t.shape, F32)),
        compiler_params=_cparams(("arbitrary",)),
        name="attn_sample",
    )(page_table, cache_t, win_t, qbd, ns8, s_new, w_new, o_c, gates8)


def _prep_w_in(w_in):
    w_t = w_in.T
    o = D_ATTN + 3 * KV_W
    ng = GQA * N_BRANCH
    gl = w_t[o:o + N_KV_HEADS * ng].reshape(N_KV_HEADS, ng, w_t.shape[1])
    w_gates = jnp.pad(gl, ((0, 0), (0, GATE_ROWS - ng), (0, 0))).reshape(N_KV_HEADS * GATE_ROWS, w_t.shape[1])
    return w_t, w_gates


def _prep_compress(cmp_pe, cmp_w1, cmp_w2):
    r = CMP_BLOCK // CMP_STRIDE
    pe_t = jnp.tile(cmp_pe.reshape(2, r, CMP_STRIDE, 1, HEAD_DIM), (1, 1, 1, 1, N_KV_HEADS))
    w1 = cmp_w1.reshape(2, r, CMP_STRIDE, HEAD_DIM, CMP_HIDDEN)
    z1 = jnp.zeros_like(w1)
    top = jnp.concatenate([w1, z1], axis=-1)
    bot = jnp.concatenate([z1, w1], axis=-1)
    w1bd = jnp.concatenate([top, bot], axis=3)
    w1bd = w1bd.reshape(2, r, CMP_STRIDE * LANES, N_KV_HEADS * CMP_HIDDEN).astype(BF16)
    z2 = jnp.zeros_like(cmp_w2)
    w2bd = jnp.concatenate([jnp.concatenate([cmp_w2, z2], axis=-1),
                            jnp.concatenate([z2, cmp_w2], axis=-1)], axis=1).astype(BF16)
    return pe_t, w1bd, w2bd


def _kv_out(kv_t):
    n, _, rows = kv_t.shape
    return jnp.transpose(kv_t.reshape(n, 2, N_KV_HEADS, HEAD_DIM, rows), (0, 4, 1, 2, 3))[None]


def kernel(x_prompt, x_sample, cache_cmp_kv, cache_slc_kv, cache_win_kv, state_pool, page_table, norm_g, w_in,
           cmp_pe, cmp_w1, cmp_w2, pool_w, pool_scale, w_out, final_g):
    n, seq_len, d_model = x_prompt.shape
    nb = x_sample.shape[0]
    n_phys, page = cache_cmp_kv.shape[1], cache_cmp_kv.shape[2]
    n_pages = page_table.shape[1]
    past_len = n_pages * page
    wlen = cache_win_kv.shape[2]

    w_t, w_gates = _prep_w_in(w_in[0])
    pe_t, w1bd, w2bd = _prep_compress(cmp_pe[0], cmp_w1[0], cmp_w2[0])
    pool_w_b = pool_w[0].astype(BF16)
    w_out_b = w_out[0].astype(BF16)
    fg = final_g.reshape(1, d_model)

    (qt, crm, ckv_t, skv_t, wkv_t, ks, vst, kw, vwt, gates_t, sza, u, szp) = _project(
        x_prompt.reshape(n * seq_len, d_model), norm_g, w_t, w_gates, tm=512, seq_len=seq_len)
    r3 = lambda a: a.reshape(n, seq_len, a.shape[-1])
    kc, rhs_c = _compress_prompt(crm, n, pe_t, w1bd, w2bd)
    a3 = _attn_prompt(qt, gates_t, r3(sza), kc, rhs_c, ks, vst, kw, vwt)
    y_prompt = _out_prompt(a3, r3(u), r3(szp), x_prompt, pool_w_b, pool_scale, w_out_b, fg, tm=512)

    new_cmp_p = _kv_out(ckv_t)
    new_slc_p = _kv_out(skv_t)
    new_win_p = _kv_out(wkv_t[:, :, seq_len - min(WINDOW, seq_len):])
    new_pool_p = r3(u)[:, seq_len - POOL_STATE:][None]

    (qt_s, _, ckv_ts, skv_ts, wkv_ts, _, _, _, _, gates_ts, sza_s, u_s, szp_s) = _project(
        x_sample.reshape(nb, d_model), norm_g, w_t, w_gates, tm=nb, seq_len=nb)
    q_s = qt_s.T
    q5 = q_s.reshape(nb, N_KV_HEADS, GQA, 1, HEAD_DIM)
    eye = jnp.eye(N_KV_HEADS, dtype=q_s.dtype).reshape(1, N_KV_HEADS, 1, N_KV_HEADS, 1)
    qbd = (q5 * eye).reshape(nb, N_HEADS, LANES).astype(F32)

    to_pages = lambda c: jnp.transpose(c[0], (0, 2, 3, 4, 1)).reshape(n_phys, KV_W, page)
    o_c, imp8 = _cmp_sample(page_table, to_pages(cache_cmp_kv), qbd, pe_t, w1bd, w2bd)
    imp_g = jnp.transpose(imp8.reshape(nb * N_KV_HEADS, GQA, LANES), (1, 0, 2))
    n_blocks = -(-(past_len + 1) // SEL_BLOCK)
    notsel = _topk_sample(imp_g, t_pos=past_len, n_blocks=n_blocks)
    ns8 = jnp.repeat(notsel.reshape(nb, N_KV_HEADS, 1, LANES), GQA, axis=2).reshape(nb, N_HEADS, LANES)

    gates8 = jnp.transpose(gates_ts[:, :GQA * N_BRANCH, :], (2, 0, 1)).reshape(nb, N_HEADS, N_BRANCH)
    gates8 = jnp.pad(gates8, ((0, 0), (0, 0), (0, LANES - N_BRANCH)))
    win_t = jnp.transpose(cache_win_kv[0], (0, 2, 3, 4, 1)).reshape(nb, KV_W, wlen)
    s_new = skv_ts[0].T.reshape(nb, 1, KV_W)
    w_new = wkv_ts[0].T.reshape(nb, 1, KV_W)
    o8, nwin_t = _attn_sample(page_table, to_pages(cache_slc_kv), win_t, qbd, ns8, s_new, w_new, o_c, gates8)
    o5 = o8.reshape(nb, N_KV_HEADS, GQA, N_KV_HEADS, HEAD_DIM)
    o2 = jnp.stack([o5[:, k, :, k, :] for k in range(N_KV_HEADS)], axis=1).reshape(nb, D_ATTN)

    state_t = jnp.transpose(state_pool[0], (1, 0, 2))
    y_sample = _out_sample(o2, sza_s, u_s, state_t, szp_s, x_sample.reshape(nb, d_model),
                           pool_w_b, pool_scale, w_out_b, fg).reshape(nb, 1, d_model)

    kv_out_s = lambda t: jnp.transpose(t.reshape(2, N_KV_HEADS, HEAD_DIM, nb, 1), (3, 4, 0, 1, 2))[None]
    new_cmp_s = kv_out_s(ckv_ts)
    new_slc_s = kv_out_s(skv_ts)
    new_win_s = jnp.transpose(nwin_t.reshape(nb, 2, N_KV_HEADS, HEAD_DIM, wlen), (0, 4, 1, 2, 3))[None]
    new_pool_s = jnp.transpose(jnp.concatenate([state_t[1:], u_s[None]], axis=0), (1, 0, 2))[None]

    return (y_prompt, y_sample, new_cmp_p, new_slc_p, new_win_p, new_pool_p,
            new_cmp_s, new_slc_s, new_win_s, new_pool_s)
```

```python
import functools

import jax
import jax.numpy as jnp
from jax import lax
from jax.experimental import pallas as pl
from jax.experimental.pallas import tpu as pltpu

F32 = jnp.float32
BF16 = jnp.bfloat16

HEAD_DIM = 64
N_KV_HEADS = 2
GQA = 4
N_HEADS = N_KV_HEADS * GQA
D_ATTN = N_HEADS * HEAD_DIM
KV_W = 2 * N_KV_HEADS * HEAD_DIM
N_BRANCH = 3
D_POOL = 512
CMP_BLOCK = 32
CMP_STRIDE = 16
CMP_HIDDEN = 128
SEL_BLOCK = 64
N_SEL = 16
WINDOW = 512
Q_BLOCK = 256
POOL_WINDOWS = (2, 4, 8, 16)
assert all(w & (w - 1) == 0 for w in POOL_WINDOWS)
POOL_GROUP_DIM = 128
POOL_STATE = 15
RMS_EPS = 1e-6
LANES = 128
NEG_BIAS = -(2.0 ** 30)
NEG_FILL = -1e30
GATE_ROWS = 16
BF16_SUBLANES = 16
V_ROWS = HEAD_DIM + BF16_SUBLANES
KEY_TILE = 512
FORCED_VALUE = 1e9
N_FORCED = 3
CHUNK_PITCH = 20
SAMPLE_GROUP = 4
SEL_CHAINS = 2
LOG2E = 1.4426950408889634
VMEM_LIMIT = 48 * 1024 * 1024


def _cparams(sem):
    return pltpu.CompilerParams(dimension_semantics=sem, vmem_limit_bytes=VMEM_LIMIT)


def _silu(z):
    return z * jax.nn.sigmoid(z)


def _proj_kernel(x_ref, g_ref, w_ref, wg_ref, qt_ref, crm_ref, ct_ref, st_ref, wt_ref, ks_ref, vst_ref, kw_ref,
                 vwt_ref, gatet_ref, sza_ref, u_ref, szp_ref, *, tm, seq_len):
    i = pl.program_id(0)
    x = x_ref[...]
    ms = jnp.mean(x * x, axis=-1, keepdims=True)
    xn = (x * lax.rsqrt(ms + RMS_EPS) * g_ref[...]).astype(BF16)
    nt = (((1,), (1,)), ((), ()))

    def mm(c0, c1):
        return lax.dot_general(xn, w_ref[c0:c1, :].astype(BF16), nt, preferred_element_type=F32)

    qt_ref[...] = (mm(0, D_ATTN) * (HEAD_DIM ** -0.5 * LOG2E)).T.astype(BF16)
    kv = mm(D_ATTN, D_ATTN + 3 * KV_W)
    crm_ref[0] = kv[:, 0:LANES]
    crm_ref[1] = kv[:, LANES:2 * LANES]
    kvt = kv.T
    ct_ref[...] = kvt[0:KV_W, :]
    st_ref[...] = kvt[KV_W:2 * KV_W, :]
    wt_ref[...] = kvt[2 * KV_W:3 * KV_W, :]

    lane = lax.broadcasted_iota(jnp.int32, (tm, LANES), 1)
    pos = (i * tm) % seq_len + lax.broadcasted_iota(jnp.int32, (tm, LANES), 0)
    onehot = jnp.where(lane - HEAD_DIM == (pos // SEL_BLOCK) % HEAD_DIM, NEG_BIAS, 0.0).astype(F32)
    ones_rows = jnp.where(lax.broadcasted_iota(jnp.int32, (V_ROWS - HEAD_DIM, tm), 0) == 0, 1.0, 0.0).astype(BF16)
    for kvh in range(N_KV_HEADS):
        for (base, k_dst, v_dst, k_pad) in ((KV_W, ks_ref, vst_ref, onehot), (2 * KV_W, kw_ref, vwt_ref, 0.0)):
            slab = kv[:, base:base + LANES]
            if kvh == 1:
                slab = pltpu.roll(slab, HEAD_DIM, 1)
            k_dst[kvh, :, :] = jnp.where(lane < HEAD_DIM, slab, k_pad).astype(BF16)
            v0 = base + LANES + kvh * HEAD_DIM
            v_dst[kvh, 0:HEAD_DIM, :] = kvt[v0:v0 + HEAD_DIM, :].astype(BF16)
            v_dst[kvh, HEAD_DIM:V_ROWS, :] = ones_rows

    c = D_ATTN + 3 * KV_W + N_HEADS * N_BRANCH
    sza_ref[...] = _silu(mm(c, c + 512)).astype(BF16)
    u_ref[...] = mm(c + 512, c + 1024)
    szp_ref[...] = _silu(mm(c + 1024, c + 1536)).astype(BF16)
    gate_t = jax.nn.sigmoid(lax.dot_general(wg_ref[...].astype(BF16), xn, nt, preferred_element_type=F32))
    for kvh in range(N_KV_HEADS):
        gatet_ref[kvh, :, :] = gate_t[kvh * GATE_ROWS:(kvh + 1) * GATE_ROWS, :]


def _project(x2d, norm_g, w_t, w_gates, *, tm, seq_len):
    rows = x2d.shape[0]
    nt = rows // tm
    tps = seq_len // tm
    row_blk = lambda w: pl.BlockSpec((tm, w), lambda i: (i, 0))
    kv_t = jax.ShapeDtypeStruct((rows // seq_len, KV_W, seq_len), F32)
    kv_t_blk = pl.BlockSpec((None, KV_W, tm), lambda i: (i // tps, 0, i % tps))
    k_rm = jax.ShapeDtypeStruct((N_KV_HEADS, rows, LANES), BF16)
    k_rm_blk = pl.BlockSpec((N_KV_HEADS, tm, LANES), lambda i: (0, i, 0))
    v_t = jax.ShapeDtypeStruct((N_KV_HEADS, V_ROWS, rows), BF16)
    v_t_blk = pl.BlockSpec((N_KV_HEADS, V_ROWS, tm), lambda i: (0, 0, i))
    out_shape = (
        jax.ShapeDtypeStruct((D_ATTN, rows), BF16),
        jax.ShapeDtypeStruct((2, rows, LANES), F32),
        kv_t, kv_t, kv_t,
        k_rm,
        v_t,
        k_rm,
        v_t,
        jax.ShapeDtypeStruct((N_KV_HEADS, GATE_ROWS, rows), F32),
        jax.ShapeDtypeStruct((rows, 512), BF16),
        jax.ShapeDtypeStruct((rows, 512), F32),
        jax.ShapeDtypeStruct((rows, 512), BF16),
    )
    out_specs = (
        pl.BlockSpec((D_ATTN, tm), lambda i: (0, i)),
        pl.BlockSpec((2, tm, LANES), lambda i: (0, i, 0)),
        kv_t_blk, kv_t_blk, kv_t_blk,
        k_rm_blk, v_t_blk, k_rm_blk, v_t_blk,
        pl.BlockSpec((N_KV_HEADS, GATE_ROWS, tm), lambda i: (0, 0, i)),
        row_blk(512), row_blk(512), row_blk(512),
    )
    return pl.pallas_call(
        functools.partial(_proj_kernel, tm=tm, seq_len=seq_len),
        grid=(nt,),
        in_specs=[row_blk(x2d.shape[1]),
                  pl.BlockSpec((1, x2d.shape[1]), lambda i: (0, 0)),
                  pl.BlockSpec(w_t.shape, lambda i: (0, 0), pipeline_mode=pl.Buffered(1)),
                  pl.BlockSpec(w_gates.shape, lambda i: (0, 0))],
        out_specs=out_specs,
        out_shape=out_shape,
        compiler_params=_cparams(("arbitrary",)),
        name="proj",
    )(x2d, norm_g, w_t, w_gates)


def _compress_hidden(load_rows, pe_ref, w1_ref, kv, m):
    xs = [load_rows(s) for s in range(CMP_STRIDE)]
    hs = []
    for sub in range(CMP_BLOCK // CMP_STRIDE):
        lhs = jnp.concatenate([(xs[s] + pe_ref[kv, sub, s]).astype(BF16) for s in range(CMP_STRIDE)], axis=1)
        hs.append(jnp.dot(lhs, w1_ref[kv, sub], preferred_element_type=F32))
    return hs[0] + pltpu.roll(hs[1], m - 1, 0)


def _cover_matrix(nc_pad, ns_pad):
    c0 = lax.broadcasted_iota(jnp.int32, (nc_pad, ns_pad), 0) * CMP_STRIDE
    s0 = lax.broadcasted_iota(jnp.int32, (nc_pad, ns_pad), 1) * SEL_BLOCK
    return jnp.where((c0 < s0 + SEL_BLOCK) & (c0 + CMP_BLOCK > s0), 1.0, 0.0).astype(F32)


def _compress_prompt_kernel(c_ref, pe_ref, w1_ref, w2_ref, kc_ref, rhs_ref, *, nchunk):
    ones_rows = jnp.where(lax.broadcasted_iota(jnp.int32, (HEAD_DIM, nchunk), 0) == 0, 1.0, 0.0).astype(BF16)
    s0 = lax.broadcasted_iota(jnp.int32, (LANES, nchunk), 0) * SEL_BLOCK
    c0 = lax.broadcasted_iota(jnp.int32, (LANES, nchunk), 1) * CMP_STRIDE
    cover_t = jnp.where((c0 < s0 + SEL_BLOCK) & (c0 + CMP_BLOCK > s0), 1.0, 0.0).astype(BF16)
    for kv in range(2):
        load = lambda s: c_ref[kv, pl.ds(s, nchunk, stride=CMP_STRIDE), :]
        h = _compress_hidden(load, pe_ref, w1_ref, kv, nchunk)
        full = jnp.dot(_silu(h).astype(BF16), w2_ref[kv], preferred_element_type=F32)
        if kv == 0:
            kc_ref[0, :, :] = full[:, 0:HEAD_DIM].astype(BF16)
            kc_ref[1, :, :] = full[:, HEAD_DIM:2 * HEAD_DIM].astype(BF16)
        else:
            full_t = full.T
            for kvh in range(N_KV_HEADS):
                rhs_ref[kvh, 0:HEAD_DIM, :] = full_t[kvh * HEAD_DIM:(kvh + 1) * HEAD_DIM, :].astype(BF16)
                rhs_ref[kvh, HEAD_DIM:LANES, :] = ones_rows
                rhs_ref[kvh, LANES:2 * LANES, :] = cover_t


def _compress_prompt(crm, n, pe_t, w1bd, w2bd):
    seq_len = crm.shape[1] // n
    nchunk = seq_len // CMP_STRIDE
    return pl.pallas_call(
        functools.partial(_compress_prompt_kernel, nchunk=nchunk),
        grid=(n,),
        in_specs=[pl.BlockSpec((2, seq_len, LANES), lambda b: (0, b, 0)),
                  pl.BlockSpec(pe_t.shape, lambda b: (0, 0, 0, 0, 0)),
                  pl.BlockSpec(w1bd.shape, lambda b: (0, 0, 0, 0)),
                  pl.BlockSpec(w2bd.shape, lambda b: (0, 0, 0))],
        out_specs=(pl.BlockSpec((None, N_KV_HEADS, nchunk, HEAD_DIM), lambda b: (b, 0, 0, 0)),
                   pl.BlockSpec((None, N_KV_HEADS, 2 * LANES, nchunk), lambda b: (b, 0, 0, 0))),
        out_shape=(jax.ShapeDtypeStruct((n, N_KV_HEADS, nchunk, HEAD_DIM), BF16),
                   jax.ShapeDtypeStruct((n, N_KV_HEADS, 2 * LANES, nchunk), BF16)),
        compiler_params=_cparams(("arbitrary",)),
        name="compress_prompt",
    )(crm, pe_t, w1bd, w2bd)


def _not_selected(val, n_top, axis):
    blk = lax.broadcasted_iota(jnp.int32, val.shape, axis).astype(F32)
    forced = val >= FORCED_VALUE
    notsel = jnp.where(forced, 0.0, 1.0).astype(F32)
    val = jnp.where(forced, -3e38, val)
    for _ in range(max(n_top - N_FORCED, 0)):
        m = jnp.max(val, axis=axis, keepdims=True)
        idx = jnp.min(jnp.where(val == m, blk, float(LANES)), axis=axis, keepdims=True)
        pick = blk == idx
        notsel = jnp.where(pick, 0.0, notsel)
        val = jnp.where(pick, -3e38, val)
    return notsel


def _masked_importance(imp, t, n_blocks, axis):
    blk = lax.broadcasted_iota(jnp.int32, imp.shape, axis)
    cur = t // SEL_BLOCK
    forced = (blk == 0) | (blk == cur) | (blk == cur - 1)
    val = jnp.where(forced, FORCED_VALUE, jnp.where(blk * SEL_BLOCK <= t, imp, -1e9))
    return jnp.where(blk < n_blocks, val, -2e9)


def _attn_prompt_kernel(qt_ref, gatet_ref, sza_ref, kc_ref, rhs_ref, ks_ref, vst_ref, kw_ref, vwt_ref, out_ref,
                        qsel_ref, sbuf_ref, smax_ref, m_ref, acc_ref, mw_ref, accw_ref, oc_ref, bias_ref,
                        *, n_sel_blocks, seq_len):
    qb = pl.program_id(2)
    qs = qb * Q_BLOCK
    hq = GQA * Q_BLOCK
    t_row = qs + lax.broadcasted_iota(jnp.int32, (1, Q_BLOCK), 1)
    t_all = jnp.concatenate([t_row] * GQA, axis=1)
    qt = qt_ref[...]
    qt_all = jnp.concatenate([qt[g * HEAD_DIM:(g + 1) * HEAD_DIM, :] for g in range(GQA)], axis=1)
    zeros_lo = jnp.zeros((HEAD_DIM, hq), BF16)
    qw = jnp.concatenate([qt_all, zeros_lo], axis=0)

    tile = min(KEY_TILE, seq_len)
    n_full = qs // tile
    lo_tiles = HEAD_DIM * SEL_BLOCK // tile
    wwidth = min(WINDOW, seq_len)

    @pl.when(qb == 0)
    def _():
        k_rel = lax.broadcasted_iota(jnp.int32, bias_ref.shape, 0) - wwidth
        q_loc = lax.broadcasted_iota(jnp.int32, bias_ref.shape, 1) % Q_BLOCK
        bias_ref[...] = jnp.where((k_rel > q_loc - WINDOW) & (k_rel <= q_loc), 0.0, NEG_FILL)

    def mask_bias(start, width):
        return bias_ref[pl.ds(pl.multiple_of(start - qs + wwidth, LANES), width), :]

    diag_start = jnp.where(qs >= wwidth, qs, qs + Q_BLOCK)

    def flash_steps(steps):
        staged = []
        for (q, k_ref, vt_ref, start, width, bias, mref, aref, c) in steps:
            start = pl.multiple_of(start, LANES)
            s = jnp.dot(k_ref[pl.ds(start, width), :], q, preferred_element_type=F32)
            if bias is not None:
                s = s + bias
            staged.append((s, vt_ref[:, pl.ds(start, width)]))
        softmaxed = []
        for (s, vt), step in zip(staged, steps):
            mref, c = step[6], step[8]
            m_old = mref[c]
            m_new = jnp.maximum(m_old, jnp.max(s, axis=0, keepdims=True))
            softmaxed.append((jnp.exp2(s - m_new).astype(BF16), jnp.exp2(m_old - m_new), m_new, vt))
        for (p, alpha, m_new, vt), step in zip(softmaxed, steps):
            mref, aref, c = step[6], step[7], step[8]
            aref[c] = alpha * aref[c] + jnp.dot(vt, p, preferred_element_type=F32)
            mref[c] = m_new

    m_ref[...] = jnp.full(m_ref.shape, NEG_FILL, F32)
    acc_ref[...] = jnp.zeros(acc_ref.shape, F32)
    mw_ref[...] = jnp.full(mw_ref.shape, NEG_FILL, F32)
    accw_ref[...] = jnp.zeros(accw_ref.shape, F32)

    nc = kc_ref.shape[0]
    c_last = lax.broadcasted_iota(jnp.int32, (nc, hq), 0) * CMP_STRIDE + (CMP_BLOCK - 1)
    ok_c = c_last <= t_all
    s = jnp.dot(kc_ref[...], qt_all, preferred_element_type=F32)
    flash_steps([(qw, kw_ref, vwt_ref, qs, Q_BLOCK, mask_bias(diag_start, Q_BLOCK), mw_ref, accw_ref, 1)])
    s = jnp.where(ok_c, s, NEG_FILL)
    e = jnp.exp2(s - jnp.max(s, axis=0, keepdims=True))
    r = jnp.dot(rhs_ref[...], e.astype(BF16), preferred_element_type=F32)
    inv = jnp.where(t_all >= CMP_BLOCK - 1, 1.0 / r[HEAD_DIM:HEAD_DIM + 1, :], 0.0)
    qw_late = jnp.where(inv > -1.0, qw, jnp.zeros_like(qw))
    band_start = jnp.maximum(qs - wwidth, 0)
    flash_steps([(qw_late, kw_ref, vwt_ref, band_start, wwidth, mask_bias(band_start, wwidth), mw_ref, accw_ref, 0)])
    oc_ref[...] = r[0:HEAD_DIM, :] * inv
    imp_all = r[LANES:2 * LANES, :] * inv
    imp = imp_all[:, 0:Q_BLOCK]
    for g in range(1, GQA):
        imp = imp + imp_all[:, g * Q_BLOCK:(g + 1) * Q_BLOCK]

    notsel = _not_selected(_masked_importance(imp, t_row, n_sel_blocks, 0), min(N_SEL, n_sel_blocks), 0)
    notsel = notsel.astype(BF16)
    q_lo = jnp.concatenate([qt_all, jnp.concatenate([notsel[0:HEAD_DIM, :]] * GQA, axis=1)], axis=0)
    q_hi = jnp.concatenate([qt_all, jnp.concatenate([notsel[HEAD_DIM:2 * HEAD_DIM, :]] * GQA, axis=1)], axis=0)
    qsel_ref[0] = q_lo
    qsel_ref[1] = q_hi

    def scores(b, kt, masked):
        start = pl.multiple_of(kt * tile, LANES)
        q = qsel_ref[jnp.where(kt >= lo_tiles, 1, 0)]
        s = jnp.dot(ks_ref[pl.ds(start, tile), :], q, preferred_element_type=F32)
        if masked:
            s = s + mask_bias(start, tile)
        sbuf_ref[b] = s
        smax_ref[b] = jnp.max(s, axis=0, keepdims=True)

    def consume(b, kt):
        start = pl.multiple_of(kt * tile, LANES)
        m_old = m_ref[b]
        m_new = jnp.maximum(m_old, smax_ref[b])
        p = jnp.exp2(sbuf_ref[b] - m_new).astype(BF16)
        acc_ref[b] = jnp.exp2(m_old - m_new) * acc_ref[b] + jnp.dot(vst_ref[:, pl.ds(start, tile)], p,
                                                                  preferred_element_type=F32)
        m_ref[b] = m_new

    @pl.when(n_full == 0)
    def _():
        scores(0, 0, True)

    @pl.when(n_full > 0)
    def _():
        scores(0, 0, False)

    def pair_step(j):
        scores(1, 2 * j + 1, False)
        consume(0, 2 * j)
        scores(0, 2 * j + 2, False)
        consume(1, 2 * j + 1)

    def pair_body(j, carry):
        pair_step(j)
        return carry

    def quad_body(j, carry):
        pair_step(2 * j)
        pair_step(2 * j + 1)
        return carry

    n_pairs = jnp.maximum(n_full - 1, 0) // 2
    lax.fori_loop(0, n_pairs // 2, quad_body, 0)
    lax.fori_loop(n_pairs // 2 * 2, n_pairs, pair_body, 0)

    @pl.when((n_full > 0) & (n_full % 2 == 0))
    def _():
        scores(1, n_full - 1, False)
        consume(0, n_full - 2)
        scores(0, n_full, True)
        consume(1, n_full - 1)
        consume(0, n_full)

    @pl.when(n_full % 2 == 1)
    def _():
        scores(1, n_full, True)
        consume(0, n_full - 1)
        consume(1, n_full)

    @pl.when(n_full == 0)
    def _():
        consume(0, 0)

    def merged(mref, aref, n_chain):
        m = mref[0]
        for c in range(1, n_chain):
            m = jnp.maximum(m, mref[c])
        a = jnp.exp2(mref[0] - m) * aref[0]
        for c in range(1, n_chain):
            a = a + jnp.exp2(mref[c] - m) * aref[c]
        return a[0:HEAD_DIM, :] * (1.0 / a[HEAD_DIM:HEAD_DIM + 1, :])

    gates = gatet_ref[...]
    sza = sza_ref[...]
    o_s = merged(m_ref, acc_ref, SEL_CHAINS)
    o_w = merged(mw_ref, accw_ref, 2)
    o_c = oc_ref[...]
    outs = []
    for g in range(GQA):
        cols = slice(g * Q_BLOCK, (g + 1) * Q_BLOCK)
        gc = gates[g * N_BRANCH + 0:g * N_BRANCH + 1, :]
        gs = gates[g * N_BRANCH + 1:g * N_BRANCH + 2, :]
        gw = gates[g * N_BRANCH + 2:g * N_BRANCH + 3, :]
        outs.append((gc * o_c[:, cols] + gs * o_s[:, cols] + gw * o_w[:, cols]).T)
    out_ref[...] = (jnp.concatenate(outs, axis=1) * sza).astype(BF16)


def _attn_prompt(qt, gates_t, sza3, kc, rhs_c, ks, vst, kw, vwt):
    n, seq_len, _ = sza3.shape
    nqb = seq_len // Q_BLOCK
    nchunk = kc.shape[2]
    hw = GQA * HEAD_DIM
    k_blk = pl.BlockSpec((None, seq_len, LANES), lambda b, k, i: (k, b, 0))
    vt_blk = pl.BlockSpec((None, V_ROWS, seq_len), lambda b, k, i: (k, 0, b))
    return pl.pallas_call(
        functools.partial(_attn_prompt_kernel, n_sel_blocks=-(-seq_len // SEL_BLOCK), seq_len=seq_len),
        grid=(n, N_KV_HEADS, nqb),
        in_specs=[
            pl.BlockSpec((hw, Q_BLOCK), lambda b, k, i: (k, b * nqb + i)),
            pl.BlockSpec((None, GATE_ROWS, Q_BLOCK), lambda b, k, i: (k, 0, b * nqb + i)),
            pl.BlockSpec((None, Q_BLOCK, hw), lambda b, k, i: (b, i, k)),
            pl.BlockSpec((None, None, nchunk, HEAD_DIM), lambda b, k, i: (b, k, 0, 0)),
            pl.BlockSpec((None, None, 2 * LANES, nchunk), lambda b, k, i: (b, k, 0, 0)),
            k_blk, vt_blk, k_blk, vt_blk,
        ],
        out_specs=pl.BlockSpec((None, Q_BLOCK, hw), lambda b, k, i: (b, i, k)),
        out_shape=jax.ShapeDtypeStruct((n, seq_len, D_ATTN), BF16),
        scratch_shapes=[
            pltpu.VMEM((2, 2 * HEAD_DIM, GQA * Q_BLOCK), BF16),
            pltpu.VMEM((SEL_CHAINS, min(KEY_TILE, seq_len), GQA * Q_BLOCK), F32),
            pltpu.VMEM((SEL_CHAINS, 1, GQA * Q_BLOCK), F32),
            pltpu.VMEM((SEL_CHAINS, 1, GQA * Q_BLOCK), F32),
            pltpu.VMEM((SEL_CHAINS, V_ROWS, GQA * Q_BLOCK), F32),
            pltpu.VMEM((2, 1, GQA * Q_BLOCK), F32),
            pltpu.VMEM((2, V_ROWS, GQA * Q_BLOCK), F32),
            pltpu.VMEM((HEAD_DIM, GQA * Q_BLOCK), F32),
            pltpu.VMEM((min(WINDOW, seq_len) + min(KEY_TILE, seq_len), GQA * Q_BLOCK), F32),
        ],
        compiler_params=_cparams(("arbitrary", "arbitrary", "arbitrary")),
        name="attn_prompt",
    )(qt, gates_t, sza3, kc, rhs_c, ks, vst, kw, vwt)


def _pool_out(d, pw_ref, ps_ref, szp):
    ys = [jnp.dot(d[:, g * POOL_GROUP_DIM:(g + 1) * POOL_GROUP_DIM].astype(BF16), pw_ref[g],
                  preferred_element_type=F32) for g in range(len(POOL_WINDOWS))]
    return jnp.concatenate(ys, axis=1) * ps_ref[...] * szp


def _finish(x, a_bf16, b, wo_ref, fg_ref):
    mix = jnp.concatenate([a_bf16, b.astype(BF16)], axis=1)
    y = x + jnp.dot(mix, wo_ref[...], preferred_element_type=F32)
    ms = jnp.mean(y * y, axis=-1, keepdims=True)
    return y * lax.rsqrt(ms + RMS_EPS) * fg_ref[...]


def _out_prompt_kernel(a_ref, u_ref, halo_ref, szp_ref, x_ref, pw_ref, ps_ref, wo_ref, fg_ref, y_ref, ext_ref,
                       *, tm, halo):
    i = pl.program_id(1)
    u = u_ref[...]
    ext_ref[0:halo, :] = jnp.where(i > 0, halo_ref[...], 0.0)
    ext_ref[halo:halo + tm, :] = u
    pos = i * tm + lax.broadcasted_iota(jnp.int32, (tm, POOL_GROUP_DIM), 0)
    ds = []
    for g, w in enumerate(POOL_WINDOWS):
        c0 = g * POOL_GROUP_DIM
        run = ext_ref[:, c0:c0 + POOL_GROUP_DIM]
        k = 1
        while k < w:
            run = run + pltpu.roll(run, k, 0)
            k *= 2
        cnt = jnp.minimum(pos + 1, w).astype(F32)
        ds.append(run[halo:halo + tm, :] / cnt - u[:, c0:c0 + POOL_GROUP_DIM])
    b = _pool_out(jnp.concatenate(ds, axis=1), pw_ref, ps_ref, szp_ref[...])
    y_ref[...] = _finish(x_ref[...], a_ref[...], b, wo_ref, fg_ref)


def _out_prompt(a3, u3, szp3, x3, pool_w, pool_scale, w_out, final_g, *, tm):
    n, seq_len, d_model = x3.shape
    halo = 16
    nt = seq_len // tm
    blk = lambda w: pl.BlockSpec((None, tm, w), lambda b, i: (b, i, 0))
    const = lambda a: pl.BlockSpec(a.shape, lambda b, i: (0,) * a.ndim)
    return pl.pallas_call(
        functools.partial(_out_prompt_kernel, tm=tm, halo=halo),
        grid=(n, nt),
        in_specs=[blk(D_ATTN), blk(D_POOL),
                  pl.BlockSpec((None, halo, D_POOL), lambda b, i: (b, jnp.maximum(i * (tm // halo) - 1, 0), 0)),
                  blk(D_POOL), blk(d_model),
                  const(pool_w), const(pool_scale), const(w_out), const(final_g)],
        out_specs=blk(d_model),
        out_shape=jax.ShapeDtypeStruct((n, seq_len, d_model), F32),
        scratch_shapes=[pltpu.VMEM((tm + halo, D_POOL), F32)],
        compiler_params=_cparams(("arbitrary", "arbitrary")),
        name="out_prompt",
    )(a3, u3, u3, szp3, x3, pool_w, pool_scale, w_out, final_g)


def _out_sample_kernel(o_ref, sza_ref, u_ref, st_ref, szp_ref, x_ref, pw_ref, ps_ref, wo_ref, fg_ref, y_ref):
    u = u_ref[...]
    ds = []
    for g, w in enumerate(POOL_WINDOWS):
        c0 = g * POOL_GROUP_DIM
        acc = u[:, c0:c0 + POOL_GROUP_DIM]
        for k in range(1, w):
            acc = acc + st_ref[POOL_STATE - k, :, c0:c0 + POOL_GROUP_DIM]
        ds.append(acc / float(w) - u[:, c0:c0 + POOL_GROUP_DIM])
    b = _pool_out(jnp.concatenate(ds, axis=1), pw_ref, ps_ref, szp_ref[...])
    a = (o_ref[...] * sza_ref[...]).astype(BF16)
    y_ref[...] = _finish(x_ref[...], a, b, wo_ref, fg_ref)


def _out_sample(o2, sza, u, state_t, szp, x2, pool_w, pool_scale, w_out, final_g):
    args = (o2, sza, u, state_t, szp, x2, pool_w, pool_scale, w_out, final_g)
    full = lambda a: pl.BlockSpec(a.shape, lambda i: (0,) * a.ndim)
    return pl.pallas_call(
        _out_sample_kernel,
        grid=(1,),
        in_specs=[full(a) for a in args],
        out_specs=full(x2),
        out_shape=jax.ShapeDtypeStruct(x2.shape, F32),
        compiler_params=_cparams(("arbitrary",)),
        name="out_sample",
    )(*args)


def _page_copy(pt_ref, cache_ref, buf_ref, sem_ref, step, slot, j, n_pages, n_seq):
    g, p = divmod(j, n_pages)
    return pltpu.make_async_copy(cache_ref.at[pt_ref[step * n_seq + g, p]], buf_ref.at[slot, j], sem_ref.at[slot])


def _fetch_pages(pt_ref, cache_ref, buf_ref, sem_ref, n_pages, n_seq):
    i = pl.program_id(0)
    slot = i % 2
    n_copy = n_seq * n_pages

    @pl.when(i == 0)
    def _():
        for j in range(n_copy):
            _page_copy(pt_ref, cache_ref, buf_ref, sem_ref, 0, 0, j, n_pages, n_seq).start()

    @pl.when(i + 1 < pl.num_programs(0))
    def _():
        for j in range(n_copy):
            _page_copy(pt_ref, cache_ref, buf_ref, sem_ref, i + 1, 1 - slot, j, n_pages, n_seq).start()

    for j in range(n_copy):
        _page_copy(pt_ref, cache_ref, buf_ref, sem_ref, i, slot, j, n_pages, n_seq).wait()
    return slot


def _cmp_sample_kernel(pt_ref, cache_ref, qbd_ref, pe_ref, w1_ref, w2_ref, oc_ref, imp_ref, rows_ref, buf_ref,
                       sem_ref, *, n_pages, page, n_seq):
    slot = _fetch_pages(pt_ref, cache_ref, buf_ref, sem_ref, n_pages, n_seq)
    pages = [buf_ref.at[slot, j] for j in range(n_seq * n_pages)]
    nchunk = n_pages * page // CMP_STRIDE
    fulls = []
    m = n_seq * nchunk
    cpp = page // CMP_STRIDE
    for kv in range(2):
        for i, page_ref in enumerate(pages):
            half = page_ref[kv * LANES:(kv + 1) * LANES, :].astype(BF16)
            rows = half.T.astype(F32)
            for c in range(cpp):
                dst = (i * cpp + c) * CHUNK_PITCH
                rows_ref[kv, dst:dst + CMP_STRIDE, :] = rows[c * CMP_STRIDE:(c + 1) * CMP_STRIDE, :]
        load = lambda s: rows_ref[kv, pl.ds(s, m, stride=CHUNK_PITCH), :]
        h = _compress_hidden(load, pe_ref, w1_ref, kv, m)
        fulls.append(jnp.dot(_silu(h).astype(BF16), w2_ref[kv], preferred_element_type=F32).astype(BF16))
    k_c, v_c = fulls
    nrow = n_seq * N_HEADS
    qbd = qbd_ref[...].reshape(nrow, LANES).astype(BF16)
    s = lax.dot_general(qbd, k_c, (((1,), (1,)), ((), ())), preferred_element_type=F32)
    col = lax.broadcasted_iota(jnp.int32, s.shape, 1)
    own = col // nchunk == lax.broadcasted_iota(jnp.int32, s.shape, 0) // N_HEADS
    ok = own & (col % nchunk < nchunk - 1)
    s = jnp.where(ok, s, NEG_FILL)
    e = jnp.where(ok, jnp.exp2(s - jnp.max(s, axis=1, keepdims=True)), 0.0)
    pc = (e * (1.0 / jnp.sum(e, axis=1, keepdims=True))).astype(BF16)
    cover = jnp.concatenate([_cover_matrix(nchunk, LANES).astype(BF16)] * n_seq, axis=0)
    oc_ref[...] = jnp.dot(pc, v_c, preferred_element_type=F32).reshape(n_seq, N_HEADS, LANES)
    imp_ref[...] = jnp.dot(pc, cover, preferred_element_type=F32).reshape(n_seq, N_HEADS, LANES)


def _cmp_sample(page_table, cache_t, qbd, pe_t, w1bd, w2bd):
    nb, n_pages = page_table.shape
    page = cache_t.shape[-1]
    n_seq = SAMPLE_GROUP
    const = lambda a: pl.BlockSpec(a.shape, lambda b, pt: (0,) * a.ndim)
    per_b = pl.BlockSpec((n_seq, N_HEADS, LANES), lambda b, pt: (b, 0, 0))
    grid_spec = pltpu.PrefetchScalarGridSpec(
        num_scalar_prefetch=1,
        grid=(nb // n_seq,),
        in_specs=[pl.BlockSpec(memory_space=pl.ANY), per_b, const(pe_t), const(w1bd), const(w2bd)],
        out_specs=(per_b, per_b),
        scratch_shapes=[pltpu.VMEM((2, n_seq * n_pages * page // CMP_STRIDE * CHUNK_PITCH, LANES), F32),
                        pltpu.VMEM((2, n_seq * n_pages, KV_W, page), F32),
                        pltpu.SemaphoreType.DMA((2,))],
    )
    return pl.pallas_call(
        functools.partial(_cmp_sample_kernel, n_pages=n_pages, page=page, n_seq=n_seq),
        grid_spec=grid_spec,
        out_shape=(jax.ShapeDtypeStruct((nb, N_HEADS, LANES), F32),
                   jax.ShapeDtypeStruct((nb, N_HEADS, LANES), F32)),
        compiler_params=_cparams(("arbitrary",)),
        name="cmp_sample",
    )(page_table, cache_t, qbd, pe_t, w1bd, w2bd)


def _topk_sample_kernel(imp_ref, out_ref, *, t_pos, n_blocks):
    imp = imp_ref[0] + imp_ref[1] + imp_ref[2] + imp_ref[3]
    t = jnp.full((imp.shape[0], 1), t_pos, jnp.int32)
    out_ref[...] = _not_selected(_masked_importance(imp, t, n_blocks, 1), min(N_SEL, n_blocks), 1)


def _topk_sample(imp_g, *, t_pos, n_blocks):
    rows = imp_g.shape[1]
    return pl.pallas_call(
        functools.partial(_topk_sample_kernel, t_pos=t_pos, n_blocks=n_blocks),
        grid=(1,),
        in_specs=[pl.BlockSpec(imp_g.shape, lambda i: (0, 0, 0))],
        out_specs=pl.BlockSpec((rows, LANES), lambda i: (0, 0)),
        out_shape=jax.ShapeDtypeStruct((rows, LANES), F32),
        compiler_params=_cparams(("arbitrary",)),
        name="topk_sample",
    )(imp_g)


def _attn_sample_kernel(pt_ref, *refs, n_pages, page, n_seq):
    (cache_ref, win_ref, qbd_ref, ns_ref, snew_ref, wnew_ref, oc_ref, gate_ref, o_ref, nwin_ref,
     buf_ref, sem_ref) = refs
    slot = _fetch_pages(pt_ref, cache_ref, buf_ref, sem_ref, n_pages, n_seq)
    pages = [buf_ref.at[slot, j] for j in range(n_seq * n_pages)]
    nrow = n_seq * N_HEADS
    qf = qbd_ref[...].reshape(nrow, LANES)
    qbd = qf.astype(BF16)
    ns = ns_ref[...].reshape(nrow, LANES)
    row_seq = lax.broadcasted_iota(jnp.int32, (nrow, 1), 0) // N_HEADS
    nt_ = (((1,), (1,)), ((), ()))

    def new_token(rows_ref):
        rows = jnp.concatenate([jnp.broadcast_to(rows_ref[g], (N_HEADS, KV_W)) for g in range(n_seq)], axis=0)
        k_new = rows[:, 0:LANES].astype(BF16).astype(F32)
        v_new = rows[:, LANES:2 * LANES].astype(BF16).astype(F32)
        return jnp.sum(qf * k_new, axis=1, keepdims=True), v_new

    lane = lax.broadcasted_iota(jnp.int32, (nrow, page), 1)
    per_page = page // SEL_BLOCK
    scores = []
    for i, page_ref in enumerate(pages):
        g, p = divmod(i, n_pages)
        s = jnp.dot(qbd, page_ref[0:LANES, :].astype(BF16), preferred_element_type=F32)
        flag = ns[:, p * per_page:p * per_page + 1]
        for j in range(1, per_page):
            flag = jnp.where(lane >= j * SEL_BLOCK, ns[:, p * per_page + j:p * per_page + j + 1], flag)
        scores.append(jnp.where((row_seq == g) & (flag < 0.5), s, NEG_FILL))
    nb_cache = n_pages * per_page
    s_new, v_new = new_token(snew_ref)
    s_new = jnp.where(ns[:, nb_cache:nb_cache + 1] > 0.5, NEG_FILL, s_new)
    s_max = scores[0]
    for s in scores[1:]:
        s_max = jnp.maximum(s_max, s)
    m = jnp.maximum(s_new, jnp.max(s_max, axis=1, keepdims=True))
    e_new = jnp.exp2(s_new - m)
    e_sum = jnp.zeros((nrow, page), F32)
    acc = e_new * v_new
    for s, page_ref in zip(scores, pages):
        e = jnp.exp2(s - m).astype(BF16)
        e_sum = e_sum + e.astype(F32)
        acc = acc + lax.dot_general(e, page_ref[LANES:2 * LANES, :].astype(BF16), nt_, preferred_element_type=F32)
    o_s = acc * (1.0 / (e_new + jnp.sum(e_sum, axis=1, keepdims=True)))

    wlen = win_ref.shape[2]
    first = max(wlen - WINDOW + 1, 0)
    in_win = lax.broadcasted_iota(jnp.int32, (nrow, wlen), 1) >= first
    scores_w = []
    for g in range(n_seq):
        s_w = jnp.dot(qbd, win_ref[g, 0:LANES, :].astype(BF16), preferred_element_type=F32)
        scores_w.append(jnp.where((row_seq == g) & in_win, s_w, NEG_FILL))
    sw_new, vw_new = new_token(wnew_ref)
    sw_max = scores_w[0]
    for s_w in scores_w[1:]:
        sw_max = jnp.maximum(sw_max, s_w)
    m_w = jnp.maximum(sw_new, jnp.max(sw_max, axis=1, keepdims=True))
    ew_new = jnp.exp2(sw_new - m_w)
    ew_sum = jnp.zeros((nrow, wlen), F32)
    acc_w = ew_new * vw_new
    for g in range(n_seq):
        e_w = jnp.exp2(scores_w[g] - m_w).astype(BF16)
        ew_sum = ew_sum + e_w.astype(F32)
        acc_w = acc_w + lax.dot_general(e_w, win_ref[g, LANES:2 * LANES, :].astype(BF16), nt_,
                                        preferred_element_type=F32)
    o_w = acc_w * (1.0 / (ew_new + jnp.sum(ew_sum, axis=1, keepdims=True)))

    gates = gate_ref[...].reshape(nrow, LANES)
    o = gates[:, 0:1] * oc_ref[...].reshape(nrow, LANES) + gates[:, 1:2] * o_s + gates[:, 2:3] * o_w
    o_ref[...] = o.reshape(n_seq, N_HEADS, LANES)

    last = lax.broadcasted_iota(jnp.int32, (KV_W, LANES), 1) == LANES - 1
    for g in range(n_seq):
        shifted = pltpu.roll(win_ref[g], wlen - 1, 1)
        new_col = jnp.broadcast_to(wnew_ref[g], (LANES, KV_W)).T
        nwin_ref[g, :, 0:wlen - LANES] = shifted[:, 0:wlen - LANES]
        nwin_ref[g, :, wlen - LANES:wlen] = jnp.where(last, new_col, shifted[:, wlen - LANES:wlen])


def _attn_sample(page_table, cache_t, win_t, qbd, ns8, s_new, w_new, o_c, gates8):
    nb, n_pages = page_table.shape
    page = cache_t.shape[-1]
    n_seq = SAMPLE_GROUP
    per_b = lambda a: pl.BlockSpec((n_seq,) + a.shape[1:], lambda b, pt: (b,) + (0,) * (a.ndim - 1))
    grid_spec = pltpu.PrefetchScalarGridSpec(
        num_scalar_prefetch=1,
        grid=(nb // n_seq,),
        in_specs=[pl.BlockSpec(memory_space=pl.ANY)]
        + [per_b(a) for a in (win_t, qbd, ns8, s_new, w_new, o_c, gates8)],
        out_specs=(per_b(o_c), per_b(win_t)),
        scratch_shapes=[pltpu.VMEM((2, n_seq * n_pages, KV_W, page), F32),
                        pltpu.SemaphoreType.DMA((2,))],
    )
    return pl.pallas_call(
        functools.partial(_attn_sample_kernel, n_pages=n_pages, page=page, n_seq=n_seq),
        grid_spec=grid_spec,
        out_shape=(jax.ShapeDtypeStruct(o_c.shape, F32), jax.ShapeDtypeStruct(win_t.shape, F32)),
        compiler_params=_cparams(("arbitrary",)),
        name="attn_sample",
    )(page_table, cache_t, win_t, qbd, ns8, s_new, w_new, o_c, gates8)


def _prep_w_in(w_in):
    w_t = w_in.T
    o = D_ATTN + 3 * KV_W
    ng = GQA * N_BRANCH
    gl = w_t[o:o + N_KV_HEADS * ng].reshape(N_KV_HEADS, ng, w_t.shape[1])
    w_gates = jnp.pad(gl, ((0, 0), (0, GATE_ROWS - ng), (0, 0))).reshape(N_KV_HEADS * GATE_ROWS, w_t.shape[1])
    return w_t, w_gates


def _prep_compress(cmp_pe, cmp_w1, cmp_w2):
    r = CMP_BLOCK // CMP_STRIDE
    pe_t = jnp.tile(cmp_pe.reshape(2, r, CMP_STRIDE, 1, HEAD_DIM), (1, 1, 1, 1, N_KV_HEADS))
    w1 = cmp_w1.reshape(2, r, CMP_STRIDE, HEAD_DIM, CMP_HIDDEN)
    z1 = jnp.zeros_like(w1)
    top = jnp.concatenate([w1, z1], axis=-1)
    bot = jnp.concatenate([z1, w1], axis=-1)
    w1bd = jnp.concatenate([top, bot], axis=3)
    w1bd = w1bd.reshape(2, r, CMP_STRIDE * LANES, N_KV_HEADS * CMP_HIDDEN).astype(BF16)
    z2 = jnp.zeros_like(cmp_w2)
    w2bd = jnp.concatenate([jnp.concatenate([cmp_w2, z2], axis=-1),
                            jnp.concatenate([z2, cmp_w2], axis=-1)], axis=1).astype(BF16)
    return pe_t, w1bd, w2bd


def _kv_out(kv_t):
    n, _, rows = kv_t.shape
    return jnp.transpose(kv_t.reshape(n, 2, N_KV_HEADS, HEAD_DIM, rows), (0, 4, 1, 2, 3))[None]


def kernel(x_prompt, x_sample, cache_cmp_kv, cache_slc_kv, cache_win_kv, state_pool, page_table, norm_g, w_in,
           cmp_pe, cmp_w1, cmp_w2, pool_w, pool_scale, w_out, final_g):
    n, seq_len, d_model = x_prompt.shape
    nb = x_sample.shape[0]
    n_phys, page = cache_cmp_kv.shape[1], cache_cmp_kv.shape[2]
    n_pages = page_table.shape[1]
    past_len = n_pages * page
    wlen = cache_win_kv.shape[2]

    w_t, w_gates = _prep_w_in(w_in[0])
    pe_t, w1bd, w2bd = _prep_compress(cmp_pe[0], cmp_w1[0], cmp_w2[0])
    pool_w_b = pool_w[0].astype(BF16)
    w_out_b = w_out[0].astype(BF16)
    fg = final_g.reshape(1, d_model)

    (qt, crm, ckv_t, skv_t, wkv_t, ks, vst, kw, vwt, gates_t, sza, u, szp) = _project(
        x_prompt.reshape(n * seq_len, d_model), norm_g, w_t, w_gates, tm=512, seq_len=seq_len)
    r3 = lambda a: a.reshape(n, seq_len, a.shape[-1])
    kc, rhs_c = _compress_prompt(crm, n, pe_t, w1bd, w2bd)
    a3 = _attn_prompt(qt, gates_t, r3(sza), kc, rhs_c, ks, vst, kw, vwt)
    y_prompt = _out_prompt(a3, r3(u), r3(szp), x_prompt, pool_w_b, pool_scale, w_out_b, fg, tm=512)

    new_cmp_p = _kv_out(ckv_t)
    new_slc_p = _kv_out(skv_t)
    new_win_p = _kv_out(wkv_t[:, :, seq_len - min(WINDOW, seq_len):])
    new_pool_p = r3(u)[:, seq_len - POOL_STATE:][None]

    (qt_s, _, ckv_ts, skv_ts, wkv_ts, _, _, _, _, gates_ts, sza_s, u_s, szp_s) = _project(
        x_sample.reshape(nb, d_model), norm_g, w_t, w_gates, tm=nb, seq_len=nb)
    q_s = qt_s.T
    q5 = q_s.reshape(nb, N_KV_HEADS, GQA, 1, HEAD_DIM)
    eye = jnp.eye(N_KV_HEADS, dtype=q_s.dtype).reshape(1, N_KV_HEADS, 1, N_KV_HEADS, 1)
    qbd = (q5 * eye).reshape(nb, N_HEADS, LANES).astype(F32)

    to_pages = lambda c: jnp.transpose(c[0], (0, 2, 3, 4, 1)).reshape(n_phys, KV_W, page)
    o_c, imp8 = _cmp_sample(page_table, to_pages(cache_cmp_kv), qbd, pe_t, w1bd, w2bd)
    imp_g = jnp.transpose(imp8.reshape(nb * N_KV_HEADS, GQA, LANES), (1, 0, 2))
    n_blocks = -(-(past_len + 1) // SEL_BLOCK)
    notsel = _topk_sample(imp_g, t_pos=past_len, n_blocks=n_blocks)
    ns8 = jnp.repeat(notsel.reshape(nb, N_KV_HEADS, 1, LANES), GQA, axis=2).reshape(nb, N_HEADS, LANES)

    gates8 = jnp.transpose(gates_ts[:, :GQA * N_BRANCH, :], (2, 0, 1)).reshape(nb, N_HEADS, N_BRANCH)
    gates8 = jnp.pad(gates8, ((0, 0), (0, 0), (0, LANES - N_BRANCH)))
    win_t = jnp.transpose(cache_win_kv[0], (0, 2, 3, 4, 1)).reshape(nb, KV_W, wlen)
    s_new = skv_ts[0].T.reshape(nb, 1, KV_W)
    w_new = wkv_ts[0].T.reshape(nb, 1, KV_W)
    o8, nwin_t = _attn_sample(page_table, to_pages(cache_slc_kv), win_t, qbd, ns8, s_new, w_new, o_c, gates8)
    o5 = o8.reshape(nb, N_KV_HEADS, GQA, N_KV_HEADS, HEAD_DIM)
    o2 = jnp.stack([o5[:, k, :, k, :] for k in range(N_KV_HEADS)], axis=1).reshape(nb, D_ATTN)

    state_t = jnp.transpose(state_pool[0], (1, 0, 2))
    y_sample = _out_sample(o2, sza_s, u_s, state_t, szp_s, x_sample.reshape(nb, d_model),
                           pool_w_b, pool_scale, w_out_b, fg).reshape(nb, 1, d_model)

    kv_out_s = lambda t: jnp.transpose(t.reshape(2, N_KV_HEADS, HEAD_DIM, nb, 1), (3, 4, 0, 1, 2))[None]
    new_cmp_s = kv_out_s(ckv_ts)
    new_slc_s = kv_out_s(skv_ts)
    new_win_s = jnp.transpose(nwin_t.reshape(nb, 2, N_KV_HEADS, HEAD_DIM, wlen), (0, 4, 1, 2, 3))[None]
    new_pool_s = jnp.transpose(jnp.concatenate([state_t[1:], u_s[None]], axis=0), (1, 0, 2))[None]

    return (y_prompt, y_sample, new_cmp_p, new_slc_p, new_win_p, new_pool_p,
            new_cmp_s, new_slc_s, new_win_s, new_pool_s)
```
